```python
import math
import jax, jax.numpy as jnp
from jax import lax
import numpy as np

D_MODEL = 2048
BATCH = 4
SEQ = 2048
DEPTH = 4
DEC_BATCH = 128
DEC_SEQ = 4
PAST_LEN = 16384
PAGE_SIZE = 128

N_EVEN = (DEPTH + 1) // 2
N_ODD = DEPTH // 2
D_RET = D_MODEL // 2
H_RET = 4
DK_RET = D_RET // H_RET
DV_RET = D_RET // H_RET
RET_CHUNK = 128
ROPE_BASE = 10000.0
D_RG = D_MODEL // 2
RG_BLOCKS = 8
RG_BW = D_RG // RG_BLOCKS
RG_CONV = 4
RG_C = 8.0
D_HG = D_MODEL
HG_DK = 128
H_HG = D_HG // HG_DK
HG_DV = D_HG // H_HG
HG_CHUNK = 32
D_FF = 5632
FFN_CONV = 3
DN_ALPHA = (2 * DEPTH) ** 0.25
DN_BETA = (8 * DEPTH) ** -0.25
LN_EPS = 1e-5
NORM_EPS = 1e-6

EVEN_IN = 4 * D_RET + 2 * D_RG
ODD_IN = 4 * D_HG

kernel_name = "hybrid_retention_rglru_hgrn2_step"

F32 = jnp.float32


def _layer_norm(x, g, b):
    xf = x.astype(F32)
    mu = xf.mean(-1, keepdims=True)
    var = jnp.square(xf - mu).mean(-1, keepdims=True)
    return ((xf - mu) * lax.rsqrt(var + LN_EPS) * g.astype(F32) + b.astype(F32)).astype(x.dtype)


def _rms_norm(x):
    xf = x.astype(F32)
    return xf * lax.rsqrt(jnp.mean(xf * xf, axis=-1, keepdims=True) + NORM_EPS)


def _rotary(x, pos):
    half = x.shape[-1] // 2
    inv = ROPE_BASE ** (-jnp.arange(half, dtype=F32) / half)
    ang = pos.astype(F32)[:, None] * inv[None, :]
    cos = jnp.cos(ang)[None, :, None, :]
    sin = jnp.sin(ang)[None, :, None, :]
    xf = x.astype(F32)
    x1, x2 = xf[..., :half], xf[..., half:]
    return jnp.concatenate([x1 * cos - x2 * sin, x2 * cos + x1 * sin], axis=-1)


def _causal_dwconv(x, buf, w, b):
    T = x.shape[1]
    W = w.shape[0]
    xp = jnp.concatenate([buf.astype(x.dtype), x], axis=1)
    y = xp[:, 0:T] * w[0] + b
    for j in range(1, W):
        y = y + xp[:, j:j + T] * w[j]
    return y, xp[:, xp.shape[1] - (W - 1):]


def _chunk_len(T, chunk):
    return chunk if T % chunk == 0 else T


def _to_chunks(a, N, C):
    B, T, H, d = a.shape
    return a.reshape(B, N, C, H, d).transpose(1, 0, 3, 2, 4)


def _from_chunks(o):
    N, B, H, C, d = o.shape
    return o.transpose(1, 0, 3, 2, 4).reshape(B, N * C, H, d)


def _retention(q, k, v, S0, chunk):
    B, T, H, _ = q.shape
    C = _chunk_len(T, chunk)
    N = T // C
    lg = jnp.log(1.0 - 2.0 ** (-5.0 - jnp.arange(H, dtype=F32)))
    idx = jnp.arange(C, dtype=F32)
    diff = idx[:, None] - idx[None, :]
    dmask = jnp.where(diff[None] >= 0, jnp.exp(jnp.maximum(diff, 0.0)[None] * lg[:, None, None]), 0.0)
    xi = jnp.exp((idx[None, :] + 1.0) * lg[:, None])[None, :, :, None]
    zeta = jnp.exp((C - 1.0 - idx[None, :]) * lg[:, None])[None, :, :, None]
    cdec = jnp.exp(C * lg)[None, :, None, None]

    def step(S, inp):
        qc, kc, vc = inp
        inner = jnp.einsum('bhtk,bhsk->bhts', qc, kc) * dmask[None]
        o = jnp.einsum('bhts,bhsv->bhtv', inner, vc) + jnp.einsum('bhtk,bhkv->bhtv', qc, S) * xi
        S = S * cdec + jnp.einsum('bhsk,bhsv->bhkv', kc * zeta, vc)
        return S, o

    S, o = lax.scan(step, S0, (_to_chunks(q, N, C), _to_chunks(k, N, C), _to_chunks(v, N, C)))
    return _from_chunks(o), S


def _gla(q, k, v, logf, S0, chunk):
    B, T, H, _ = q.shape
    C = _chunk_len(T, chunk)
    N = T // C
    causal = jnp.tril(jnp.ones((C, C), dtype=bool))

    def step(S, inp):
        qc, kc, vc, gc = inp
        b = jnp.cumsum(gc, axis=2)
        diff = b[:, :, :, None, :] - b[:, :, None, :, :]
        decay = jnp.exp(jnp.where(causal[None, None, :, :, None], diff, -jnp.inf))
        A = jnp.einsum('bhtk,bhsk,bhtsk->bhts', qc, kc, decay)
        o = jnp.einsum('bhts,bhsv->bhtv', A, vc) + jnp.einsum('bhtk,bhkv->bhtv', qc * jnp.exp(b), S)
        b_last = b[:, :, -1:, :]
        S = S * jnp.exp(b_last[:, :, 0, :])[..., None] + jnp.einsum('bhsk,bhsv->bhkv', kc * jnp.exp(b_last - b), vc)
        return S, o

    S, o = lax.scan(step, S0, (_to_chunks(q, N, C), _to_chunks(k, N, C), _to_chunks(v, N, C), _to_chunks(logf, N, C)))
    return _from_chunks(o), S


def _rglru(xc, pos, h0, wa, ba, wx, bx, lam):
    B, T, D = xc.shape
    xb = xc.reshape(B, T, RG_BLOCKS, RG_BW)
    r = jax.nn.sigmoid((jnp.einsum('btni,nij->btnj', xb, wa).reshape(B, T, D) + ba).astype(F32))
    i = jax.nn.sigmoid((jnp.einsum('btni,nij->btnj', xb, wx).reshape(B, T, D) + bx).astype(F32))
    log_a = -RG_C * r * jax.nn.softplus(-lam.astype(F32))
    a = jnp.exp(log_a)
    mult = jnp.sqrt(-jnp.expm1(2.0 * log_a))
    mult = jnp.where((pos == 0)[None, :, None], 1.0, mult)
    bterm = xc.astype(F32) * i * mult
    bterm = bterm.at[:, 0].add(a[:, 0] * h0.astype(F32))

    def comb(l, rr):
        return (l[0] * rr[0], rr[0] * l[1] + rr[1])

    _, h = lax.associative_scan(comb, (a, bterm), axis=1)
    return h, h[:, -1]


def _even_mixer(x, pos, S_ret, h_rg, buf_rg, w_in, w_out, cw, cb, wa, ba, wx, bx, lam):
    B, T, _ = x.shape
    dt = x.dtype
    q, k, v, g, xr, gr = jnp.split(x @ w_in, [D_RET, 2 * D_RET, 3 * D_RET, 4 * D_RET, 4 * D_RET + D_RG], axis=-1)
    q = _rotary(q.reshape(B, T, H_RET, DK_RET), pos)
    k = _rotary(k.reshape(B, T, H_RET, DK_RET), pos) * (DK_RET ** -0.5)
    v = v.reshape(B, T, H_RET, DV_RET).astype(F32)
    o, S_new = _retention(q, k, v, S_ret.astype(F32), RET_CHUNK)
    o_ret = _rms_norm(o).reshape(B, T, D_RET) * jax.nn.silu(g.astype(F32))
    xc, buf_new = _causal_dwconv(xr, buf_rg, cw, cb)
    h, h_last = _rglru(xc, pos, h_rg, wa, ba, wx, bx, lam)
    o_rg = h * jax.nn.gelu(gr.astype(F32))
    mixed = jnp.concatenate([o_ret, o_rg], axis=-1).astype(dt)
    return (mixed @ w_out, S_new.astype(S_ret.dtype), h_last.astype(h_rg.dtype), buf_new.astype(buf_rg.dtype))


def _odd_mixer(x, S_hg, w_in, w_out, norm_g, lb):
    B, T, _ = x.shape
    dt = x.dtype
    q, f, i, g = jnp.split(x @ w_in, 4, axis=-1)
    q = jax.nn.silu(q.astype(F32))
    fg = lb.astype(F32) + (1.0 - lb.astype(F32)) * jax.nn.sigmoid(f.astype(F32))
    logf = jnp.log(fg)
    kk = 1.0 - fg
    o, S_new = _gla(q.reshape(B, T, H_HG, HG_DK), kk.reshape(B, T, H_HG, HG_DK),
                    i.astype(F32).reshape(B, T, H_HG, HG_DV), logf.reshape(B, T, H_HG, HG_DK),
                    S_hg.astype(F32), HG_CHUNK)
    o = (_rms_norm(o) * norm_g.astype(F32)).reshape(B, T, D_HG) * jax.nn.sigmoid(g.astype(F32))
    return o.astype(dt) @ w_out, S_new.astype(S_hg.dtype)


def _conv_ffn(x, buf, w_up, cw, cb, w_down):
    u, v = jnp.split(x @ w_up, 2, axis=-1)
    uc, buf_new = _causal_dwconv(u, buf, cw, cb)
    return (jax.nn.gelu(uc) * v) @ w_down, buf_new.astype(buf.dtype)


def _trunk(x, pos, s_ret, s_h, s_conv, s_hg, s_ffc, w, lb):
    n_ret, n_h, n_conv, n_hg, n_ffc = [], [], [], [], []
    for l in range(DEPTH):
        if l % 2 == 0:
            e = l // 2
            mix, sr, sh, sc = _even_mixer(x, pos, s_ret[e], s_h[e], s_conv[e], w['ev_w_in'][e], w['ev_w_out'][e],
                                          w['ev_rg_conv_w'][e], w['ev_rg_conv_b'][e], w['ev_rg_wa'][e], w['ev_rg_ba'][e],
                                          w['ev_rg_wx'][e], w['ev_rg_bx'][e], w['ev_rg_lambda'][e])
            n_ret.append(sr); n_h.append(sh); n_conv.append(sc)
        else:
            o = l // 2
            mix, sg = _odd_mixer(x, s_hg[o], w['od_w_in'][o], w['od_w_out'][o], w['od_norm_g'][o], lb[o])
            n_hg.append(sg)
        x = _layer_norm(DN_ALPHA * x + mix, w['ln_g'][l, 0], w['ln_b'][l, 0])
        f, fc = _conv_ffn(x, s_ffc[l], w['ffn_w_up'][l], w['ffn_conv_w'][l], w['ffn_conv_b'][l], w['ffn_w_down'][l])
        n_ffc.append(fc)
        x = _layer_norm(DN_ALPHA * x + f, w['ln_g'][l, 1], w['ln_b'][l, 1])
    return x, (jnp.stack(n_ret), jnp.stack(n_h), jnp.stack(n_conv), jnp.stack(n_hg), jnp.stack(n_ffc))


def setup_inputs(seed: int = 0) -> dict:
    key = jax.random.key(seed)
    ks = jax.random.split(key, 32)
    nrm = lambda k, s, sc: jax.random.normal(k, s, F32) * sc
    a0 = jax.random.uniform(ks[13], (N_EVEN, D_RG), F32, minval=0.9, maxval=0.999)
    s0 = a0 ** (1.0 / RG_C)
    lam = jnp.log(s0) - jnp.log1p(-s0)
    return {
        'x_prompt': nrm(ks[0], (BATCH, SEQ, D_MODEL), 1.0),
        'x_sample': nrm(ks[1], (DEC_BATCH, DEC_SEQ, D_MODEL), 1.0),
        'state_ret': nrm(ks[2], (N_EVEN, DEC_BATCH, H_RET, DK_RET, DV_RET), 0.1),
        'state_rglru_h': nrm(ks[3], (N_EVEN, DEC_BATCH, D_RG), 0.5),
        'state_rglru_conv': nrm(ks[4], (N_EVEN, DEC_BATCH, RG_CONV - 1, D_RG), 1.0),
        'state_hgrn': nrm(ks[5], (N_ODD, DEC_BATCH, H_HG, HG_DK, HG_DV), 0.1),
        'state_ffn_conv': nrm(ks[6], (DEPTH, DEC_BATCH, FFN_CONV - 1, D_FF), 1.0),
        'ev_w_in': nrm(ks[7], (N_EVEN, D_MODEL, EVEN_IN), D_MODEL ** -0.5),
        'ev_w_out': nrm(ks[8], (N_EVEN, D_RET + D_RG, D_MODEL), (D_RET + D_RG) ** -0.5 * DN_BETA),
        'ev_rg_conv_w': nrm(ks[9], (N_EVEN, RG_CONV, D_RG), RG_CONV ** -0.5),
        'ev_rg_conv_b': nrm(ks[10], (N_EVEN, D_RG), 0.01),
        'ev_rg_wa': nrm(ks[11], (N_EVEN, RG_BLOCKS, RG_BW, RG_BW), RG_BW ** -0.5),
        'ev_rg_ba': nrm(ks[12], (N_EVEN, D_RG), 0.01),
        'ev_rg_wx': nrm(ks[14], (N_EVEN, RG_BLOCKS, RG_BW, RG_BW), RG_BW ** -0.5),
        'ev_rg_bx': nrm(ks[15], (N_EVEN, D_RG), 0.01),
        'ev_rg_lambda': lam,
        'od_w_in': nrm(ks[16], (N_ODD, D_MODEL, ODD_IN), D_MODEL ** -0.5),
        'od_w_out': nrm(ks[17], (N_ODD, D_HG, D_MODEL), D_HG ** -0.5 * DN_BETA),
        'od_norm_g': 1.0 + nrm(ks[18], (N_ODD, HG_DV), 0.01),
        'od_lb_logits': nrm(ks[19], (N_ODD, D_HG), 0.1),
        'ln_g': 1.0 + nrm(ks[20], (DEPTH, 2, D_MODEL), 0.01),
        'ln_b': nrm(ks[21], (DEPTH, 2, D_MODEL), 0.01),
        'ffn_w_up': nrm(ks[22], (DEPTH, D_MODEL, 2 * D_FF), D_MODEL ** -0.5),
        'ffn_conv_w': nrm(ks[23], (DEPTH, FFN_CONV, D_FF), FFN_CONV ** -0.5),
        'ffn_conv_b': nrm(ks[24], (DEPTH, D_FF), 0.01),
        'ffn_w_down': nrm(ks[25], (DEPTH, D_FF, D_MODEL), D_FF ** -0.5 * DN_BETA),
    }


def reference(x_prompt, x_sample, state_ret, state_rglru_h, state_rglru_conv, state_hgrn, state_ffn_conv,
              ev_w_in, ev_w_out, ev_rg_conv_w, ev_rg_conv_b, ev_rg_wa, ev_rg_ba, ev_rg_wx, ev_rg_bx, ev_rg_lambda,
              od_w_in, od_w_out, od_norm_g, od_lb_logits, ln_g, ln_b, ffn_w_up, ffn_conv_w, ffn_conv_b, ffn_w_down):
    w = {'ev_w_in': ev_w_in, 'ev_w_out': ev_w_out, 'ev_rg_conv_w': ev_rg_conv_w, 'ev_rg_conv_b': ev_rg_conv_b,
         'ev_rg_wa': ev_rg_wa, 'ev_rg_ba': ev_rg_ba, 'ev_rg_wx': ev_rg_wx, 'ev_rg_bx': ev_rg_bx,
         'ev_rg_lambda': ev_rg_lambda, 'od_w_in': od_w_in, 'od_w_out': od_w_out, 'od_norm_g': od_norm_g,
         'ln_g': ln_g, 'ln_b': ln_b, 'ffn_w_up': ffn_w_up, 'ffn_conv_w': ffn_conv_w,
         'ffn_conv_b': ffn_conv_b, 'ffn_w_down': ffn_w_down}
    p = jax.nn.softmax(od_lb_logits.astype(F32), axis=0)
    lb = jnp.cumsum(p, axis=0) - p[0:1]

    Bp, Tp, _ = x_prompt.shape
    dt = x_prompt.dtype
    pos_p = jnp.arange(Tp, dtype=jnp.int32)
    pos_s = PAST_LEN + jnp.arange(x_sample.shape[1], dtype=jnp.int32)
    z_ret = jnp.zeros((N_EVEN, Bp, H_RET, DK_RET, DV_RET), dt)
    z_h = jnp.zeros((N_EVEN, Bp, D_RG), dt)
    z_conv = jnp.zeros((N_EVEN, Bp, RG_CONV - 1, D_RG), dt)
    z_hg = jnp.zeros((N_ODD, Bp, H_HG, HG_DK, HG_DV), dt)
    z_ffc = jnp.zeros((DEPTH, Bp, FFN_CONV - 1, D_FF), dt)

    y_prompt, (rp, hp, cp, gp, fp) = _trunk(x_prompt, pos_p, z_ret, z_h, z_conv, z_hg, z_ffc, w, lb)
    y_sample, (rs, hs, cs, gs, fs) = _trunk(x_sample, pos_s, state_ret, state_rglru_h, state_rglru_conv,
                                            state_hgrn, state_ffn_conv, w, lb)
    return (y_prompt, y_sample, rp, rs, hp, hs, cp, cs, gp, gs, fp, fs)
```

```python
import functools

import numpy as np
import jax
import jax.numpy as jnp
from jax import lax
from jax.experimental import pallas as pl
from jax.experimental.pallas import tpu as pltpu

F32 = jnp.float32
BF16 = jnp.bfloat16

LN_EPS = 1e-5
NORM_EPS = 1e-6
ROPE_BASE = 10000.0
RG_C = 8.0
PAST_LEN = 16384
LANES = 128
SUBLANES = 8
ROW_TILE = 512
LN_ROW_TILE = 256
RET_CHUNK = 128
RET_SAMPLE_GROUP = 8
RG_TIME_TILE = 256
GLA_BLOCK = 128
GLA_HEADS_PER_STEP = 2
GLA_SAMPLE_GROUP = 4
MIB = 1024 * 1024


def _cparams(n_axes, vmem_mib=48):
    return pltpu.CompilerParams(dimension_semantics=("arbitrary",) * n_axes,
                                vmem_limit_bytes=vmem_mib * MIB)


def _dot(a, b):
    return jnp.dot(a.astype(BF16), b.astype(BF16), preferred_element_type=F32)


def _dot_nt(a, b):
    return lax.dot_general(a.astype(BF16), b.astype(BF16), (((1,), (1,)), ((), ())),
                           preferred_element_type=F32)


def _dot_tn(a, b):
    return lax.dot_general(a.astype(BF16), b.astype(BF16), (((0,), (0,)), ((), ())),
                           preferred_element_type=F32)


def _silu(x):
    return x * jax.nn.sigmoid(x)


def _mm_body(x_ref, w_ref, o_ref):
    o_ref[...] = jnp.dot(x_ref[...], w_ref[...], preferred_element_type=F32).astype(o_ref.dtype)


def _matmul(xb, wb, tm, tn):
    m, k = xb.shape
    n = wb.shape[1]
    assert m % tm == 0 and n % tn == 0
    return pl.pallas_call(
        _mm_body,
        grid=(n // tn, m // tm),
        in_specs=[pl.BlockSpec((tm, k), lambda j, i: (i, 0)),
                  pl.BlockSpec((k, tn), lambda j, i: (0, j))],
        out_specs=pl.BlockSpec((tm, tn), lambda j, i: (i, j)),
        out_shape=jax.ShapeDtypeStruct((m, n), F32),
        compiler_params=_cparams(2),
        name="in_proj",
    )(xb, wb)


def _proj_ln_body(*refs, n_parts, alpha):
    a_refs = refs[:n_parts]
    w_refs = refs[n_parts:2 * n_parts]
    x_ref, g_ref, b_ref, o_ref, ob_ref = refs[2 * n_parts:]
    acc = None
    for a_ref, w_ref in zip(a_refs, w_refs):
        d = jnp.dot(a_ref[...], w_ref[...], preferred_element_type=F32)
        acc = d if acc is None else acc + d
    y = alpha * x_ref[...] + acc
    mu = jnp.mean(y, axis=-1, keepdims=True)
    d = y - mu
    var = jnp.mean(d * d, axis=-1, keepdims=True)
    out = d * lax.rsqrt(var + LN_EPS) * g_ref[...] + b_ref[...]
    o_ref[...] = out
    ob_ref[...] = out.astype(BF16)


def _proj_ln(parts, weights, x, g, b, alpha, tm):
    m, d = x.shape
    assert m % tm == 0
    n_parts = len(parts)
    in_specs = [pl.BlockSpec((tm, p.shape[1]), lambda i: (i, 0)) for p in parts]
    in_specs += [pl.BlockSpec(w.shape, lambda i: (0, 0), pipeline_mode=pl.Buffered(1)) for w in weights]
    in_specs += [pl.BlockSpec((tm, d), lambda i: (i, 0)),
                 pl.BlockSpec((1, d), lambda i: (0, 0)),
                 pl.BlockSpec((1, d), lambda i: (0, 0))]
    return pl.pallas_call(
        functools.partial(_proj_ln_body, n_parts=n_parts, alpha=alpha),
        grid=(m // tm,),
        in_specs=in_specs,
        out_specs=[pl.BlockSpec((tm, d), lambda i: (i, 0)), pl.BlockSpec((tm, d), lambda i: (i, 0))],
        out_shape=[jax.ShapeDtypeStruct((m, d), F32), jax.ShapeDtypeStruct((m, d), BF16)],
        compiler_params=_cparams(1, vmem_mib=56),
        name="proj_ln",
    )(*parts, *weights, x, g.reshape(1, d), b.reshape(1, d))


def _ffn_up_body(x_ref, wu_ref, wv_ref, cw_ref, cb_ref, e1_ref, e2_ref, h_ref, tail_ref, us_ref,
                 carry_ref, *, n_prompt_tiles, tiles_per_seq, dec_seq):
    i = pl.program_id(1)
    x = x_ref[...]
    u = jnp.dot(x, wu_ref[...], preferred_element_type=F32)
    v = jnp.dot(x, wv_ref[...], preferred_element_type=F32)
    tm = u.shape[0]
    row = lax.broadcasted_iota(jnp.int32, u.shape, 0)
    r1 = pltpu.roll(u, 1, axis=0)
    r2 = pltpu.roll(u, 2, axis=0)

    def finish(u1, u2):
        uc = u2 * cw_ref[0:1, :] + cb_ref[...]
        uc = uc + u1 * cw_ref[1:2, :]
        uc = uc + u * cw_ref[2:3, :]
        h_ref[...] = (jax.nn.gelu(uc) * v).astype(h_ref.dtype)

    @pl.when(i < n_prompt_tiles)
    def _():
        @pl.when(i % tiles_per_seq == 0)
        def _():
            carry_ref[...] = jnp.zeros_like(carry_ref)
        c_last = carry_ref[SUBLANES - 1:SUBLANES, :]
        c_prev = carry_ref[SUBLANES - 2:SUBLANES - 1, :]
        u1 = jnp.where(row == 0, c_last, r1)
        u2 = jnp.where(row == 0, c_prev, jnp.where(row == 1, c_last, r2))
        finish(u1, u2)

    @pl.when(i >= n_prompt_tiles)
    def _():
        t = row % dec_seq
        finish(jnp.where(t >= 1, r1, e1_ref[...]), jnp.where(t >= 2, r2, e2_ref[...]))
        us_ref[...] = u

    carry_ref[...] = u[tm - SUBLANES:, :]
    tail_ref[...] = u[tm - SUBLANES:, :]


def _ffn_up(xb, w_up, cw, cb, e1, e2, *, m_prompt, seq, dec_seq, tm, tn):
    m, k = xb.shape
    d_ff = w_up.shape[1] // 2
    m_s = m - m_prompt
    assert m_prompt % tm == 0 and m_s % tm == 0 and seq % tm == 0 and d_ff % tn == 0
    assert tm % dec_seq == 0 and dec_seq >= 2 and cw.shape[0] == 3
    npt = m_prompt // tm
    nj = d_ff // tn
    srow = lambda j, i: (jnp.maximum(i - npt, 0), j)
    return pl.pallas_call(
        functools.partial(_ffn_up_body, n_prompt_tiles=npt, tiles_per_seq=seq // tm, dec_seq=dec_seq),
        grid=(nj, m // tm),
        in_specs=[pl.BlockSpec((tm, k), lambda j, i: (i, 0)),
                  pl.BlockSpec((k, tn), lambda j, i: (0, j)),
                  pl.BlockSpec((k, tn), lambda j, i: (0, j + nj)),
                  pl.BlockSpec((3, tn), lambda j, i: (0, j)),
                  pl.BlockSpec((1, tn), lambda j, i: (0, j)),
                  pl.BlockSpec((tm, tn), srow),
                  pl.BlockSpec((tm, tn), srow)],
        out_specs=[pl.BlockSpec((tm, tn), lambda j, i: (i, j)),
                   pl.BlockSpec((SUBLANES, tn), lambda j, i: (i, j)),
                   pl.BlockSpec((tm, tn), srow)],
        out_shape=[jax.ShapeDtypeStruct((m, d_ff), BF16),
                   jax.ShapeDtypeStruct((m // tm * SUBLANES, d_ff), F32),
                   jax.ShapeDtypeStruct((m_s, d_ff), F32)],
        scratch_shapes=[pltpu.VMEM((SUBLANES, tn), F32)],
        compiler_params=_cparams(2),
        name="ffn_up",
    )(xb, w_up, w_up, cw, cb.reshape(1, d_ff), e1, e2)


def _ret_body(q_ref, k_ref, v_ref, g_ref, cos_ref, sin_ref, dm_ref, xi_ref, zeta_ref, cdec_ref, s0_ref,
              o_ref, so_ref, s_scr, q_scr, kz_scr, o_scr, *, n_seq, chunk, n_chunks, scale):
    n = pl.program_id(2)

    @pl.when(n == 0)
    def _():
        s_scr[...] = s0_ref[...]

    half = q_ref.shape[1] // 2
    cos = cos_ref[...]
    sin = sin_ref[...]

    def rot(x):
        x1 = x[:, :half]
        x2 = x[:, half:]
        return jnp.concatenate([x1 * cos - x2 * sin, x2 * cos + x1 * sin], axis=-1)

    q = rot(q_ref[...])
    k = rot(k_ref[...]) * scale
    vb = v_ref[...].astype(BF16)
    inner = _dot_nt(q, k) * dm_ref[...]
    o_scr[...] = _dot(inner, vb)
    q_scr[...] = q
    kz_scr[...] = k * zeta_ref[...]
    cdec = cdec_ref[...]
    for gi in range(n_seq):
        rows = pl.ds(gi * chunk, chunk)
        s = s_scr[gi]
        o_scr[rows, :] += _dot(q_scr[rows, :], s) * xi_ref[rows, :]
        s_scr[gi] = s * cdec + _dot_tn(kz_scr[rows, :], v_ref[rows, :])
    o = o_scr[...]
    gate = g_ref[...]
    o = o * lax.rsqrt(jnp.mean(o * o, axis=-1, keepdims=True) + NORM_EPS) * _silu(gate)
    o_ref[...] = o.astype(o_ref.dtype)

    @pl.when(n == n_chunks - 1)
    def _():
        so_ref[...] = s_scr[...]


def _ret_tables(n_heads, chunk, n_seq):
    lg = jnp.log(1.0 - 2.0 ** (-5.0 - jnp.arange(n_heads, dtype=F32)))
    idx = jnp.arange(chunk, dtype=F32)
    diff = idx[:, None] - idx[None, :]
    dmask = jnp.where(diff[None] >= 0, jnp.exp(jnp.maximum(diff, 0.0)[None] * lg[:, None, None]), 0.0)
    eye = jnp.eye(n_seq, dtype=F32)
    dm = jnp.einsum('ab,hts->hatbs', eye, dmask).reshape(n_heads, n_seq * chunk, n_seq * chunk)
    xi = jnp.tile(jnp.exp((idx[None, :] + 1.0) * lg[:, None]), (1, n_seq))[:, :, None]
    zeta = jnp.tile(jnp.exp((chunk - 1.0 - idx[None, :]) * lg[:, None]), (1, n_seq))[:, :, None]
    cdec = jnp.exp(chunk * lg)[:, None, None]
    return dm, xi, zeta, cdec


def _rope_tables(pos, half):
    inv = ROPE_BASE ** (-jnp.arange(half, dtype=F32) / half)
    ang = pos.astype(F32)[:, None] * inv[None, :]
    return jnp.cos(ang), jnp.sin(ang)


def _retention(p, s0, pos, *, row0, n_batch, n_seq, chunk, n_chunks, n_heads, dk):
    rows = n_seq * chunk
    assert row0 % rows == 0
    rb0 = row0 // rows
    dm, xi, zeta, cdec = _ret_tables(n_heads, chunk, n_seq)
    cos, sin = _rope_tables(pos, dk // 2)
    if n_seq > 1:
        assert n_chunks == 1
        cos = jnp.tile(cos, (n_seq, 1))
        sin = jnp.tile(sin, (n_seq, 1))
    hq = n_heads

    def col(off):
        return lambda b, h, n: (rb0 + b * n_chunks + n, off * hq + h)

    tab = lambda b, h, n: (h, 0, 0)
    state_spec = pl.BlockSpec((n_seq, None, dk, dk), lambda b, h, n: (b, h, 0, 0))
    total_rows = n_batch * n_chunks * rows
    return pl.pallas_call(
        functools.partial(_ret_body, n_seq=n_seq, chunk=chunk, n_chunks=n_chunks, scale=dk ** -0.5),
        grid=(n_batch, n_heads, n_chunks),
        in_specs=[pl.BlockSpec((rows, dk), col(0)), pl.BlockSpec((rows, dk), col(1)),
                  pl.BlockSpec((rows, dk), col(2)), pl.BlockSpec((rows, dk), col(3)),
                  pl.BlockSpec((rows, dk // 2), lambda b, h, n: (n, 0)),
                  pl.BlockSpec((rows, dk // 2), lambda b, h, n: (n, 0)),
                  pl.BlockSpec((None, rows, rows), tab),
                  pl.BlockSpec((None, rows, 1), tab),
                  pl.BlockSpec((None, rows, 1), tab),
                  pl.BlockSpec((None, 1, 1), tab),
                  state_spec],
        out_specs=[pl.BlockSpec((rows, dk), lambda b, h, n: (b * n_chunks + n, h)), state_spec],
        out_shape=[jax.ShapeDtypeStruct((total_rows, n_heads * dk), BF16),
                   jax.ShapeDtypeStruct((n_batch * n_seq, n_heads, dk, dk), F32)],
        scratch_shapes=[pltpu.VMEM((n_seq, dk, dk), F32), pltpu.VMEM((rows, dk), F32),
                        pltpu.VMEM((rows, dk), F32), pltpu.VMEM((rows, dk), F32)],
        compiler_params=_cparams(3),
        name="retention",
    )(p, p, p, p, cos, sin, dm, xi, zeta, cdec, s0)


def _rg_gates(xc, wa_ref, wx_ref, ba, bx, lam):
    nb, bw, _ = wa_ref.shape
    ra, ia = [], []
    for blk in range(nb):
        xb = xc[:, blk * bw:(blk + 1) * bw].astype(BF16)
        ra.append(jnp.dot(xb, wa_ref[blk], preferred_element_type=F32))
        ia.append(jnp.dot(xb, wx_ref[blk], preferred_element_type=F32))
    r = jax.nn.sigmoid(jnp.concatenate(ra, axis=-1) + ba)
    i = jax.nn.sigmoid(jnp.concatenate(ia, axis=-1) + bx)
    log_a = -RG_C * r * jax.nn.softplus(-lam)
    a = jnp.exp(log_a)
    mult = jnp.sqrt(-jnp.tanh(log_a) * (a * a + 1.0))
    return a, mult, i


def _scan_rows(a, b, row_in_seg, seg_len):
    s = 1
    while s < seg_len:
        a_sh = pltpu.roll(a, s, axis=0)
        b_sh = pltpu.roll(b, s, axis=0)
        m = row_in_seg >= s
        b = jnp.where(m, a * b_sh + b, b)
        a = jnp.where(m, a * a_sh, a)
        s *= 2
    return a, b


def _rg_prompt_body(xr_ref, gr_ref, cw_ref, cb_ref, wa_ref, wx_ref, ba_ref, bx_ref, lam_ref,
                    o_ref, hl_ref, xt_ref, h_scr, x_scr, *, n_tiles):
    n = pl.program_id(1)

    @pl.when(n == 0)
    def _():
        h_scr[...] = jnp.zeros_like(h_scr)
        x_scr[...] = jnp.zeros_like(x_scr)

    xr = xr_ref[...]
    tc = xr.shape[0]
    row = lax.broadcasted_iota(jnp.int32, xr.shape, 0)
    prev = x_scr[...]
    n_tap = cw_ref.shape[0]
    xc = None
    for j in range(n_tap):
        d = n_tap - 1 - j
        if d == 0:
            xd = xr
        else:
            prev_d = jnp.tile(pltpu.roll(prev, d, axis=0), (tc // SUBLANES, 1))
            xd = jnp.where(row < d, prev_d, pltpu.roll(xr, d, axis=0))
        term = xd * cw_ref[j:j + 1, :]
        xc = term + cb_ref[...] if xc is None else xc + term
    a, mult, gate_i = _rg_gates(xc, wa_ref, wx_ref, ba_ref[...], bx_ref[...], lam_ref[...])
    mult = jnp.where(row + n * tc == 0, 1.0, mult)
    bterm = xc * gate_i * mult
    a_cum, b_cum = _scan_rows(a, bterm, row, tc)
    h = a_cum * h_scr[...] + b_cum
    o_ref[...] = (h * jax.nn.gelu(gr_ref[...])).astype(o_ref.dtype)
    h_scr[...] = h[tc - 1:tc, :]
    x_scr[...] = xr[tc - SUBLANES:, :]

    @pl.when(n == n_tiles - 1)
    def _():
        hl_ref[...] = h[tc - 1:tc, :]
        xt_ref[...] = xr[tc - SUBLANES:, :]


def _rg_prompt(p, cw, cb, wa, wx, ba, bx, lam, *, n_batch, seq, col0, d_rg, total_rows):
    tc = min(RG_TIME_TILE, seq)
    assert seq % tc == 0 and col0 % d_rg == 0 and (tc & (tc - 1)) == 0
    nt = seq // tc
    cb0 = col0 // d_rg
    vec = lambda a: a.reshape(1, d_rg)
    full = lambda a: pl.BlockSpec(a.shape, lambda b, n: (0,) * a.ndim)
    args = (cw, vec(cb), wa, wx, vec(ba), vec(bx), vec(lam))
    return pl.pallas_call(
        functools.partial(_rg_prompt_body, n_tiles=nt),
        grid=(n_batch, nt),
        in_specs=[pl.BlockSpec((tc, d_rg), lambda b, n: (b * nt + n, cb0)),
                  pl.BlockSpec((tc, d_rg), lambda b, n: (b * nt + n, cb0 + 1))] + [full(a) for a in args],
        out_specs=[pl.BlockSpec((tc, d_rg), lambda b, n: (b * nt + n, 0)),
                   pl.BlockSpec((None, 1, d_rg), lambda b, n: (b, 0, 0)),
                   pl.BlockSpec((None, SUBLANES, d_rg), lambda b, n: (b, 0, 0))],
        out_shape=[jax.ShapeDtypeStruct((total_rows, d_rg), BF16),
                   jax.ShapeDtypeStruct((n_batch, 1, d_rg), F32),
                   jax.ShapeDtypeStruct((n_batch, SUBLANES, d_rg), F32)],
        scratch_shapes=[pltpu.VMEM((1, d_rg), F32), pltpu.VMEM((SUBLANES, d_rg), F32)],
        compiler_params=_cparams(2),
        name="rglru_prompt",
    )(p, p, *args)


def _rg_sample_body(xr_ref, gr_ref, e_ref, h0_ref, cw_ref, cb_ref, wa_ref, wx_ref, ba_ref, bx_ref, lam_ref,
                    o_ref, h_ref, *, dec_seq, first_pos):
    xr = xr_ref[...]
    row = lax.broadcasted_iota(jnp.int32, xr.shape, 0)
    t = row % dec_seq
    n_tap = cw_ref.shape[0]
    xc = None
    for j in range(n_tap):
        d = n_tap - 1 - j
        xd = xr if d == 0 else jnp.where(t >= d, pltpu.roll(xr, d, axis=0), e_ref[d - 1])
        term = xd * cw_ref[j:j + 1, :]
        xc = term + cb_ref[...] if xc is None else xc + term
    a, mult, gate_i = _rg_gates(xc, wa_ref, wx_ref, ba_ref[...], bx_ref[...], lam_ref[...])
    if first_pos == 0:
        mult = jnp.where(t == 0, 1.0, mult)
    bterm = xc * gate_i * mult + jnp.where(t == 0, a * h0_ref[...], 0.0)
    _, h = _scan_rows(a, bterm, t, dec_seq)
    h_ref[...] = h
    o_ref[...] = (h * jax.nn.gelu(gr_ref[...])).astype(o_ref.dtype)


def _rg_sample(p, e_prev, h0_rows, cw, cb, wa, wx, ba, bx, lam, *, row0, m_s, dec_seq, col0, d_rg):
    tr = min(128, m_s)
    assert m_s % tr == 0 and row0 % tr == 0 and tr % dec_seq == 0 and (dec_seq & (dec_seq - 1)) == 0
    rb0 = row0 // tr
    cb0 = col0 // d_rg
    vec = lambda a: a.reshape(1, d_rg)
    full = lambda a: pl.BlockSpec(a.shape, lambda i: (0,) * a.ndim)
    args = (cw, vec(cb), wa, wx, vec(ba), vec(bx), vec(lam))
    n_prev = e_prev.shape[0]
    return pl.pallas_call(
        functools.partial(_rg_sample_body, dec_seq=dec_seq, first_pos=PAST_LEN),
        grid=(m_s // tr,),
        in_specs=[pl.BlockSpec((tr, d_rg), lambda i: (rb0 + i, cb0)),
                  pl.BlockSpec((tr, d_rg), lambda i: (rb0 + i, cb0 + 1)),
                  pl.BlockSpec((n_prev, tr, d_rg), lambda i: (0, i, 0)),
                  pl.BlockSpec((tr, d_rg), lambda i: (i, 0))] + [full(a) for a in args],
        out_specs=[pl.BlockSpec((tr, d_rg), lambda i: (i, 0)), pl.BlockSpec((tr, d_rg), lambda i: (i, 0))],
        out_shape=[jax.ShapeDtypeStruct((m_s, d_rg), BF16), jax.ShapeDtypeStruct((m_s, d_rg), F32)],
        compiler_params=_cparams(1),
        name="rglru_sample",
    )(p, p, e_prev, h0_rows, *args)


def _lower_bound(logit_rows, layer):
    m = logit_rows[0]
    for z in logit_rows[1:]:
        m = jnp.maximum(m, z)
    e = [jnp.exp(z - m) for z in logit_rows]
    tot = e[0]
    for x in e[1:]:
        tot = tot + x
    if layer == 0:
        return jnp.zeros_like(tot)
    num = e[1]
    for x in e[2:layer + 1]:
        num = num + x
    return num / tot


_GLA_VREGS = GLA_BLOCK // SUBLANES


def _gla_level_masks():
    r = np.arange(GLA_BLOCK)
    t = r // SUBLANES + _GLA_VREGS * (r % SUBLANES)
    tq, tk = t[:, None], t[None, :]
    n_levels = GLA_BLOCK.bit_length()
    masks = np.zeros((n_levels, GLA_BLOCK, GLA_BLOCK), np.float32)
    masks[0] = tq == tk
    for lvl in range(1, n_levels):
        gs = 1 << lvl
        masks[lvl] = (tq > tk) & (tq // gs == tk // gs) & (tq // (gs // 2) != tk // (gs // 2))
    return masks


def _gla_prompt_body(*refs, heads, layer, n_blocks):
    q_refs, f_refs, i_refs, g_refs = (refs[p * heads:(p + 1) * heads] for p in range(4))
    lbl_ref, ng_ref, msk_ref, o_ref, so_ref, st_scr, o_scr = refs[4 * heads:]
    n = pl.program_id(2)

    @pl.when(n == 0)
    def _():
        st_scr[...] = jnp.zeros_like(st_scr)

    nv = _GLA_VREGS
    sub = lax.broadcasted_iota(jnp.int32, (SUBLANES, LANES), 0)
    zero = jnp.zeros((SUBLANES, LANES), F32)
    for hh in range(heads):
        cols = slice(hh * LANES, (hh + 1) * LANES)

        def load(ref):
            return jnp.concatenate([ref[pl.ds(j, SUBLANES, stride=nv), :] for j in range(nv)], axis=0)

        def groups(x):
            return [x[SUBLANES * j:SUBLANES * (j + 1)] for j in range(nv)]

        lb = _lower_bound([lbl_ref[r:r + 1, cols] for r in range(lbl_ref.shape[0])], layer)
        q = _silu(load(q_refs[hh]))
        fg = lb + (1.0 - lb) * jax.nn.sigmoid(load(f_refs[hh]))
        kk = 1.0 - fg
        v = load(i_refs[hh])
        vb = v.astype(BF16)
        lf = groups(jnp.log(fg))

        c = [lf[0]]
        for j in range(1, nv):
            c.append(c[-1] + lf[j])
        tot = c[nv - 1]
        x = tot
        s = 1
        while s < SUBLANES:
            x = x + jnp.where(sub >= s, pltpu.roll(x, s, axis=0), 0.0)
            s *= 2
        before = x - tot
        bj = [cj + before for cj in c]
        b = jnp.concatenate(bj, axis=0)
        b_last = bj[nv - 1][SUBLANES - 1:SUBLANES, :]

        qj = groups(q)
        kj = groups(kk)
        a_mat = msk_ref[0] * _dot_nt(q, kk)
        lvl = 1
        gs = 2
        while gs <= nv:
            hs = gs // 2
            qd, kd = [], []
            for j in range(nv):
                ref = (j // gs) * gs + hs - 1
                if j % gs >= hs:
                    qd.append(qj[j] * jnp.exp(bj[j] - bj[ref]))
                    kd.append(zero)
                else:
                    kd.append(kj[j] if j == ref else kj[j] * jnp.exp(bj[ref] - bj[j]))
                    qd.append(zero)
            a_mat = a_mat + msk_ref[lvl] * _dot_nt(jnp.concatenate(qd, axis=0), jnp.concatenate(kd, axis=0))
            lvl += 1
            gs *= 2
        m = 2
        while m <= SUBLANES:
            src = sub - sub % m + (m // 2 - 1)
            ref = zero
            for s_src in range(m // 2 - 1, SUBLANES, m):
                row_b = jnp.broadcast_to(bj[nv - 1][s_src:s_src + 1, :], (SUBLANES, LANES))
                ref = jnp.where(src == s_src, row_b, ref)
            upper = (sub % m) >= (m // 2)
            qd, kd = [], []
            for j in range(nv):
                e = jnp.exp(-jnp.abs(bj[j] - ref))
                qd.append(jnp.where(upper, qj[j] * e, 0.0))
                kd.append(jnp.where(upper, 0.0, kj[j] * e))
            a_mat = a_mat + msk_ref[lvl] * _dot_nt(jnp.concatenate(qd, axis=0), jnp.concatenate(kd, axis=0))
            lvl += 1
            m *= 2

        st = st_scr[hh]
        o = _dot(a_mat, vb) + _dot_nt(q * jnp.exp(b), st)
        st_new = st * jnp.exp(b_last) + _dot_tn(vb, kk * jnp.exp(b_last - b))
        st_scr[hh] = st_new

        o = o * lax.rsqrt(jnp.mean(o * o, axis=-1, keepdims=True) + NORM_EPS) * ng_ref[...]
        o = o * jax.nn.sigmoid(load(g_refs[hh]))
        for j in range(nv):
            o_scr[pl.ds(j, SUBLANES, stride=nv), :] = o[SUBLANES * j:SUBLANES * (j + 1)]
        o_ref[:, cols] = o_scr[...].astype(o_ref.dtype)

        @pl.when(n == n_blocks - 1)
        def _():
            so_ref[hh] = st_new.T


def _gla_prompt(p, lb_logits, norm_g, *, layer, n_batch, seq, n_heads, total_rows):
    blk = GLA_BLOCK
    hb = GLA_HEADS_PER_STEP
    assert seq % blk == 0 and n_heads % hb == 0
    nb = seq // blk
    hg = n_heads // hb
    w = hb * LANES
    masks = jnp.asarray(_gla_level_masks())

    def head_spec(part, hh):
        return pl.BlockSpec((blk, LANES), lambda b, h, n: (b * nb + n, part * n_heads + h * hb + hh))

    return pl.pallas_call(
        functools.partial(_gla_prompt_body, heads=hb, layer=layer, n_blocks=nb),
        grid=(n_batch, hg, nb),
        in_specs=[head_spec(part, hh) for part in range(4) for hh in range(hb)] + [
                  pl.BlockSpec((lb_logits.shape[0], w), lambda b, h, n: (0, h)),
                  pl.BlockSpec((1, LANES), lambda b, h, n: (0, 0)),
                  pl.BlockSpec(masks.shape, lambda b, h, n: (0, 0, 0))],
        out_specs=[pl.BlockSpec((blk, w), lambda b, h, n: (b * nb + n, h)),
                   pl.BlockSpec((None, hb, LANES, LANES), lambda b, h, n: (b, h, 0, 0))],
        out_shape=[jax.ShapeDtypeStruct((total_rows, n_heads * LANES), BF16),
                   jax.ShapeDtypeStruct((n_batch, n_heads, LANES, LANES), F32)],
        scratch_shapes=[pltpu.VMEM((hb, LANES, LANES), F32), pltpu.VMEM((blk, LANES), F32)],
        compiler_params=_cparams(3),
        name="hgrn_prompt",
    )(*([p] * (4 * hb)), lb_logits, norm_g.reshape(1, LANES), masks)


def _gla_sample_body(q_ref, f_ref, i_ref, g_ref, lbl_ref, ng_ref, s0_ref, o_ref, so_ref,
                     qd_scr, kd_scr, o_scr, *, layer):
    n_g, n_t, n_h, _ = q_ref.shape
    lb = _lower_bound([lbl_ref[r] for r in range(lbl_ref.shape[0])], layer)
    q = _silu(q_ref[...])
    fg = lb + (1.0 - lb) * jax.nn.sigmoid(f_ref[...])
    kk = 1.0 - fg
    lf = jnp.log(fg)
    v = i_ref[...]
    bt = [lf[:, 0]]
    for t in range(1, n_t):
        bt.append(bt[-1] + lf[:, t])
    b_last = bt[n_t - 1]
    for t in range(n_t):
        acc = None
        for s in range(t + 1):
            w = q[:, t] * kk[:, s]
            if s < t:
                w = w * jnp.exp(bt[t] - bt[s])
            term = jnp.sum(w, axis=-1, keepdims=True) * v[:, s]
            acc = term if acc is None else acc + term
        o_scr[:, t] = acc
        qd_scr[:, t] = q[:, t] * jnp.exp(bt[t])
        kd_scr[:, t] = kk[:, t] * jnp.exp(b_last - bt[t])
    e_last = jnp.exp(b_last)
    for gi in range(n_g):
        e_cols = e_last[gi].T
        for h in range(n_h):
            s = s0_ref[gi, h]
            o_scr[gi, :, h, :] += _dot(qd_scr[gi, :, h, :], s)
            so_ref[gi, h] = s * e_cols[:, h:h + 1] + _dot_tn(kd_scr[gi, :, h, :], i_ref[gi, :, h, :])
    o = o_scr[...]
    o = o * lax.rsqrt(jnp.mean(o * o, axis=-1, keepdims=True) + NORM_EPS) * ng_ref[...]
    o_ref[...] = (o * jax.nn.sigmoid(g_ref[...])).astype(o_ref.dtype)


def _gla_sample(p4, lb_logits, norm_g, s0, *, layer):
    bd, td, h4, _ = p4.shape
    nh = h4 // 4
    g = min(GLA_SAMPLE_GROUP, bd)
    assert bd % g == 0
    blk = (g, td, nh, LANES)
    sblk = (g, nh, LANES, LANES)
    lb3 = lb_logits.reshape(lb_logits.shape[0], nh, LANES)
    part = lambda off: pl.BlockSpec(blk, lambda i: (i, 0, off, 0))
    return pl.pallas_call(
        functools.partial(_gla_sample_body, layer=layer),
        grid=(bd // g,),
        in_specs=[part(0), part(1), part(2), part(3),
                  pl.BlockSpec(lb3.shape, lambda i: (0, 0, 0)),
                  pl.BlockSpec((1, LANES), lambda i: (0, 0)),
                  pl.BlockSpec(sblk, lambda i: (i, 0, 0, 0))],
        out_specs=[pl.BlockSpec(blk, lambda i: (i, 0, 0, 0)), pl.BlockSpec(sblk, lambda i: (i, 0, 0, 0))],
        out_shape=[jax.ShapeDtypeStruct((bd, td, nh, LANES), BF16),
                   jax.ShapeDtypeStruct((bd, nh, LANES, LANES), F32)],
        scratch_shapes=[pltpu.VMEM(blk, F32), pltpu.VMEM(blk, F32), pltpu.VMEM(blk, F32)],
        compiler_params=_cparams(1),
        name="hgrn_sample",
    )(p4, p4, p4, p4, lb3, norm_g.reshape(1, LANES), s0)


def _prev_rows(buf, dec_seq):
    bd, n_prev, d = buf.shape
    outs = []
    for dd in range(1, n_prev + 1):
        rows = [buf[:, n_prev - dd + t] if t < dd else jnp.zeros((bd, d), buf.dtype) for t in range(dec_seq)]
        outs.append(jnp.stack(rows, axis=1).reshape(bd * dec_seq, d))
    return outs


def _first_rows(vals, dec_seq):
    bd, d = vals.shape
    z = jnp.zeros((bd, dec_seq - 1, d), vals.dtype)
    return jnp.concatenate([vals[:, None, :], z], axis=1).reshape(bd * dec_seq, d)


def kernel(x_prompt, x_sample, state_ret, state_rglru_h, state_rglru_conv, state_hgrn, state_ffn_conv,
           ev_w_in, ev_w_out, ev_rg_conv_w, ev_rg_conv_b, ev_rg_wa, ev_rg_ba, ev_rg_wx, ev_rg_bx, ev_rg_lambda,
           od_w_in, od_w_out, od_norm_g, od_lb_logits, ln_g, ln_b, ffn_w_up, ffn_conv_w, ffn_conv_b, ffn_w_down):
    bp, tp, d_model = x_prompt.shape
    bd, td, _ = x_sample.shape
    depth = ln_g.shape[0]
    m_p, m_s = bp * tp, bd * td
    m = m_p + m_s
    alpha = (2.0 * depth) ** 0.25
    h_ret, dk_ret = state_ret.shape[2], state_ret.shape[3]
    d_ret = h_ret * dk_ret
    d_rg = state_rglru_h.shape[-1]
    h_hg = state_hgrn.shape[2]
    d_ff = ffn_conv_b.shape[-1]
    assert state_hgrn.shape[3] == LANES and state_hgrn.shape[4] == LANES and d_ret == d_rg
    assert ffn_conv_w.shape[1] == 3 and td >= 3

    x = jnp.concatenate([x_prompt.reshape(m_p, d_model), x_sample.reshape(m_s, d_model)], axis=0)
    xb = x.astype(BF16)
    pos_p = jnp.arange(tp, dtype=jnp.int32)
    pos_s = PAST_LEN + jnp.arange(td, dtype=jnp.int32)
    zero_ret = jnp.zeros((bp,) + state_ret.shape[2:], F32)

    n_ret_p, n_ret_s, n_h_p, n_h_s, n_cv_p, n_cv_s = [], [], [], [], [], []
    n_hg_p, n_hg_s, n_ff_p, n_ff_s = [], [], [], []
    tm_ff = min(ROW_TILE, m_s)
    tm_ln = min(LN_ROW_TILE, m_s)
    for l in range(depth):
        if l % 2 == 0:
            e = l // 2
            p = _matmul(xb, ev_w_in[e].astype(BF16), tm_ff, 1024)
            chunk = RET_CHUNK if tp % RET_CHUNK == 0 else tp
            o_ret, s_p = _retention(p, zero_ret, pos_p, row0=0, n_batch=bp, n_seq=1, chunk=chunk,
                                    n_chunks=tp // chunk, n_heads=h_ret, dk=dk_ret)
            g_ret = min(RET_SAMPLE_GROUP, bd)
            o_ret_s, s_s = _retention(p, state_ret[e], pos_s, row0=m_p, n_batch=bd // g_ret, n_seq=g_ret,
                                      chunk=td, n_chunks=1, n_heads=h_ret, dk=dk_ret)
            wa = ev_rg_wa[e].astype(BF16)
            wx = ev_rg_wx[e].astype(BF16)
            rg_args = (ev_rg_conv_w[e], ev_rg_conv_b[e], wa, wx, ev_rg_ba[e], ev_rg_bx[e], ev_rg_lambda[e])
            o_rg, hl_p, xt_p = _rg_prompt(p, *rg_args, n_batch=bp, seq=tp, col0=4 * d_ret, d_rg=d_rg,
                                          total_rows=m)
            e_prev = jnp.stack(_prev_rows(state_rglru_conv[e], td))
            o_rg_s, h_s = _rg_sample(p, e_prev, _first_rows(state_rglru_h[e], td), *rg_args,
                                     row0=m_p, m_s=m_s, dec_seq=td, col0=4 * d_ret, d_rg=d_rg)
            n_conv = state_rglru_conv.shape[2]
            xr_s = p[m_p:, 4 * d_ret:4 * d_ret + d_rg].reshape(bd, td, d_rg)
            n_ret_p.append(s_p)
            n_ret_s.append(s_s)
            n_h_p.append(hl_p[:, 0])
            n_h_s.append(h_s.reshape(bd, td, d_rg)[:, td - 1])
            n_cv_p.append(xt_p[:, SUBLANES - n_conv:])
            n_cv_s.append(xr_s[:, td - n_conv:])
            o_ret = jnp.concatenate([o_ret, o_ret_s], axis=0)
            o_rg = lax.dynamic_update_slice(o_rg, o_rg_s, (m_p, 0))
            w_out = ev_w_out[e].astype(BF16)
            x, xb = _proj_ln([o_ret, o_rg], [w_out[:d_ret], w_out[d_ret:]], x, ln_g[l, 0], ln_b[l, 0], alpha, tm_ln)
        else:
            o = l // 2
            p = _matmul(xb, od_w_in[o].astype(BF16), tm_ff, 1024)
            o_hg, g_p = _gla_prompt(p, od_lb_logits, od_norm_g[o], layer=o, n_batch=bp, seq=tp,
                                    n_heads=h_hg, total_rows=m)
            p4 = p[m_p:].reshape(bd, td, 4 * h_hg, LANES)
            o_hg_s, g_s = _gla_sample(p4, od_lb_logits, od_norm_g[o], state_hgrn[o], layer=o)
            o_hg = lax.dynamic_update_slice(o_hg, o_hg_s.reshape(m_s, h_hg * LANES), (m_p, 0))
            n_hg_p.append(g_p)
            n_hg_s.append(g_s)
            x, xb = _proj_ln([o_hg], [od_w_out[o].astype(BF16)], x, ln_g[l, 0], ln_b[l, 0], alpha, tm_ln)
        e1, e2 = _prev_rows(state_ffn_conv[l], td)
        h, tails, u_s = _ffn_up(xb, ffn_w_up[l].astype(BF16), ffn_conv_w[l], ffn_conv_b[l], e1, e2,
                                m_prompt=m_p, seq=tp, dec_seq=td, tm=tm_ff, tn=512)
        tiles_per_seq = tp // tm_ff
        tails = tails.reshape(m // tm_ff, SUBLANES, d_ff)[:bp * tiles_per_seq]
        tails = tails.reshape(bp, tiles_per_seq, SUBLANES, d_ff)
        n_ff_p.append(tails[:, tiles_per_seq - 1, SUBLANES - 2:])
        n_ff_s.append(u_s.reshape(bd, td, d_ff)[:, td - 2:])
        x, xb = _proj_ln([h], [ffn_w_down[l].astype(BF16)], x, ln_g[l, 1], ln_b[l, 1], alpha, tm_ln)

    y_prompt = x[:m_p].reshape(bp, tp, d_model)
    y_sample = x[m_p:].reshape(bd, td, d_model)
    return (y_prompt, y_sample, jnp.stack(n_ret_p), jnp.stack(n_ret_s), jnp.stack(n_h_p), jnp.stack(n_h_s),
            jnp.stack(n_cv_p), jnp.stack(n_cv_s), jnp.stack(n_hg_p), jnp.stack(n_hg_s),
            jnp.stack(n_ff_p), jnp.stack(n_ff_s))
```

```python
import functools

import numpy as np
import jax
import jax.numpy as jnp
from jax import lax
from jax.experimental import pallas as pl
from jax.experimental.pallas import tpu as pltpu

F32 = jnp.float32
BF16 = jnp.bfloat16

LN_EPS = 1e-5
NORM_EPS = 1e-6
ROPE_BASE = 10000.0
RG_C = 8.0
PAST_LEN = 16384
LANES = 128
SUBLANES = 8
MXU_COLS = 256
ROW_TILE = 512
LN_ROW_TILE = 256
RET_CHUNK = 128
RET_SAMPLE_GROUP = 8
RG_TIME_TILE = 256
GLA_BLOCK = 128
GLA_HEADS_PER_STEP = 2
GLA_SAMPLE_GROUP = 4
MIB = 1024 * 1024


def _cparams(n_axes, vmem_mib=48):
    return pltpu.CompilerParams(dimension_semantics=("arbitrary",) * n_axes,
                                vmem_limit_bytes=vmem_mib * MIB)


def _dot(a, b):
    return jnp.dot(a.astype(BF16), b.astype(BF16), preferred_element_type=F32)


def _dot_nt(a, b):
    return lax.dot_general(a.astype(BF16), b.astype(BF16), (((1,), (1,)), ((), ())),
                           preferred_element_type=F32)


def _dot_tn(a, b):
    return lax.dot_general(a.astype(BF16), b.astype(BF16), (((0,), (0,)), ((), ())),
                           preferred_element_type=F32)


def _silu(x):
    return x * jax.nn.sigmoid(x)


def _mm_body(x_ref, w_ref, o_ref, wb_scr):
    @pl.when(pl.program_id(1) == 0)
    def _():
        wb_scr[...] = w_ref[...].astype(BF16)

    o_ref[...] = jnp.dot(x_ref[...], wb_scr[...], preferred_element_type=F32).astype(o_ref.dtype)


def _matmul(xb, w, layer, tm, tn):
    m, k = xb.shape
    n = w.shape[2]
    assert m % tm == 0 and n % tn == 0
    return pl.pallas_call(
        _mm_body,
        grid=(n // tn, m // tm),
        in_specs=[pl.BlockSpec((tm, k), lambda j, i: (i, 0)),
                  pl.BlockSpec((None, k, tn), lambda j, i: (layer, 0, j))],
        out_specs=pl.BlockSpec((tm, tn), lambda j, i: (i, j)),
        out_shape=jax.ShapeDtypeStruct((m, n), F32),
        scratch_shapes=[pltpu.VMEM((k, tn), BF16)],
        compiler_params=_cparams(2),
        name="in_proj",
    )(xb, w)


def _proj_ln_body(*refs, n_parts, alpha):
    a_refs = refs[:n_parts]
    w_refs = refs[n_parts:2 * n_parts]
    x_ref, g_ref, b_ref, o_ref, ob_ref = refs[2 * n_parts:]
    acc = None
    for a_ref, w_ref in zip(a_refs, w_refs):
        d = jnp.dot(a_ref[...], w_ref[...], preferred_element_type=F32)
        acc = d if acc is None else acc + d
    y = alpha * x_ref[...] + acc
    mu = jnp.mean(y, axis=-1, keepdims=True)
    d = y - mu
    var = jnp.mean(d * d, axis=-1, keepdims=True)
    out = d * lax.rsqrt(var + LN_EPS) * g_ref[...] + b_ref[...]
    o_ref[...] = out
    ob_ref[...] = out.astype(BF16)


def _proj_ln(parts, wb, layer, x, g, b, alpha, tm):
    m, d = x.shape
    assert m % tm == 0
    n_parts = len(parts)
    kp = parts[0].shape[1]
    assert all(p.shape[1] == kp for p in parts) and wb.shape[1] == kp * n_parts

    def w_spec(part):
        return pl.BlockSpec((None, kp, d), lambda i: (layer, part, 0), pipeline_mode=pl.Buffered(1))

    in_specs = [pl.BlockSpec((tm, kp), lambda i: (i, 0)) for _ in parts]
    in_specs += [w_spec(part) for part in range(n_parts)]
    in_specs += [pl.BlockSpec((tm, d), lambda i: (i, 0)),
                 pl.BlockSpec((1, d), lambda i: (0, 0)),
                 pl.BlockSpec((1, d), lambda i: (0, 0))]
    return pl.pallas_call(
        functools.partial(_proj_ln_body, n_parts=n_parts, alpha=alpha),
        grid=(m // tm,),
        in_specs=in_specs,
        out_specs=[pl.BlockSpec((tm, d), lambda i: (i, 0)), pl.BlockSpec((tm, d), lambda i: (i, 0))],
        out_shape=[jax.ShapeDtypeStruct((m, d), F32), jax.ShapeDtypeStruct((m, d), BF16)],
        compiler_params=_cparams(1, vmem_mib=56),
        name="proj_ln",
    )(*parts, *([wb] * n_parts), x, g.reshape(1, d), b.reshape(1, d))


def _ffn_up_body(x_ref, wu_ref, wv_ref, cw_ref, cb_ref, e1_ref, e2_ref, h_ref, tail_ref, us_ref,
                 wub_scr, wvb_scr, carry_ref, *, n_prompt_tiles, tiles_per_seq, dec_seq):
    i = pl.program_id(1)

    @pl.when(i == 0)
    def _():
        wub_scr[...] = wu_ref[...].astype(BF16)
        wvb_scr[...] = wv_ref[...].astype(BF16)

    tm = x_ref.shape[0]
    tn = h_ref.shape[1]
    cw = min(MXU_COLS, tn)
    row = lax.broadcasted_iota(jnp.int32, (tm, cw), 0)
    row8 = lax.broadcasted_iota(jnp.int32, (SUBLANES, cw), 0)

    def chunk(c, fix):
        cols = slice(c * cw, (c + 1) * cw)
        x = x_ref[...]
        u = jnp.dot(x, wub_scr[:, cols], preferred_element_type=F32)
        v = jnp.dot(x, wvb_scr[:, cols], preferred_element_type=F32)
        u1, u2 = fix(cols, pltpu.roll(u, 1, axis=0), pltpu.roll(u, 2, axis=0))
        uc = u2 * cw_ref[0:1, cols] + cb_ref[:, cols]
        uc = uc + u1 * cw_ref[1:2, cols]
        uc = uc + u * cw_ref[2:3, cols]
        h_ref[:, cols] = (jax.nn.gelu(uc) * v).astype(h_ref.dtype)
        carry_ref[:, cols] = u[tm - SUBLANES:, :]
        tail_ref[:, cols] = u[tm - SUBLANES:, :]
        return u

    @pl.when(i < n_prompt_tiles)
    def _():
        first = i % tiles_per_seq == 0

        def fix(cols, r1, r2):
            c = jnp.where(first, 0.0, carry_ref[:, cols])
            c_last = c[SUBLANES - 1:SUBLANES]
            c_prev = c[SUBLANES - 2:SUBLANES - 1]
            h1 = jnp.where(row8 == 0, c_last, r1[:SUBLANES])
            h2 = jnp.where(row8 == 0, c_prev, jnp.where(row8 == 1, c_last, r2[:SUBLANES]))
            return (jnp.concatenate([h1, r1[SUBLANES:]], axis=0), jnp.concatenate([h2, r2[SUBLANES:]], axis=0))

        for c in range(tn // cw):
            chunk(c, fix)

    @pl.when(i >= n_prompt_tiles)
    def _():
        t = row % dec_seq

        def fix(cols, r1, r2):
            return jnp.where(t >= 1, r1, e1_ref[:, cols]), jnp.where(t >= 2, r2, e2_ref[:, cols])

        for c in range(tn // cw):
            cols = slice(c * cw, (c + 1) * cw)
            us_ref[:, cols] = chunk(c, fix)


def _ffn_up(xb, w_up, layer, cw, cb, e1, e2, *, m_prompt, seq, dec_seq, tm, tn):
    m, k = xb.shape
    d_ff = w_up.shape[2] // 2
    m_s = m - m_prompt
    assert m_prompt % tm == 0 and m_s % tm == 0 and seq % tm == 0 and d_ff % tn == 0
    assert tm % dec_seq == 0 and dec_seq >= 2 and cw.shape[0] == 3 and tn % min(MXU_COLS, tn) == 0
    npt = m_prompt // tm
    nj = d_ff // tn
    srow = lambda j, i: (jnp.maximum(i - npt, 0), j)
    return pl.pallas_call(
        functools.partial(_ffn_up_body, n_prompt_tiles=npt, tiles_per_seq=seq // tm, dec_seq=dec_seq),
        grid=(nj, m // tm),
        in_specs=[pl.BlockSpec((tm, k), lambda j, i: (i, 0)),
                  pl.BlockSpec((None, k, tn), lambda j, i: (layer, 0, j)),
                  pl.BlockSpec((None, k, tn), lambda j, i: (layer, 0, j + nj)),
                  pl.BlockSpec((3, tn), lambda j, i: (0, j)),
                  pl.BlockSpec((1, tn), lambda j, i: (0, j)),
                  pl.BlockSpec((tm, tn), srow),
                  pl.BlockSpec((tm, tn), srow)],
        out_specs=[pl.BlockSpec((tm, tn), lambda j, i: (i, j)),
                   pl.BlockSpec((SUBLANES, tn), lambda j, i: (i, j)),
                   pl.BlockSpec((tm, tn), srow)],
        out_shape=[jax.ShapeDtypeStruct((m, d_ff), BF16),
                   jax.ShapeDtypeStruct((m // tm * SUBLANES, d_ff), F32),
                   jax.ShapeDtypeStruct((m_s, d_ff), F32)],
        scratch_shapes=[pltpu.VMEM((k, tn), BF16), pltpu.VMEM((k, tn), BF16), pltpu.VMEM((SUBLANES, tn), F32)],
        compiler_params=_cparams(2),
        name="ffn_up",
    )(xb, w_up, w_up, cw, cb.reshape(1, d_ff), e1, e2)


def _ret_body(q_ref, k_ref, v_ref, g_ref, cos_ref, sin_ref, dm_ref, xi_ref, zeta_ref, cdec_ref, s0_ref,
              *rest, n_seq, chunk, n_chunks, scale):
    o_ref, so_ref, s_scr, q_scr, kz_scr, o_scr = rest[-6:]
    n = pl.program_id(2)

    @pl.when(n == 0)
    def _():
        s_scr[...] = s0_ref[...]

    half = q_ref.shape[1] // 2
    cos = cos_ref[...]
    sin = sin_ref[...]

    def rot(x):
        x1 = x[:, :half]
        x2 = x[:, half:]
        return jnp.concatenate([x1 * cos - x2 * sin, x2 * cos + x1 * sin], axis=-1)

    q = rot(q_ref[...])
    k = rot(k_ref[...]) * scale
    vb = v_ref[...].astype(BF16)
    inner = _dot_nt(q, k) * dm_ref[...]
    o_scr[...] = _dot(inner, vb)
    q_scr[...] = q
    kz_scr[...] = k * zeta_ref[...]
    cdec = cdec_ref[...]
    for gi in range(n_seq):
        rows = pl.ds(gi * chunk, chunk)
        s = s_scr[gi]
        o_scr[rows, :] += _dot(q_scr[rows, :], s) * xi_ref[rows, :]
        s_scr[gi] = s * cdec + _dot_tn(kz_scr[rows, :], v_ref[rows, :])
    o = o_scr[...]
    gate = g_ref[...]
    o = o * lax.rsqrt(jnp.mean(o * o, axis=-1, keepdims=True) + NORM_EPS) * _silu(gate)
    o_ref[...] = o.astype(o_ref.dtype)

    @pl.when(n == n_chunks - 1)
    def _():
        so_ref[...] = s_scr[...]


def _ret_tables(n_heads, chunk, n_seq):
    lg = jnp.log(1.0 - 2.0 ** (-5.0 - jnp.arange(n_heads, dtype=F32)))
    idx = jnp.arange(chunk, dtype=F32)
    diff = idx[:, None] - idx[None, :]
    dmask = jnp.where(diff[None] >= 0, jnp.exp(jnp.maximum(diff, 0.0)[None] * lg[:, None, None]), 0.0)
    eye = jnp.eye(n_seq, dtype=F32)
    dm = jnp.einsum('ab,hts->hatbs', eye, dmask).reshape(n_heads, n_seq * chunk, n_seq * chunk)
    xi = jnp.tile(jnp.exp((idx[None, :] + 1.0) * lg[:, None]), (1, n_seq))[:, :, None]
    zeta = jnp.tile(jnp.exp((chunk - 1.0 - idx[None, :]) * lg[:, None]), (1, n_seq))[:, :, None]
    cdec = jnp.exp(chunk * lg)[:, None, None]
    return dm, xi, zeta, cdec


def _rope_tables(pos, half):
    inv = ROPE_BASE ** (-jnp.arange(half, dtype=F32) / half)
    ang = pos.astype(F32)[:, None] * inv[None, :]
    return jnp.cos(ang), jnp.sin(ang)


def _stacked_out(n_layers, layer, shape, block, index_map, prev, n_inputs):
    sds = jax.ShapeDtypeStruct((n_layers,) + tuple(shape), F32)
    spec = pl.BlockSpec((None,) + tuple(block), lambda *g: (layer,) + tuple(index_map(*g)))
    if prev is None:
        return sds, spec, [], [], {}
    return sds, spec, [prev], [pl.BlockSpec(memory_space=pl.ANY)], {n_inputs: 1}


def _retention(p, s0, pos, *, layer, n_layers, prev_state, row0, n_batch, n_seq, chunk, n_chunks, n_heads, dk):
    rows = n_seq * chunk
    assert row0 % rows == 0
    rb0 = row0 // rows
    dm, xi, zeta, cdec = _ret_tables(n_heads, chunk, n_seq)
    cos, sin = _rope_tables(pos, dk // 2)
    if n_seq > 1:
        assert n_chunks == 1
        cos = jnp.tile(cos, (n_seq, 1))
        sin = jnp.tile(sin, (n_seq, 1))
    hq = n_heads

    def col(off):
        return lambda b, h, n: (rb0 + b * n_chunks + n, off * hq + h)

    tab = lambda b, h, n: (h, 0, 0)
    s0_layer = layer if s0.shape[0] > 1 else 0
    in_specs = [pl.BlockSpec((rows, dk), col(0)), pl.BlockSpec((rows, dk), col(1)),
                pl.BlockSpec((rows, dk), col(2)), pl.BlockSpec((rows, dk), col(3)),
                pl.BlockSpec((rows, dk // 2), lambda b, h, n: (n, 0)),
                pl.BlockSpec((rows, dk // 2), lambda b, h, n: (n, 0)),
                pl.BlockSpec((None, rows, rows), tab),
                pl.BlockSpec((None, rows, 1), tab),
                pl.BlockSpec((None, rows, 1), tab),
                pl.BlockSpec((None, 1, 1), tab),
                pl.BlockSpec((None, n_seq, None, dk, dk), lambda b, h, n: (s0_layer, b, h, 0, 0))]
    args = [p, p, p, p, cos, sin, dm, xi, zeta, cdec, s0]
    so_shape, so_spec, extra, extra_specs, aliases = _stacked_out(
        n_layers, layer, (n_batch * n_seq, n_heads, dk, dk), (n_seq, None, dk, dk),
        lambda b, h, n: (b, h, 0, 0), prev_state, len(args))
    total_rows = n_batch * n_chunks * rows
    return pl.pallas_call(
        functools.partial(_ret_body, n_seq=n_seq, chunk=chunk, n_chunks=n_chunks, scale=dk ** -0.5),
        grid=(n_batch, n_heads, n_chunks),
        in_specs=in_specs + extra_specs,
        out_specs=[pl.BlockSpec((rows, dk), lambda b, h, n: (b * n_chunks + n, h)), so_spec],
        out_shape=[jax.ShapeDtypeStruct((total_rows, n_heads * dk), BF16), so_shape],
        input_output_aliases=aliases,
        scratch_shapes=[pltpu.VMEM((n_seq, dk, dk), F32), pltpu.VMEM((rows, dk), F32),
                        pltpu.VMEM((rows, dk), F32), pltpu.VMEM((rows, dk), F32)],
        compiler_params=_cparams(3),
        name="retention",
    )(*args, *extra)


def _rg_gates(xc, wa_ref, wx_ref, ba, bx, lam):
    nb, bw, _ = wa_ref.shape
    ra, ia = [], []
    for blk in range(nb):
        xb = xc[:, blk * bw:(blk + 1) * bw].astype(BF16)
        ra.append(jnp.dot(xb, wa_ref[blk], preferred_element_type=F32))
        ia.append(jnp.dot(xb, wx_ref[blk], preferred_element_type=F32))
    r = jax.nn.sigmoid(jnp.concatenate(ra, axis=-1) + ba)
    i = jax.nn.sigmoid(jnp.concatenate(ia, axis=-1) + bx)
    log_a = -RG_C * r * jax.nn.softplus(-lam)
    a = jnp.exp(log_a)
    mult = jnp.sqrt(-jnp.tanh(log_a) * (a * a + 1.0))
    return a, mult, i


def _scan_rows(a, b, row_in_seg, seg_len):
    s = 1
    while s < seg_len:
        a_sh = pltpu.roll(a, s, axis=0)
        b_sh = pltpu.roll(b, s, axis=0)
        m = row_in_seg >= s
        b = jnp.where(m, a * b_sh + b, b)
        a = jnp.where(m, a * a_sh, a)
        s *= 2
    return a, b


def _rg_prompt_body(xr_ref, gr_ref, cw_ref, cb_ref, wa_ref, wx_ref, ba_ref, bx_ref, lam_ref,
                    o_ref, hl_ref, xt_ref, h_scr, x_scr, *, n_tiles):
    n = pl.program_id(1)

    @pl.when(n == 0)
    def _():
        h_scr[...] = jnp.zeros_like(h_scr)
        x_scr[...] = jnp.zeros_like(x_scr)

    xr = xr_ref[...]
    tc = xr.shape[0]
    row = lax.broadcasted_iota(jnp.int32, xr.shape, 0)
    prev = x_scr[...]
    n_tap = cw_ref.shape[0]
    xc = None
    for j in range(n_tap):
        d = n_tap - 1 - j
        if d == 0:
            xd = xr
        else:
            prev_d = jnp.tile(pltpu.roll(prev, d, axis=0), (tc // SUBLANES, 1))
            xd = jnp.where(row < d, prev_d, pltpu.roll(xr, d, axis=0))
        term = xd * cw_ref[j:j + 1, :]
        xc = term + cb_ref[...] if xc is None else xc + term
    a, mult, gate_i = _rg_gates(xc, wa_ref, wx_ref, ba_ref[...], bx_ref[...], lam_ref[...])
    mult = jnp.where(row + n * tc == 0, 1.0, mult)
    bterm = xc * gate_i * mult
    a_cum, b_cum = _scan_rows(a, bterm, row, tc)
    h = a_cum * h_scr[...] + b_cum
    o_ref[...] = (h * jax.nn.gelu(gr_ref[...])).astype(o_ref.dtype)
    h_scr[...] = h[tc - 1:tc, :]
    x_scr[...] = xr[tc - SUBLANES:, :]

    @pl.when(n == n_tiles - 1)
    def _():
        hl_ref[...] = h[tc - 1:tc, :]
        xt_ref[...] = xr[tc - SUBLANES:, :]


def _rg_prompt(p, cw, cb, wa, wx, ba, bx, lam, *, n_batch, seq, col0, d_rg, total_rows):
    tc = min(RG_TIME_TILE, seq)
    assert seq % tc == 0 and col0 % d_rg == 0 and (tc & (tc - 1)) == 0
    nt = seq // tc
    cb0 = col0 // d_rg
    vec = lambda a: a.reshape(1, d_rg)
    full = lambda a: pl.BlockSpec(a.shape, lambda b, n: (0,) * a.ndim)
    args = (cw, vec(cb), wa, wx, vec(ba), vec(bx), vec(lam))
    return pl.pallas_call(
        functools.partial(_rg_prompt_body, n_tiles=nt),
        grid=(n_batch, nt),
        in_specs=[pl.BlockSpec((tc, d_rg), lambda b, n: (b * nt + n, cb0)),
                  pl.BlockSpec((tc, d_rg), lambda b, n: (b * nt + n, cb0 + 1))] + [full(a) for a in args],
        out_specs=[pl.BlockSpec((tc, d_rg), lambda b, n: (b * nt + n, 0)),
                   pl.BlockSpec((None, 1, d_rg), lambda b, n: (b, 0, 0)),
                   pl.BlockSpec((None, SUBLANES, d_rg), lambda b, n: (b, 0, 0))],
        out_shape=[jax.ShapeDtypeStruct((total_rows, d_rg), BF16),
                   jax.ShapeDtypeStruct((n_batch, 1, d_rg), F32),
                   jax.ShapeDtypeStruct((n_batch, SUBLANES, d_rg), F32)],
        scratch_shapes=[pltpu.VMEM((1, d_rg), F32), pltpu.VMEM((SUBLANES, d_rg), F32)],
        compiler_params=_cparams(2),
        name="rglru_prompt",
    )(p, p, *args)


def _rg_sample_body(xr_ref, gr_ref, e_ref, h0_ref, cw_ref, cb_ref, wa_ref, wx_ref, ba_ref, bx_ref, lam_ref,
                    o_ref, h_ref, *, dec_seq, first_pos):
    xr = xr_ref[...]
    row = lax.broadcasted_iota(jnp.int32, xr.shape, 0)
    t = row % dec_seq
    n_tap = cw_ref.shape[0]
    xc = None
    for j in range(n_tap):
        d = n_tap - 1 - j
        xd = xr if d == 0 else jnp.where(t >= d, pltpu.roll(xr, d, axis=0), e_ref[d - 1])
        term = xd * cw_ref[j:j + 1, :]
        xc = term + cb_ref[...] if xc is None else xc + term
    a, mult, gate_i = _rg_gates(xc, wa_ref, wx_ref, ba_ref[...], bx_ref[...], lam_ref[...])
    if first_pos == 0:
        mult = jnp.where(t == 0, 1.0, mult)
    bterm = xc * gate_i * mult + jnp.where(t == 0, a * h0_ref[...], 0.0)
    _, h = _scan_rows(a, bterm, t, dec_seq)
    h_ref[...] = h
    o_ref[...] = (h * jax.nn.gelu(gr_ref[...])).astype(o_ref.dtype)


def _rg_sample(p, e_prev, h0_rows, cw, cb, wa, wx, ba, bx, lam, *, row0, m_s, dec_seq, col0, d_rg):
    tr = min(128, m_s)
    assert m_s % tr == 0 and row0 % tr == 0 and tr % dec_seq == 0 and (dec_seq & (dec_seq - 1)) == 0
    rb0 = row0 // tr
    cb0 = col0 // d_rg
    vec = lambda a: a.reshape(1, d_rg)
    full = lambda a: pl.BlockSpec(a.shape, lambda i: (0,) * a.ndim)
    args = (cw, vec(cb), wa, wx, vec(ba), vec(bx), vec(lam))
    n_prev = e_prev.shape[0]
    return pl.pallas_call(
        functools.partial(_rg_sample_body, dec_seq=dec_seq, first_pos=PAST_LEN),
        grid=(m_s // tr,),
        in_specs=[pl.BlockSpec((tr, d_rg), lambda i: (rb0 + i, cb0)),
                  pl.BlockSpec((tr, d_rg), lambda i: (rb0 + i, cb0 + 1)),
                  pl.BlockSpec((n_prev, tr, d_rg), lambda i: (0, i, 0)),
                  pl.BlockSpec((tr, d_rg), lambda i: (i, 0))] + [full(a) for a in args],
        out_specs=[pl.BlockSpec((tr, d_rg), lambda i: (i, 0)), pl.BlockSpec((tr, d_rg), lambda i: (i, 0))],
        out_shape=[jax.ShapeDtypeStruct((m_s, d_rg), BF16), jax.ShapeDtypeStruct((m_s, d_rg), F32)],
        compiler_params=_cparams(1),
        name="rglru_sample",
    )(p, p, e_prev, h0_rows, *args)


def _lower_bound(logit_rows, layer):
    m = logit_rows[0]
    for z in logit_rows[1:]:
        m = jnp.maximum(m, z)
    e = [jnp.exp(z - m) for z in logit_rows]
    tot = e[0]
    for x in e[1:]:
        tot = tot + x
    if layer == 0:
        return jnp.zeros_like(tot)
    num = e[1]
    for x in e[2:layer + 1]:
        num = num + x
    return num / tot


_GLA_VREGS = GLA_BLOCK // SUBLANES


def _gla_level_masks():
    r = np.arange(GLA_BLOCK)
    t = r // SUBLANES + _GLA_VREGS * (r % SUBLANES)
    tq, tk = t[:, None], t[None, :]
    n_levels = GLA_BLOCK.bit_length()
    masks = np.zeros((n_levels, GLA_BLOCK, GLA_BLOCK), np.float32)
    masks[0] = tq == tk
    for lvl in range(1, n_levels):
        gs = 1 << lvl
        masks[lvl] = (tq > tk) & (tq // gs == tk // gs) & (tq // (gs // 2) != tk // (gs // 2))
    return masks


def _gla_prompt_body(*refs, heads, layer, n_blocks):
    q_refs, f_refs, i_refs, g_refs = (refs[p * heads:(p + 1) * heads] for p in range(4))
    lbl_ref, ng_ref, msk_ref, o_ref, so_ref, st_scr, o_scr = refs[4 * heads:]
    n = pl.program_id(2)

    @pl.when(n == 0)
    def _():
        st_scr[...] = jnp.zeros_like(st_scr)

    nv = _GLA_VREGS
    sub = lax.broadcasted_iota(jnp.int32, (SUBLANES, LANES), 0)
    zero = jnp.zeros((SUBLANES, LANES), F32)
    for hh in range(heads):
        cols = slice(hh * LANES, (hh + 1) * LANES)

        def load(ref):
            return jnp.concatenate([ref[pl.ds(j, SUBLANES, stride=nv), :] for j in range(nv)], axis=0)

        def groups(x):
            return [x[SUBLANES * j:SUBLANES * (j + 1)] for j in range(nv)]

        lb = _lower_bound([lbl_ref[r:r + 1, cols] for r in range(lbl_ref.shape[0])], layer)
        q = _silu(load(q_refs[hh]))
        fg = lb + (1.0 - lb) * jax.nn.sigmoid(load(f_refs[hh]))
        kk = 1.0 - fg
        v = load(i_refs[hh])
        vb = v.astype(BF16)
        lf = groups(jnp.log(fg))

        c = [lf[0]]
        for j in range(1, nv):
            c.append(c[-1] + lf[j])
        tot = c[nv - 1]
        x = tot
        s = 1
        while s < SUBLANES:
            x = x + jnp.where(sub >= s, pltpu.roll(x, s, axis=0), 0.0)
            s *= 2
        before = x - tot
        bj = [cj + before for cj in c]
        b = jnp.concatenate(bj, axis=0)
        b_last = bj[nv - 1][SUBLANES - 1:SUBLANES, :]

        qj = groups(q)
        kj = groups(kk)
        a_mat = msk_ref[0] * _dot_nt(q, kk)
        lvl = 1
        gs = 2
        while gs <= nv:
            hs = gs // 2
            qd, kd = [], []
            for j in range(nv):
                ref = (j // gs) * gs + hs - 1
                if j % gs >= hs:
                    qd.append(qj[j] * jnp.exp(bj[j] - bj[ref]))
                    kd.append(zero)
                else:
                    kd.append(kj[j] if j == ref else kj[j] * jnp.exp(bj[ref] - bj[j]))
                    qd.append(zero)
            a_mat = a_mat + msk_ref[lvl] * _dot_nt(jnp.concatenate(qd, axis=0), jnp.concatenate(kd, axis=0))
            lvl += 1
            gs *= 2
        m = 2
        while m <= SUBLANES:
            src = sub - sub % m + (m // 2 - 1)
            ref = zero
            for s_src in range(m // 2 - 1, SUBLANES, m):
                row_b = jnp.broadcast_to(bj[nv - 1][s_src:s_src + 1, :], (SUBLANES, LANES))
                ref = jnp.where(src == s_src, row_b, ref)
            upper = (sub % m) >= (m // 2)
            qd, kd = [], []
            for j in range(nv):
                e = jnp.exp(-jnp.abs(bj[j] - ref))
                qd.append(jnp.where(upper, qj[j] * e, 0.0))
                kd.append(jnp.where(upper, 0.0, kj[j] * e))
            a_mat = a_mat + msk_ref[lvl] * _dot_nt(jnp.concatenate(qd, axis=0), jnp.concatenate(kd, axis=0))
            lvl += 1
            m *= 2

        st = st_scr[hh]
        o = _dot(a_mat, vb) + _dot_nt(q * jnp.exp(b), st)
        st_new = st * jnp.exp(b_last) + _dot_tn(vb, kk * jnp.exp(b_last - b))
        st_scr[hh] = st_new

        o = o * lax.rsqrt(jnp.mean(o * o, axis=-1, keepdims=True) + NORM_EPS) * ng_ref[...]
        o = o * jax.nn.sigmoid(load(g_refs[hh]))
        for j in range(nv):
            o_scr[pl.ds(j, SUBLANES, stride=nv), :] = o[SUBLANES * j:SUBLANES * (j + 1)]
        o_ref[:, cols] = o_scr[...].astype(o_ref.dtype)

        @pl.when(n == n_blocks - 1)
        def _():
            so_ref[hh] = st_new.T


def _gla_prompt(p, lb_logits, norm_g, *, layer, n_batch, seq, n_heads, total_rows):
    blk = GLA_BLOCK
    hb = GLA_HEADS_PER_STEP
    assert seq % blk == 0 and n_heads % hb == 0
    nb = seq // blk
    hg = n_heads // hb
    w = hb * LANES
    masks = jnp.asarray(_gla_level_masks())

    def head_spec(part, hh):
        return pl.BlockSpec((blk, LANES), lambda b, h, n: (b * nb + n, part * n_heads + h * hb + hh))

    return pl.pallas_call(
        functools.partial(_gla_prompt_body, heads=hb, layer=layer, n_blocks=nb),
        grid=(n_batch, hg, nb),
        in_specs=[head_spec(part, hh) for part in range(4) for hh in range(hb)] + [
                  pl.BlockSpec((lb_logits.shape[0], w), lambda b, h, n: (0, h)),
                  pl.BlockSpec((1, LANES), lambda b, h, n: (0, 0)),
                  pl.BlockSpec(masks.shape, lambda b, h, n: (0, 0, 0))],
        out_specs=[pl.BlockSpec((blk, w), lambda b, h, n: (b * nb + n, h)),
                   pl.BlockSpec((None, hb, LANES, LANES), lambda b, h, n: (b, h, 0, 0))],
        out_shape=[jax.ShapeDtypeStruct((total_rows, n_heads * LANES), BF16),
                   jax.ShapeDtypeStruct((n_batch, n_heads, LANES, LANES), F32)],
        scratch_shapes=[pltpu.VMEM((hb, LANES, LANES), F32), pltpu.VMEM((blk, LANES), F32)],
        compiler_params=_cparams(3),
        name="hgrn_prompt",
    )(*([p] * (4 * hb)), lb_logits, norm_g.reshape(1, LANES), masks)


def _gla_sample_body(q_ref, f_ref, i_ref, g_ref, lbl_ref, ng_ref, s0_ref, *rest, layer):
    o_ref, so_ref, qd_scr, kd_scr, o_scr = rest[-5:]
    n_g, n_t, n_h, _ = q_ref.shape
    lb = _lower_bound([lbl_ref[r] for r in range(lbl_ref.shape[0])], layer)
    q = _silu(q_ref[...])
    fg = lb + (1.0 - lb) * jax.nn.sigmoid(f_ref[...])
    kk = 1.0 - fg
    lf = jnp.log(fg)
    v = i_ref[...]
    bt = [lf[:, 0]]
    for t in range(1, n_t):
        bt.append(bt[-1] + lf[:, t])
    b_last = bt[n_t - 1]
    for t in range(n_t):
        acc = None
        for s in range(t + 1):
            w = q[:, t] * kk[:, s]
            if s < t:
                w = w * jnp.exp(bt[t] - bt[s])
            term = jnp.sum(w, axis=-1, keepdims=True) * v[:, s]
            acc = term if acc is None else acc + term
        o_scr[:, t] = acc
        qd_scr[:, t] = q[:, t] * jnp.exp(bt[t])
        kd_scr[:, t] = kk[:, t] * jnp.exp(b_last - bt[t])
    e_last = jnp.exp(b_last)
    for gi in range(n_g):
        e_cols = e_last[gi].T
        for h in range(n_h):
            s = s0_ref[gi, h]
            o_scr[gi, :, h, :] += _dot(qd_scr[gi, :, h, :], s)
            so_ref[gi, h] = s * e_cols[:, h:h + 1] + _dot_tn(kd_scr[gi, :, h, :], i_ref[gi, :, h, :])
    o = o_scr[...]
    o = o * lax.rsqrt(jnp.mean(o * o, axis=-1, keepdims=True) + NORM_EPS) * ng_ref[...]
    o_ref[...] = (o * jax.nn.sigmoid(g_ref[...])).astype(o_ref.dtype)


def _gla_sample(p4, lb_logits, norm_g, s0, *, layer, prev_state):
    bd, td, h4, _ = p4.shape
    nh = h4 // 4
    g = min(GLA_SAMPLE_GROUP, bd)
    assert bd % g == 0
    blk = (g, td, nh, LANES)
    sblk = (g, nh, LANES, LANES)
    lb3 = lb_logits.reshape(lb_logits.shape[0], nh, LANES)
    part = lambda off: pl.BlockSpec(blk, lambda i: (i, 0, off, 0))
    in_specs = [part(0), part(1), part(2), part(3),
                pl.BlockSpec(lb3.shape, lambda i: (0, 0, 0)),
                pl.BlockSpec((1, LANES), lambda i: (0, 0)),
                pl.BlockSpec((None,) + sblk, lambda i: (layer, i, 0, 0, 0))]
    args = [p4, p4, p4, p4, lb3, norm_g.reshape(1, LANES), s0]
    so_shape, so_spec, extra, extra_specs, aliases = _stacked_out(
        s0.shape[0], layer, (bd, nh, LANES, LANES), sblk, lambda i: (i, 0, 0, 0), prev_state, len(args))
    return pl.pallas_call(
        functools.partial(_gla_sample_body, layer=layer),
        grid=(bd // g,),
        in_specs=in_specs + extra_specs,
        out_specs=[pl.BlockSpec(blk, lambda i: (i, 0, 0, 0)), so_spec],
        out_shape=[jax.ShapeDtypeStruct((bd, td, nh, LANES), BF16), so_shape],
        input_output_aliases=aliases,
        scratch_shapes=[pltpu.VMEM(blk, F32), pltpu.VMEM(blk, F32), pltpu.VMEM(blk, F32)],
        compiler_params=_cparams(1),
        name="hgrn_sample",
    )(*args, *extra)


def _prev_rows(buf, dec_seq):
    bd, n_prev, d = buf.shape
    outs = []
    for dd in range(1, n_prev + 1):
        rows = [buf[:, n_prev - dd + t] if t < dd else jnp.zeros((bd, d), buf.dtype) for t in range(dec_seq)]
        outs.append(jnp.stack(rows, axis=1).reshape(bd * dec_seq, d))
    return outs


def _first_rows(vals, dec_seq):
    bd, d = vals.shape
    z = jnp.zeros((bd, dec_seq - 1, d), vals.dtype)
    return jnp.concatenate([vals[:, None, :], z], axis=1).reshape(bd * dec_seq, d)


def kernel(x_prompt, x_sample, state_ret, state_rglru_h, state_rglru_conv, state_hgrn, state_ffn_conv,
           ev_w_in, ev_w_out, ev_rg_conv_w, ev_rg_conv_b, ev_rg_wa, ev_rg_ba, ev_rg_wx, ev_rg_bx, ev_rg_lambda,
           od_w_in, od_w_out, od_norm_g, od_lb_logits, ln_g, ln_b, ffn_w_up, ffn_conv_w, ffn_conv_b, ffn_w_down):
    bp, tp, d_model = x_prompt.shape
    bd, td, _ = x_sample.shape
    depth = ln_g.shape[0]
    m_p, m_s = bp * tp, bd * td
    m = m_p + m_s
    alpha = (2.0 * depth) ** 0.25
    h_ret, dk_ret = state_ret.shape[2], state_ret.shape[3]
    d_ret = h_ret * dk_ret
    d_rg = state_rglru_h.shape[-1]
    h_hg = state_hgrn.shape[2]
    d_ff = ffn_conv_b.shape[-1]
    assert state_hgrn.shape[3] == LANES and state_hgrn.shape[4] == LANES and d_ret == d_rg
    assert ffn_conv_w.shape[1] == 3 and td >= 3

    x = jnp.concatenate([x_prompt.reshape(m_p, d_model), x_sample.reshape(m_s, d_model)], axis=0)
    xb = x.astype(BF16)
    pos_p = jnp.arange(tp, dtype=jnp.int32)
    pos_s = PAST_LEN + jnp.arange(td, dtype=jnp.int32)
    zero_ret = jnp.zeros((1, bp) + state_ret.shape[2:], F32)
    n_even = state_ret.shape[0]
    w_out_ev = ev_w_out.astype(BF16)
    w_out_od = od_w_out.astype(BF16)
    w_down = ffn_w_down.astype(BF16)

    ret_p = ret_s = hg_s = None
    n_h_p, n_h_s, n_cv_p, n_cv_s, n_hg_p, n_ff_p, n_ff_s = [], [], [], [], [], [], []
    tm_ff = min(ROW_TILE, m_s)
    tm_ln = min(LN_ROW_TILE, m_s)
    for l in range(depth):
        if l % 2 == 0:
            e = l // 2
            p = _matmul(xb, ev_w_in, e, tm_ff, 1024)
            chunk = RET_CHUNK if tp % RET_CHUNK == 0 else tp
            o_ret, ret_p = _retention(p, zero_ret, pos_p, layer=e, n_layers=n_even, prev_state=ret_p, row0=0,
                                      n_batch=bp, n_seq=1, chunk=chunk, n_chunks=tp // chunk, n_heads=h_ret,
                                      dk=dk_ret)
            g_ret = min(RET_SAMPLE_GROUP, bd)
            o_ret_s, ret_s = _retention(p, state_ret, pos_s, layer=e, n_layers=n_even, prev_state=ret_s,
                                        row0=m_p, n_batch=bd // g_ret, n_seq=g_ret, chunk=td, n_chunks=1,
                                        n_heads=h_ret, dk=dk_ret)
            wa = ev_rg_wa[e].astype(BF16)
            wx = ev_rg_wx[e].astype(BF16)
            rg_args = (ev_rg_conv_w[e], ev_rg_conv_b[e], wa, wx, ev_rg_ba[e], ev_rg_bx[e], ev_rg_lambda[e])
            o_rg, hl_p, xt_p = _rg_prompt(p, *rg_args, n_batch=bp, seq=tp, col0=4 * d_ret, d_rg=d_rg,
                                          total_rows=m)
            e_prev = jnp.stack(_prev_rows(state_rglru_conv[e], td))
            o_rg_s, h_s = _rg_sample(p, e_prev, _first_rows(state_rglru_h[e], td), *rg_args,
                                     row0=m_p, m_s=m_s, dec_seq=td, col0=4 * d_ret, d_rg=d_rg)
            n_conv = state_rglru_conv.shape[2]
            xr_s = p[m_p:, 4 * d_ret:4 * d_ret + d_rg].reshape(bd, td, d_rg)
            n_h_p.append(hl_p[:, 0])
            n_h_s.append(h_s.reshape(bd, td, d_rg)[:, td - 1])
            n_cv_p.append(xt_p[:, SUBLANES - n_conv:])
            n_cv_s.append(xr_s[:, td - n_conv:])
            o_ret = jnp.concatenate([o_ret, o_ret_s], axis=0)
            o_rg = lax.dynamic_update_slice(o_rg, o_rg_s, (m_p, 0))
            x, xb = _proj_ln([o_ret, o_rg], w_out_ev, e, x, ln_g[l, 0], ln_b[l, 0], alpha, tm_ln)
        else:
            o = l // 2
            p = _matmul(xb, od_w_in, o, tm_ff, 1024)
            o_hg, g_p = _gla_prompt(p, od_lb_logits, od_norm_g[o], layer=o, n_batch=bp, seq=tp,
                                    n_heads=h_hg, total_rows=m)
            p4 = p[m_p:].reshape(bd, td, 4 * h_hg, LANES)
            o_hg_s, hg_s = _gla_sample(p4, od_lb_logits, od_norm_g[o], state_hgrn, layer=o, prev_state=hg_s)
            o_hg = lax.dynamic_update_slice(o_hg, o_hg_s.reshape(m_s, h_hg * LANES), (m_p, 0))
            n_hg_p.append(g_p)
            x, xb = _proj_ln([o_hg], w_out_od, o, x, ln_g[l, 0], ln_b[l, 0], alpha, tm_ln)
        e1, e2 = _prev_rows(state_ffn_conv[l], td)
        h, tails, u_s = _ffn_up(xb, ffn_w_up, l, ffn_conv_w[l], ffn_conv_b[l], e1, e2,
                                m_prompt=m_p, seq=tp, dec_seq=td, tm=tm_ff, tn=512)
        tiles_per_seq = tp // tm_ff
        tails = tails.reshape(m // tm_ff, SUBLANES, d_ff)[:bp * tiles_per_seq]
        tails = tails.reshape(bp, tiles_per_seq, SUBLANES, d_ff)
        n_ff_p.append(tails[:, tiles_per_seq - 1, SUBLANES - 2:])
        n_ff_s.append(u_s.reshape(bd, td, d_ff)[:, td - 2:])
        x, xb = _proj_ln([h], w_down, l, x, ln_g[l, 1], ln_b[l, 1], alpha, tm_ln)

    y_prompt = x[:m_p].reshape(bp, tp, d_model)
    y_sample = x[m_p:].reshape(bd, td, d_model)
    return (y_prompt, y_sample, ret_p, ret_s, jnp.stack(n_h_p), jnp.stack(n_h_s),
            jnp.stack(n_cv_p), jnp.stack(n_cv_s), jnp.stack(n_hg_p), hg_s,
            jnp.stack(n_ff_p), jnp.stack(n_ff_s))
```

```python
import functools

import numpy as np
import jax
import jax.numpy as jnp
from jax import lax
from jax.experimental import pallas as pl
from jax.experimental.pallas import tpu as pltpu

F32 = jnp.float32
BF16 = jnp.bfloat16

LN_EPS = 1e-5
NORM_EPS = 1e-6
ROPE_BASE = 10000.0
RG_C = 8.0
PAST_LEN = 16384
LANES = 128
SUBLANES = 8
MXU_COLS = 256
ROW_TILE = 512
FFN_ROW_SUB = 256
LN_ROW_TILE = 256
RET_CHUNK = 128
RET_SAMPLE_GROUP = 8
RG_TIME_TILE = 256
GLA_BLOCK = 128
GLA_HEADS_PER_STEP = 8
GLA_SAMPLE_GROUP = 4
MIB = 1024 * 1024


def _cparams(n_axes, vmem_mib=48):
    return pltpu.CompilerParams(dimension_semantics=("arbitrary",) * n_axes,
                                vmem_limit_bytes=vmem_mib * MIB)


def _dot(a, b):
    return jnp.dot(a.astype(BF16), b.astype(BF16), preferred_element_type=F32)


def _dot_nt(a, b):
    return lax.dot_general(a.astype(BF16), b.astype(BF16), (((1,), (1,)), ((), ())),
                           preferred_element_type=F32)


def _dot_tn(a, b):
    return lax.dot_general(a.astype(BF16), b.astype(BF16), (((0,), (0,)), ((), ())),
                           preferred_element_type=F32)


def _silu(x):
    return x * jax.nn.sigmoid(x)


def _mm_body(x_ref, w_ref, o_ref, wb_scr):
    @pl.when(pl.program_id(1) == 0)
    def _():
        wb_scr[...] = w_ref[...].astype(BF16)

    o_ref[...] = jnp.dot(x_ref[...], wb_scr[...], preferred_element_type=F32).astype(o_ref.dtype)


def _matmul(xb, w, layer, tm, tn):
    m, k = xb.shape
    n = w.shape[2]
    assert m % tm == 0 and n % tn == 0
    return pl.pallas_call(
        _mm_body,
        grid=(n // tn, m // tm),
        in_specs=[pl.BlockSpec((tm, k), lambda j, i: (i, 0)),
                  pl.BlockSpec((None, k, tn), lambda j, i: (layer, 0, j))],
        out_specs=pl.BlockSpec((tm, tn), lambda j, i: (i, j)),
        out_shape=jax.ShapeDtypeStruct((m, n), F32),
        scratch_shapes=[pltpu.VMEM((k, tn), BF16)],
        compiler_params=_cparams(2),
        name="in_proj",
    )(xb, w)


def _proj_ln_body(*refs, part_arity, alpha, n_prompt_tiles, split_out):
    i = pl.program_id(0)
    n_in = sum(part_arity)
    a_refs = refs[:n_in]
    w_refs = refs[n_in:n_in + len(part_arity)]
    x_ref, g_ref, b_ref, o1_ref, o2_ref = refs[n_in + len(part_arity):]
    acc = None
    pos = 0
    for arity, w_ref in zip(part_arity, w_refs):
        if arity == 1:
            a = a_refs[pos][...]
        else:
            a = jnp.where(i < n_prompt_tiles, a_refs[pos][...], a_refs[pos + 1][...])
        pos += arity
        d = jnp.dot(a, w_ref[...], preferred_element_type=F32)
        acc = d if acc is None else acc + d
    y = alpha * x_ref[...] + acc
    mu = jnp.mean(y, axis=-1, keepdims=True)
    d = y - mu
    var = jnp.mean(d * d, axis=-1, keepdims=True)
    out = d * lax.rsqrt(var + LN_EPS) * g_ref[...] + b_ref[...]
    if split_out:
        @pl.when(i < n_prompt_tiles)
        def _():
            o1_ref[...] = out

        @pl.when(i >= n_prompt_tiles)
        def _():
            o2_ref[...] = out
    else:
        o1_ref[...] = out
        o2_ref[...] = out.astype(BF16)


def _proj_ln(parts, wb, layer, x, g, b, alpha, tm, m_prompt, split_out=False):
    m, d = x.shape
    assert m % tm == 0 and m_prompt % tm == 0
    npt = m_prompt // tm
    kp = parts[0][0].shape[1]
    assert all(a.shape[1] == kp for p in parts for a in p) and wb.shape[1] == kp * len(parts)
    prompt_rows = lambda i: (jnp.minimum(i, npt - 1), 0)
    sample_rows = lambda i: (jnp.maximum(i - npt, 0), 0)

    in_specs, args = [], []
    for p in parts:
        args += list(p)
        if len(p) == 1:
            in_specs.append(pl.BlockSpec((tm, kp), lambda i: (i, 0)))
        else:
            in_specs += [pl.BlockSpec((tm, kp), prompt_rows), pl.BlockSpec((tm, kp), sample_rows)]
    for part in range(len(parts)):
        in_specs.append(pl.BlockSpec((None, kp, d), lambda i, part=part: (layer, part, 0),
                                     pipeline_mode=pl.Buffered(1)))
    in_specs += [pl.BlockSpec((tm, d), lambda i: (i, 0)),
                 pl.BlockSpec((1, d), lambda i: (0, 0)),
                 pl.BlockSpec((1, d), lambda i: (0, 0))]
    if split_out:
        out_specs = [pl.BlockSpec((tm, d), prompt_rows), pl.BlockSpec((tm, d), sample_rows)]
        out_shape = [jax.ShapeDtypeStruct((m_prompt, d), F32), jax.ShapeDtypeStruct((m - m_prompt, d), F32)]
    else:
        out_specs = [pl.BlockSpec((tm, d), lambda i: (i, 0)), pl.BlockSpec((tm, d), lambda i: (i, 0))]
        out_shape = [jax.ShapeDtypeStruct((m, d), F32), jax.ShapeDtypeStruct((m, d), BF16)]
    return pl.pallas_call(
        functools.partial(_proj_ln_body, part_arity=tuple(len(p) for p in parts), alpha=alpha,
                          n_prompt_tiles=npt, split_out=split_out),
        grid=(m // tm,),
        in_specs=in_specs,
        out_specs=out_specs,
        out_shape=out_shape,
        compiler_params=_cparams(1, vmem_mib=56),
        name="proj_ln",
    )(*args, *([wb] * len(parts)), x, g.reshape(1, d), b.reshape(1, d))


def _ffn_up_body(x_ref, wu_ref, wv_ref, cw_ref, cb_ref, b0_ref, b1_ref, *rest,
                 n_prompt_tiles, tiles_per_seq, dec_seq):
    h_ref, tail_ref, fs_ref, wub_scr, wvb_scr, carry_ref, e1_scr, e2_scr, us_scr = rest[-9:]
    i = pl.program_id(1)

    @pl.when(i == 0)
    def _():
        wub_scr[...] = wu_ref[...].astype(BF16)
        wvb_scr[...] = wv_ref[...].astype(BF16)

    tm = x_ref.shape[0]
    tn = h_ref.shape[1]
    cw = min(MXU_COLS, tn)
    rs = min(FFN_ROW_SUB, tm)
    lane_blocks = cw // LANES
    row = lax.broadcasted_iota(jnp.int32, (rs, cw), 0)
    row8 = lax.broadcasted_iota(jnp.int32, (SUBLANES, cw), 0)

    def dots(c, r):
        cols = slice(c * cw, (c + 1) * cw)
        x = x_ref[r * rs:(r + 1) * rs, :]
        return (jnp.dot(x, wub_scr[:, cols], preferred_element_type=F32),
                jnp.dot(x, wvb_scr[:, cols], preferred_element_type=F32))

    def epilogue(c, r, u, v, fix):
        cols = slice(c * cw, (c + 1) * cw)
        u1, u2 = fix(pltpu.roll(u, 1, axis=0), pltpu.roll(u, 2, axis=0))
        uc = u2 * cw_ref[0:1, cols] + cb_ref[:, cols]
        uc = uc + u1 * cw_ref[1:2, cols]
        uc = uc + u * cw_ref[2:3, cols]
        h_ref[r * rs:(r + 1) * rs, cols] = (jax.nn.gelu(uc) * v).astype(h_ref.dtype)

    def sub_tile(c, r, fix):
        u, v = dots(c, r)
        epilogue(c, r, u, v, fix)
        return u

    @pl.when(i < n_prompt_tiles)
    def _():
        first = i % tiles_per_seq == 0
        n_r = tm // rs

        def prompt_fix(prev):
            def fix(r1, r2):
                p_last = prev[SUBLANES - 1:SUBLANES]
                p_prev = prev[SUBLANES - 2:SUBLANES - 1]
                h1 = jnp.where(row8 == 0, p_last, r1[:SUBLANES])
                h2 = jnp.where(row8 == 0, p_prev, jnp.where(row8 == 1, p_last, r2[:SUBLANES]))
                return (jnp.concatenate([h1, r1[SUBLANES:]], axis=0), jnp.concatenate([h2, r2[SUBLANES:]], axis=0))
            return fix

        pending = None
        for c in range(tn // cw):
            cols = slice(c * cw, (c + 1) * cw)
            prev = jnp.where(first, 0.0, carry_ref[:, cols])
            for r in range(n_r):
                u, v = dots(c, r)
                if pending is not None:
                    epilogue(*pending)
                pending = (c, r, u, v, prompt_fix(prev))
                prev = u[rs - SUBLANES:]
            carry_ref[:, cols] = prev
            tail_ref[:, cols] = prev
        epilogue(*pending)

    @pl.when(i >= n_prompt_tiles)
    def _():
        t = row % dec_seq
        n_seq = tm // dec_seq
        e1_scr[...] = jnp.zeros_like(e1_scr)
        e2_scr[...] = jnp.zeros_like(e2_scr)
        for lb in range(tn // LANES):
            lanes = slice(lb * LANES, (lb + 1) * LANES)
            e1_scr[lb, pl.ds(0, n_seq, stride=dec_seq), :] = b1_ref[:, lanes]
            e2_scr[lb, pl.ds(0, n_seq, stride=dec_seq), :] = b0_ref[:, lanes]
            e2_scr[lb, pl.ds(1, n_seq, stride=dec_seq), :] = b1_ref[:, lanes]
        for c in range(tn // cw):
            for r in range(tm // rs):
                rows = slice(r * rs, (r + 1) * rs)
                lbs = range(c * lane_blocks, (c + 1) * lane_blocks)
                e1 = jnp.concatenate([e1_scr[lb, rows, :] for lb in lbs], axis=1)
                e2 = jnp.concatenate([e2_scr[lb, rows, :] for lb in lbs], axis=1)
                u = sub_tile(c, r, lambda r1, r2: (jnp.where(t >= 1, r1, e1), jnp.where(t >= 2, r2, e2)))
                for k, lb in enumerate(lbs):
                    us_scr[lb, rows, :] = u[:, k * LANES:(k + 1) * LANES]
        for lb in range(tn // LANES):
            lanes = slice(lb * LANES, (lb + 1) * LANES)
            fs_ref[0, :, lanes] = us_scr[lb, pl.ds(dec_seq - 2, n_seq, stride=dec_seq), :]
            fs_ref[1, :, lanes] = us_scr[lb, pl.ds(dec_seq - 1, n_seq, stride=dec_seq), :]
        tail_ref[...] = jnp.zeros_like(tail_ref)


def _ffn_up(xb, w_up, layer, cw, cb, conv_state, prev_state, *, m_prompt, seq, dec_seq, tm, tn):
    m, k = xb.shape
    d_ff = w_up.shape[2] // 2
    n_layers, bd = conv_state.shape[:2]
    m_s = m - m_prompt
    assert m_prompt % tm == 0 and m_s % tm == 0 and seq % tm == 0 and d_ff % tn == 0
    assert tm % dec_seq == 0 and dec_seq >= 2 and cw.shape[0] == 3 and conv_state.shape[2] == 2
    assert tn % min(MXU_COLS, tn) == 0 and tm % min(FFN_ROW_SUB, tm) == 0 and m_s == bd * dec_seq
    npt = m_prompt // tm
    nj = d_ff // tn
    n_seq = tm // dec_seq
    state2d = conv_state.reshape(n_layers, bd, 2 * d_ff)
    in_specs = [pl.BlockSpec((tm, k), lambda j, i: (i, 0)),
                pl.BlockSpec((None, k, tn), lambda j, i: (layer, 0, j)),
                pl.BlockSpec((None, k, tn), lambda j, i: (layer, 0, j + nj)),
                pl.BlockSpec((3, tn), lambda j, i: (0, j)),
                pl.BlockSpec((1, tn), lambda j, i: (0, j)),
                pl.BlockSpec((None, n_seq, tn), lambda j, i: (layer, jnp.maximum(i - npt, 0), j)),
                pl.BlockSpec((None, n_seq, tn), lambda j, i: (layer, jnp.maximum(i - npt, 0), j + nj))]
    args = [xb, w_up, w_up, cw, cb.reshape(1, d_ff), state2d, state2d]
    fs_shape, fs_spec, extra, extra_specs, aliases = _stacked_out(
        n_layers, layer, (2, bd, d_ff), (2, n_seq, tn), lambda j, i: (0, jnp.maximum(i - npt, 0), j),
        prev_state, len(args), out_index=2)
    n_lb = tn // LANES
    return pl.pallas_call(
        functools.partial(_ffn_up_body, n_prompt_tiles=npt, tiles_per_seq=seq // tm, dec_seq=dec_seq),
        grid=(nj, m // tm),
        in_specs=in_specs + extra_specs,
        out_specs=[pl.BlockSpec((tm, tn), lambda j, i: (i, j)),
                   pl.BlockSpec((SUBLANES, tn), lambda j, i: (i, j)),
                   fs_spec],
        out_shape=[jax.ShapeDtypeStruct((m, d_ff), BF16),
                   jax.ShapeDtypeStruct((m // tm * SUBLANES, d_ff), F32),
                   fs_shape],
        input_output_aliases=aliases,
        scratch_shapes=[pltpu.VMEM((k, tn), BF16), pltpu.VMEM((k, tn), BF16), pltpu.VMEM((SUBLANES, tn), F32),
                        pltpu.VMEM((n_lb, tm, LANES), F32), pltpu.VMEM((n_lb, tm, LANES), F32),
                        pltpu.VMEM((n_lb, tm, LANES), F32)],
        compiler_params=_cparams(2),
        name="ffn_up",
    )(*args, *extra)


def _ret_body(q_ref, k_ref, v_ref, g_ref, cos_ref, sin_ref, dm_ref, xi_ref, zeta_ref, cdec_ref, s0_ref,
              *rest, n_seq, chunk, n_chunks, scale):
    o_ref, so_ref, s_scr, q_scr, kz_scr, o_scr = rest[-6:]
    n = pl.program_id(2)

    @pl.when(n == 0)
    def _():
        s_scr[...] = s0_ref[...]

    half = q_ref.shape[1] // 2
    cos = cos_ref[...]
    sin = sin_ref[...]

    def rot(x):
        x1 = x[:, :half]
        x2 = x[:, half:]
        return jnp.concatenate([x1 * cos - x2 * sin, x2 * cos + x1 * sin], axis=-1)

    q = rot(q_ref[...])
    k = rot(k_ref[...]) * scale
    vb = v_ref[...].astype(BF16)
    inner = _dot_nt(q, k) * dm_ref[...]
    o_scr[...] = _dot(inner, vb)
    q_scr[...] = q
    kz_scr[...] = k * zeta_ref[...]
    cdec = cdec_ref[...]
    for gi in range(n_seq):
        rows = pl.ds(gi * chunk, chunk)
        s = s_scr[gi]
        o_scr[rows, :] += _dot(q_scr[rows, :], s) * xi_ref[rows, :]
        s_scr[gi] = s * cdec + _dot_tn(kz_scr[rows, :], v_ref[rows, :])
    o = o_scr[...]
    gate = g_ref[...]
    o = o * lax.rsqrt(jnp.mean(o * o, axis=-1, keepdims=True) + NORM_EPS) * _silu(gate)
    o_ref[...] = o.astype(o_ref.dtype)

    @pl.when(n == n_chunks - 1)
    def _():
        so_ref[...] = s_scr[...]


def _ret_tables(n_heads, chunk, n_seq):
    lg = jnp.log(1.0 - 2.0 ** (-5.0 - jnp.arange(n_heads, dtype=F32)))
    idx = jnp.arange(chunk, dtype=F32)
    diff = idx[:, None] - idx[None, :]
    dmask = jnp.where(diff[None] >= 0, jnp.exp(jnp.maximum(diff, 0.0)[None] * lg[:, None, None]), 0.0)
    eye = jnp.eye(n_seq, dtype=F32)
    dm = jnp.einsum('ab,hts->hatbs', eye, dmask).reshape(n_heads, n_seq * chunk, n_seq * chunk)
    xi = jnp.tile(jnp.exp((idx[None, :] + 1.0) * lg[:, None]), (1, n_seq))[:, :, None]
    zeta = jnp.tile(jnp.exp((chunk - 1.0 - idx[None, :]) * lg[:, None]), (1, n_seq))[:, :, None]
    cdec = jnp.exp(chunk * lg)[:, None, None]
    return dm, xi, zeta, cdec


def _rope_tables(pos, half):
    inv = ROPE_BASE ** (-jnp.arange(half, dtype=F32) / half)
    ang = pos.astype(F32)[:, None] * inv[None, :]
    return jnp.cos(ang), jnp.sin(ang)


def _stacked_out(n_layers, layer, shape, block, index_map, prev, n_inputs, out_index=1):
    sds = jax.ShapeDtypeStruct((n_layers,) + tuple(shape), F32)
    spec = pl.BlockSpec((None,) + tuple(block), lambda *g: (layer,) + tuple(index_map(*g)))
    if prev is None:
        return sds, spec, [], [], {}
    return sds, spec, [prev], [pl.BlockSpec(memory_space=pl.ANY)], {n_inputs: out_index}


def _retention(p, s0, pos, *, layer, n_layers, prev_state, row0, n_batch, n_seq, chunk, n_chunks, n_heads, dk):
    rows = n_seq * chunk
    assert row0 % rows == 0
    rb0 = row0 // rows
    dm, xi, zeta, cdec = _ret_tables(n_heads, chunk, n_seq)
    cos, sin = _rope_tables(pos, dk // 2)
    if n_seq > 1:
        assert n_chunks == 1
        cos = jnp.tile(cos, (n_seq, 1))
        sin = jnp.tile(sin, (n_seq, 1))
    hq = n_heads

    def col(off):
        return lambda b, h, n: (rb0 + b * n_chunks + n, off * hq + h)

    tab = lambda b, h, n: (h, 0, 0)
    s0_layer = layer if s0.shape[0] > 1 else 0
    in_specs = [pl.BlockSpec((rows, dk), col(0)), pl.BlockSpec((rows, dk), col(1)),
                pl.BlockSpec((rows, dk), col(2)), pl.BlockSpec((rows, dk), col(3)),
                pl.BlockSpec((rows, dk // 2), lambda b, h, n: (n, 0)),
                pl.BlockSpec((rows, dk // 2), lambda b, h, n: (n, 0)),
                pl.BlockSpec((None, rows, rows), tab),
                pl.BlockSpec((None, rows, 1), tab),
                pl.BlockSpec((None, rows, 1), tab),
                pl.BlockSpec((None, 1, 1), tab),
                pl.BlockSpec((None, n_seq, None, dk, dk), lambda b, h, n: (s0_layer, b, h, 0, 0))]
    args = [p, p, p, p, cos, sin, dm, xi, zeta, cdec, s0]
    so_shape, so_spec, extra, extra_specs, aliases = _stacked_out(
        n_layers, layer, (n_batch * n_seq, n_heads, dk, dk), (n_seq, None, dk, dk),
        lambda b, h, n: (b, h, 0, 0), prev_state, len(args))
    total_rows = n_batch * n_chunks * rows
    return pl.pallas_call(
        functools.partial(_ret_body, n_seq=n_seq, chunk=chunk, n_chunks=n_chunks, scale=dk ** -0.5),
        grid=(n_batch, n_heads, n_chunks),
        in_specs=in_specs + extra_specs,
        out_specs=[pl.BlockSpec((rows, dk), lambda b, h, n: (b * n_chunks + n, h)), so_spec],
        out_shape=[jax.ShapeDtypeStruct((total_rows, n_heads * dk), BF16), so_shape],
        input_output_aliases=aliases,
        scratch_shapes=[pltpu.VMEM((n_seq, dk, dk), F32), pltpu.VMEM((rows, dk), F32),
                        pltpu.VMEM((rows, dk), F32), pltpu.VMEM((rows, dk), F32)],
        compiler_params=_cparams(3),
        name="retention",
    )(*args, *extra)


def _rg_gates(xc, wa_ref, wx_ref, ba, bx, lam):
    nb, bw, _ = wa_ref.shape
    ra, ia = [], []
    for blk in range(nb):
        xb = xc[:, blk * bw:(blk + 1) * bw].astype(BF16)
        ra.append(jnp.dot(xb, wa_ref[blk], preferred_element_type=F32))
        ia.append(jnp.dot(xb, wx_ref[blk], preferred_element_type=F32))
    r = jax.nn.sigmoid(jnp.concatenate(ra, axis=-1) + ba)
    i = jax.nn.sigmoid(jnp.concatenate(ia, axis=-1) + bx)
    log_a = -RG_C * r * jax.nn.softplus(-lam)
    a = jnp.exp(log_a)
    mult = jnp.sqrt(-jnp.tanh(log_a) * (a * a + 1.0))
    return a, mult, i


def _scan_rows(a, b, row_in_seg, seg_len):
    s = 1
    while s < seg_len:
        a_sh = pltpu.roll(a, s, axis=0)
        b_sh = pltpu.roll(b, s, axis=0)
        m = row_in_seg >= s
        b = jnp.where(m, a * b_sh + b, b)
        a = jnp.where(m, a * a_sh, a)
        s *= 2
    return a, b


def _rg_prompt_body(xr_ref, gr_ref, cw_ref, cb_ref, wa_ref, wx_ref, ba_ref, bx_ref, lam_ref,
                    o_ref, hl_ref, xt_ref, h_scr, x_scr, *, n_tiles):
    n = pl.program_id(1)

    @pl.when(n == 0)
    def _():
        h_scr[...] = jnp.zeros_like(h_scr)
        x_scr[...] = jnp.zeros_like(x_scr)

    xr = xr_ref[...]
    tc = xr.shape[0]
    row = lax.broadcasted_iota(jnp.int32, xr.shape, 0)
    prev = x_scr[...]
    n_tap = cw_ref.shape[0]
    xc = None
    for j in range(n_tap):
        d = n_tap - 1 - j
        if d == 0:
            xd = xr
        else:
            prev_d = jnp.tile(pltpu.roll(prev, d, axis=0), (tc // SUBLANES, 1))
            xd = jnp.where(row < d, prev_d, pltpu.roll(xr, d, axis=0))
        term = xd * cw_ref[j:j + 1, :]
        xc = term + cb_ref[...] if xc is None else xc + term
    a, mult, gate_i = _rg_gates(xc, wa_ref, wx_ref, ba_ref[...], bx_ref[...], lam_ref[...])
    mult = jnp.where(row + n * tc == 0, 1.0, mult)
    bterm = xc * gate_i * mult
    a_cum, b_cum = _scan_rows(a, bterm, row, tc)
    h = a_cum * h_scr[...] + b_cum
    o_ref[...] = (h * jax.nn.gelu(gr_ref[...])).astype(o_ref.dtype)
    h_scr[...] = h[tc - 1:tc, :]
    x_scr[...] = xr[tc - SUBLANES:, :]

    @pl.when(n == n_tiles - 1)
    def _():
        hl_ref[...] = h[tc - 1:tc, :]
        xt_ref[...] = xr[tc - SUBLANES:, :]


def _rg_prompt(p, cw, cb, wa, wx, ba, bx, lam, *, n_batch, seq, col0, d_rg, total_rows):
    tc = min(RG_TIME_TILE, seq)
    assert seq % tc == 0 and col0 % d_rg == 0 and (tc & (tc - 1)) == 0
    nt = seq // tc
    cb0 = col0 // d_rg
    vec = lambda a: a.reshape(1, d_rg)
    full = lambda a: pl.BlockSpec(a.shape, lambda b, n: (0,) * a.ndim)
    args = (cw, vec(cb), wa, wx, vec(ba), vec(bx), vec(lam))
    return pl.pallas_call(
        functools.partial(_rg_prompt_body, n_tiles=nt),
        grid=(n_batch, nt),
        in_specs=[pl.BlockSpec((tc, d_rg), lambda b, n: (b * nt + n, cb0)),
                  pl.BlockSpec((tc, d_rg), lambda b, n: (b * nt + n, cb0 + 1))] + [full(a) for a in args],
        out_specs=[pl.BlockSpec((tc, d_rg), lambda b, n: (b * nt + n, 0)),
                   pl.BlockSpec((None, 1, d_rg), lambda b, n: (b, 0, 0)),
                   pl.BlockSpec((None, SUBLANES, d_rg), lambda b, n: (b, 0, 0))],
        out_shape=[jax.ShapeDtypeStruct((total_rows, d_rg), BF16),
                   jax.ShapeDtypeStruct((n_batch, 1, d_rg), F32),
                   jax.ShapeDtypeStruct((n_batch, SUBLANES, d_rg), F32)],
        scratch_shapes=[pltpu.VMEM((1, d_rg), F32), pltpu.VMEM((SUBLANES, d_rg), F32)],
        compiler_params=_cparams(2),
        name="rglru_prompt",
    )(p, p, *args)


def _rg_sample_body(xr_ref, gr_ref, e_ref, h0_ref, cw_ref, cb_ref, wa_ref, wx_ref, ba_ref, bx_ref, lam_ref,
                    o_ref, h_ref, *, dec_seq, first_pos):
    xr = xr_ref[...]
    row = lax.broadcasted_iota(jnp.int32, xr.shape, 0)
    t = row % dec_seq
    n_tap = cw_ref.shape[0]
    xc = None
    for j in range(n_tap):
        d = n_tap - 1 - j
        xd = xr if d == 0 else jnp.where(t >= d, pltpu.roll(xr, d, axis=0), e_ref[d - 1])
        term = xd * cw_ref[j:j + 1, :]
        xc = term + cb_ref[...] if xc is None else xc + term
    a, mult, gate_i = _rg_gates(xc, wa_ref, wx_ref, ba_ref[...], bx_ref[...], lam_ref[...])
    if first_pos == 0:
        mult = jnp.where(t == 0, 1.0, mult)
    bterm = xc * gate_i * mult + jnp.where(t == 0, a * h0_ref[...], 0.0)
    _, h = _scan_rows(a, bterm, t, dec_seq)
    h_ref[...] = h
    o_ref[...] = (h * jax.nn.gelu(gr_ref[...])).astype(o_ref.dtype)


def _rg_sample(p, e_prev, h0_rows, cw, cb, wa, wx, ba, bx, lam, *, row0, m_s, dec_seq, col0, d_rg):
    tr = min(128, m_s)
    assert m_s % tr == 0 and row0 % tr == 0 and tr % dec_seq == 0 and (dec_seq & (dec_seq - 1)) == 0
    rb0 = row0 // tr
    cb0 = col0 // d_rg
    vec = lambda a: a.reshape(1, d_rg)
    full = lambda a: pl.BlockSpec(a.shape, lambda i: (0,) * a.ndim)
    args = (cw, vec(cb), wa, wx, vec(ba), vec(bx), vec(lam))
    n_prev = e_prev.shape[0]
    return pl.pallas_call(
        functools.partial(_rg_sample_body, dec_seq=dec_seq, first_pos=PAST_LEN),
        grid=(m_s // tr,),
        in_specs=[pl.BlockSpec((tr, d_rg), lambda i: (rb0 + i, cb0)),
                  pl.BlockSpec((tr, d_rg), lambda i: (rb0 + i, cb0 + 1)),
                  pl.BlockSpec((n_prev, tr, d_rg), lambda i: (0, i, 0)),
                  pl.BlockSpec((tr, d_rg), lambda i: (i, 0))] + [full(a) for a in args],
        out_specs=[pl.BlockSpec((tr, d_rg), lambda i: (i, 0)), pl.BlockSpec((tr, d_rg), lambda i: (i, 0))],
        out_shape=[jax.ShapeDtypeStruct((m_s, d_rg), BF16), jax.ShapeDtypeStruct((m_s, d_rg), F32)],
        compiler_params=_cparams(1),
        name="rglru_sample",
    )(p, p, e_prev, h0_rows, *args)


def _lower_bound(logit_rows, layer):
    m = logit_rows[0]
    for z in logit_rows[1:]:
        m = jnp.maximum(m, z)
    e = [jnp.exp(z - m) for z in logit_rows]
    tot = e[0]
    for x in e[1:]:
        tot = tot + x
    if layer == 0:
        return jnp.zeros_like(tot)
    num = e[1]
    for x in e[2:layer + 1]:
        num = num + x
    return num / tot


_GLA_VREGS = GLA_BLOCK // SUBLANES


def _gla_level_masks():
    r = np.arange(GLA_BLOCK)
    t = r // SUBLANES + _GLA_VREGS * (r % SUBLANES)
    tq, tk = t[:, None], t[None, :]
    n_levels = GLA_BLOCK.bit_length()
    masks = np.zeros((n_levels, GLA_BLOCK, GLA_BLOCK), np.float32)
    masks[0] = tq == tk
    for lvl in range(1, n_levels):
        gs = 1 << lvl
        masks[lvl] = (tq > tk) & (tq // gs == tk // gs) & (tq // (gs // 2) != tk // (gs // 2))
    return masks


def _gla_prompt_body(*refs, heads, layer, n_blocks):
    q_refs, f_refs, i_refs, g_refs = (refs[p * heads:(p + 1) * heads] for p in range(4))
    lbl_ref, ng_ref, msk_ref, o_ref, so_ref, st_scr, o_scr = refs[4 * heads:]
    n = pl.program_id(2)

    @pl.when(n == 0)
    def _():
        st_scr[...] = jnp.zeros_like(st_scr)

    nv = _GLA_VREGS
    sub = lax.broadcasted_iota(jnp.int32, (SUBLANES, LANES), 0)
    zero = jnp.zeros((SUBLANES, LANES), F32)
    for hh in range(heads):
        cols = slice(hh * LANES, (hh + 1) * LANES)

        def load(ref):
            return jnp.concatenate([ref[pl.ds(j, SUBLANES, stride=nv), :] for j in range(nv)], axis=0)

        def groups(x):
            return [x[SUBLANES * j:SUBLANES * (j + 1)] for j in range(nv)]

        lb = _lower_bound([lbl_ref[r:r + 1, cols] for r in range(lbl_ref.shape[0])], layer)
        q = _silu(load(q_refs[hh]))
        fg = lb + (1.0 - lb) * jax.nn.sigmoid(load(f_refs[hh]))
        kk = 1.0 - fg
        v = load(i_refs[hh])
        vb = v.astype(BF16)
        lf = groups(jnp.log(fg))

        c = [lf[0]]
        for j in range(1, nv):
            c.append(c[-1] + lf[j])
        tot = c[nv - 1]
        x = tot
        s = 1
        while s < SUBLANES:
            x = x + jnp.where(sub >= s, pltpu.roll(x, s, axis=0), 0.0)
            s *= 2
        before = x - tot
        bj = [cj + before for cj in c]
        b = jnp.concatenate(bj, axis=0)
        b_last = bj[nv - 1][SUBLANES - 1:SUBLANES, :]

        qj = groups(q)
        kj = groups(kk)
        def keep(lvl, scores, acc):
            return jnp.where(msk_ref[lvl] > 0, scores, acc)

        a_mat = keep(0, _dot_nt(q, kk), jnp.zeros((GLA_BLOCK, GLA_BLOCK), F32))
        lvl = 1
        gs = 2
        while gs <= nv:
            hs = gs // 2
            qd, kd = [], []
            for j in range(nv):
                ref = (j // gs) * gs + hs - 1
                if j % gs >= hs:
                    qd.append(qj[j] * jnp.exp(bj[j] - bj[ref]))
                    kd.append(zero)
                else:
                    kd.append(kj[j] if j == ref else kj[j] * jnp.exp(bj[ref] - bj[j]))
                    qd.append(zero)
            a_mat = keep(lvl, _dot_nt(jnp.concatenate(qd, axis=0), jnp.concatenate(kd, axis=0)), a_mat)
            lvl += 1
            gs *= 2
        m = 2
        while m <= SUBLANES:
            src = sub - sub % m + (m // 2 - 1)
            ref = zero
            for s_src in range(m // 2 - 1, SUBLANES, m):
                row_b = jnp.broadcast_to(bj[nv - 1][s_src:s_src + 1, :], (SUBLANES, LANES))
                ref = jnp.where(src == s_src, row_b, ref)
            e = [jnp.exp(-jnp.abs(bj[j] - ref)) for j in range(nv)]
            qd = jnp.concatenate([qj[j] * e[j] for j in range(nv)], axis=0)
            kd = jnp.concatenate([kj[j] * e[j] for j in range(nv)], axis=0)
            a_mat = keep(lvl, _dot_nt(qd, kd), a_mat)
            lvl += 1
            m *= 2

        st = st_scr[hh]
        o = _dot(a_mat, vb) + _dot_nt(q * jnp.exp(b), st)
        st_new = st * jnp.exp(b_last) + _dot_tn(vb, kk * jnp.exp(b_last - b))
        st_scr[hh] = st_new

        o = o * lax.rsqrt(jnp.mean(o * o, axis=-1, keepdims=True) + NORM_EPS) * ng_ref[...]
        o = o * jax.nn.sigmoid(load(g_refs[hh]))
        for j in range(nv):
            o_scr[hh, pl.ds(j, SUBLANES, stride=nv), :] = o[SUBLANES * j:SUBLANES * (j + 1)]
        o_ref[:, cols] = o_scr[hh].astype(o_ref.dtype)

    @pl.when(n == n_blocks - 1)
    def _():
        for hh in range(heads):
            so_ref[hh] = st_scr[hh].T


def _gla_prompt(p, lb_logits, norm_g, *, layer, n_batch, seq, n_heads, total_rows):
    blk = GLA_BLOCK
    hb = GLA_HEADS_PER_STEP
    assert seq % blk == 0 and n_heads % hb == 0
    nb = seq // blk
    hg = n_heads // hb
    w = hb * LANES
    masks = jnp.asarray(_gla_level_masks())

    def head_spec(part, hh):
        return pl.BlockSpec((blk, LANES), lambda b, h, n: (b * nb + n, part * n_heads + h * hb + hh))

    return pl.pallas_call(
        functools.partial(_gla_prompt_body, heads=hb, layer=layer, n_blocks=nb),
        grid=(n_batch, hg, nb),
        in_specs=[head_spec(part, hh) for part in range(4) for hh in range(hb)] + [
                  pl.BlockSpec((lb_logits.shape[0], w), lambda b, h, n: (0, h)),
                  pl.BlockSpec((1, LANES), lambda b, h, n: (0, 0)),
                  pl.BlockSpec(masks.shape, lambda b, h, n: (0, 0, 0))],
        out_specs=[pl.BlockSpec((blk, w), lambda b, h, n: (b * nb + n, h)),
                   pl.BlockSpec((None, hb, LANES, LANES), lambda b, h, n: (b, h, 0, 0))],
        out_shape=[jax.ShapeDtypeStruct((total_rows, n_heads * LANES), BF16),
                   jax.ShapeDtypeStruct((n_batch, n_heads, LANES, LANES), F32)],
        scratch_shapes=[pltpu.VMEM((hb, LANES, LANES), F32), pltpu.VMEM((hb, blk, LANES), F32)],
        compiler_params=_cparams(3),
        name="hgrn_prompt",
    )(*([p] * (4 * hb)), lb_logits, norm_g.reshape(1, LANES), masks)


def _gla_sample_body(q_ref, f_ref, i_ref, g_ref, lbl_ref, ng_ref, s0_ref, *rest, layer):
    o_ref, so_ref, qd_scr, kd_scr, o_scr = rest[-5:]
    n_g, n_t, n_h, _ = q_ref.shape
    lb = _lower_bound([lbl_ref[r] for r in range(lbl_ref.shape[0])], layer)
    q = _silu(q_ref[...])
    fg = lb + (1.0 - lb) * jax.nn.sigmoid(f_ref[...])
    kk = 1.0 - fg
    lf = jnp.log(fg)
    v = i_ref[...]
    bt = [lf[:, 0]]
    for t in range(1, n_t):
        bt.append(bt[-1] + lf[:, t])
    b_last = bt[n_t - 1]
    for t in range(n_t):
        acc = None
        for s in range(t + 1):
            w = q[:, t] * kk[:, s]
            if s < t:
                w = w * jnp.exp(bt[t] - bt[s])
            term = jnp.sum(w, axis=-1, keepdims=True) * v[:, s]
            acc = term if acc is None else acc + term
        o_scr[:, t] = acc
        qd_scr[:, t] = q[:, t] * jnp.exp(bt[t])
        kd_scr[:, t] = kk[:, t] * jnp.exp(b_last - bt[t])
    e_last = jnp.exp(b_last)
    for gi in range(n_g):
        e_cols = e_last[gi].T
        for h in range(n_h):
            s = s0_ref[gi, h]
            o_scr[gi, :, h, :] += _dot(qd_scr[gi, :, h, :], s)
            so_ref[gi, h] = s * e_cols[:, h:h + 1] + _dot_tn(kd_scr[gi, :, h, :], i_ref[gi, :, h, :])
    o = o_scr[...]
    o = o * lax.rsqrt(jnp.mean(o * o, axis=-1, keepdims=True) + NORM_EPS) * ng_ref[...]
    o_ref[...] = (o * jax.nn.sigmoid(g_ref[...])).astype(o_ref.dtype)


def _gla_sample(p4, lb_logits, norm_g, s0, *, layer, prev_state, seq0):
    _, td, h4, _ = p4.shape
    bd = s0.shape[1]
    nh = h4 // 4
    g = min(GLA_SAMPLE_GROUP, bd)
    assert bd % g == 0 and seq0 % g == 0
    blk = (g, td, nh, LANES)
    sblk = (g, nh, LANES, LANES)
    lb3 = lb_logits.reshape(lb_logits.shape[0], nh, LANES)
    part = lambda off: pl.BlockSpec(blk, lambda i: (seq0 // g + i, 0, off, 0))
    in_specs = [part(0), part(1), part(2), part(3),
                pl.BlockSpec(lb3.shape, lambda i: (0, 0, 0)),
                pl.BlockSpec((1, LANES), lambda i: (0, 0)),
                pl.BlockSpec((None,) + sblk, lambda i: (layer, i, 0, 0, 0))]
    args = [p4, p4, p4, p4, lb3, norm_g.reshape(1, LANES), s0]
    so_shape, so_spec, extra, extra_specs, aliases = _stacked_out(
        s0.shape[0], layer, (bd, nh, LANES, LANES), sblk, lambda i: (i, 0, 0, 0), prev_state, len(args))
    return pl.pallas_call(
        functools.partial(_gla_sample_body, layer=layer),
        grid=(bd // g,),
        in_specs=in_specs + extra_specs,
        out_specs=[pl.BlockSpec(blk, lambda i: (i, 0, 0, 0)), so_spec],
        out_shape=[jax.ShapeDtypeStruct((bd, td, nh, LANES), BF16), so_shape],
        input_output_aliases=aliases,
        scratch_shapes=[pltpu.VMEM(blk, F32), pltpu.VMEM(blk, F32), pltpu.VMEM(blk, F32)],
        compiler_params=_cparams(1),
        name="hgrn_sample",
    )(*args, *extra)


def _prev_rows(buf, dec_seq):
    bd, n_prev, d = buf.shape
    outs = []
    for dd in range(1, n_prev + 1):
        rows = [buf[:, n_prev - dd + t] if t < dd else jnp.zeros((bd, d), buf.dtype) for t in range(dec_seq)]
        outs.append(jnp.stack(rows, axis=1).reshape(bd * dec_seq, d))
    return outs


def _first_rows(vals, dec_seq):
    bd, d = vals.shape
    z = jnp.zeros((bd, dec_seq - 1, d), vals.dtype)
    return jnp.concatenate([vals[:, None, :], z], axis=1).reshape(bd * dec_seq, d)


def kernel(x_prompt, x_sample, state_ret, state_rglru_h, state_rglru_conv, state_hgrn, state_ffn_conv,
           ev_w_in, ev_w_out, ev_rg_conv_w, ev_rg_conv_b, ev_rg_wa, ev_rg_ba, ev_rg_wx, ev_rg_bx, ev_rg_lambda,
           od_w_in, od_w_out, od_norm_g, od_lb_logits, ln_g, ln_b, ffn_w_up, ffn_conv_w, ffn_conv_b, ffn_w_down):
    bp, tp, d_model = x_prompt.shape
    bd, td, _ = x_sample.shape
    depth = ln_g.shape[0]
    m_p, m_s = bp * tp, bd * td
    m = m_p + m_s
    alpha = (2.0 * depth) ** 0.25
    h_ret, dk_ret = state_ret.shape[2], state_ret.shape[3]
    d_ret = h_ret * dk_ret
    d_rg = state_rglru_h.shape[-1]
    h_hg = state_hgrn.shape[2]
    d_ff = ffn_conv_b.shape[-1]
    assert state_hgrn.shape[3] == LANES and state_hgrn.shape[4] == LANES and d_ret == d_rg
    assert ffn_conv_w.shape[1] == 3 and td >= 3

    x = jnp.concatenate([x_prompt.reshape(m_p, d_model), x_sample.reshape(m_s, d_model)], axis=0)
    xb = x.astype(BF16)
    pos_p = jnp.arange(tp, dtype=jnp.int32)
    pos_s = PAST_LEN + jnp.arange(td, dtype=jnp.int32)
    zero_ret = jnp.zeros((1, bp) + state_ret.shape[2:], F32)
    n_even = state_ret.shape[0]
    w_out_ev = ev_w_out.astype(BF16)
    w_out_od = od_w_out.astype(BF16)
    w_down = ffn_w_down.astype(BF16)

    ret_p = ret_s = hg_s = ff_s = None
    n_h_p, n_h_s, n_cv_p, n_cv_s, n_hg_p, n_ff_p = [], [], [], [], [], []
    tm_ff = min(ROW_TILE, m_s)
    tm_ln = min(LN_ROW_TILE, m_s)
    for l in range(depth):
        if l % 2 == 0:
            e = l // 2
            p = _matmul(xb, ev_w_in, e, tm_ff, 1024)
            chunk = RET_CHUNK if tp % RET_CHUNK == 0 else tp
            o_ret, ret_p = _retention(p, zero_ret, pos_p, layer=e, n_layers=n_even, prev_state=ret_p, row0=0,
                                      n_batch=bp, n_seq=1, chunk=chunk, n_chunks=tp // chunk, n_heads=h_ret,
                                      dk=dk_ret)
            g_ret = min(RET_SAMPLE_GROUP, bd)
            o_ret_s, ret_s = _retention(p, state_ret, pos_s, layer=e, n_layers=n_even, prev_state=ret_s,
                                        row0=m_p, n_batch=bd // g_ret, n_seq=g_ret, chunk=td, n_chunks=1,
                                        n_heads=h_ret, dk=dk_ret)
            wa = ev_rg_wa[e].astype(BF16)
            wx = ev_rg_wx[e].astype(BF16)
            rg_args = (ev_rg_conv_w[e], ev_rg_conv_b[e], wa, wx, ev_rg_ba[e], ev_rg_bx[e], ev_rg_lambda[e])
            o_rg, hl_p, xt_p = _rg_prompt(p, *rg_args, n_batch=bp, seq=tp, col0=4 * d_ret, d_rg=d_rg,
                                          total_rows=m_p)
            e_prev = jnp.stack(_prev_rows(state_rglru_conv[e], td))
            o_rg_s, h_s = _rg_sample(p, e_prev, _first_rows(state_rglru_h[e], td), *rg_args,
                                     row0=m_p, m_s=m_s, dec_seq=td, col0=4 * d_ret, d_rg=d_rg)
            n_conv = state_rglru_conv.shape[2]
            xr_s = p[m_p:, 4 * d_ret:4 * d_ret + d_rg].reshape(bd, td, d_rg)
            n_h_p.append(hl_p[:, 0])
            n_h_s.append(h_s.reshape(bd, td, d_rg)[:, td - 1])
            n_cv_p.append(xt_p[:, SUBLANES - n_conv:])
            n_cv_s.append(xr_s[:, td - n_conv:])
            x, xb = _proj_ln([(o_ret, o_ret_s), (o_rg, o_rg_s)], w_out_ev, e, x, ln_g[l, 0], ln_b[l, 0], alpha,
                             tm_ln, m_p)
        else:
            o = l // 2
            p = _matmul(xb, od_w_in, o, tm_ff, 1024)
            o_hg, g_p = _gla_prompt(p, od_lb_logits, od_norm_g[o], layer=o, n_batch=bp, seq=tp,
                                    n_heads=h_hg, total_rows=m_p)
            p4 = p.reshape(m // td, td, 4 * h_hg, LANES)
            o_hg_s, hg_s = _gla_sample(p4, od_lb_logits, od_norm_g[o], state_hgrn, layer=o, prev_state=hg_s,
                                       seq0=m_p // td)
            n_hg_p.append(g_p)
            x, xb = _proj_ln([(o_hg, o_hg_s.reshape(m_s, h_hg * LANES))], w_out_od, o, x, ln_g[l, 0], ln_b[l, 0],
                             alpha, tm_ln, m_p)
        h, tails, ff_s = _ffn_up(xb, ffn_w_up, l, ffn_conv_w[l], ffn_conv_b[l], state_ffn_conv, ff_s,
                                 m_prompt=m_p, seq=tp, dec_seq=td, tm=tm_ff, tn=512)
        tiles_per_seq = tp // tm_ff
        tails = tails.reshape(m // tm_ff, SUBLANES, d_ff)[:bp * tiles_per_seq]
        tails = tails.reshape(bp, tiles_per_seq, SUBLANES, d_ff)
        n_ff_p.append(tails[:, tiles_per_seq - 1, SUBLANES - 2:])
        last = l == depth - 1
        x, xb = _proj_ln([(h,)], w_down, l, x, ln_g[l, 1], ln_b[l, 1], alpha, tm_ln, m_p, split_out=last)

    y_prompt = x.reshape(bp, tp, d_model)
    y_sample = xb.reshape(bd, td, d_model)
    return (y_prompt, y_sample, ret_p, ret_s, jnp.stack(n_h_p), jnp.stack(n_h_s),
            jnp.stack(n_cv_p), jnp.stack(n_cv_s), jnp.stack(n_hg_p), hg_s,
            jnp.stack(n_ff_p), jnp.swapaxes(ff_s, 1, 2))
```

```python
import functools

import numpy as np
import jax
import jax.numpy as jnp
from jax import lax
from jax.experimental import pallas as pl
from jax.experimental.pallas import tpu as pltpu

F32 = jnp.float32
BF16 = jnp.bfloat16

LN_EPS = 1e-5
NORM_EPS = 1e-6
ROPE_BASE = 10000.0
RG_C = 8.0
GELU_C0 = 0.7978845608028654
GELU_C1 = 0.044715
PAST_LEN = 16384
LANES = 128
SUBLANES = 8
MXU_COLS = 256
ROW_TILE = 512
FFN_ROW_SUB = 256
LN_ROW_TILE = 256
IN_PROJ_ROW_TILE = 256
IN_PROJ_COLS = 2048
RET_CHUNK = 128
RET_SAMPLE_GROUP = 8
RG_TIME_TILE = 256
GLA_BLOCK = 128
GLA_HEADS_PER_STEP = 8
GLA_SAMPLE_GROUP = 4
MIB = 1024 * 1024


def _cparams(n_axes, vmem_mib=48):
    return pltpu.CompilerParams(dimension_semantics=("arbitrary",) * n_axes,
                                vmem_limit_bytes=vmem_mib * MIB)


def _dot(a, b):
    return jnp.dot(a.astype(BF16), b.astype(BF16), preferred_element_type=F32)


def _dot_nt(a, b):
    return lax.dot_general(a.astype(BF16), b.astype(BF16), (((1,), (1,)), ((), ())),
                           preferred_element_type=F32)


def _dot_tn(a, b):
    return lax.dot_general(a.astype(BF16), b.astype(BF16), (((0,), (0,)), ((), ())),
                           preferred_element_type=F32)


def _silu(x):
    return x * jax.nn.sigmoid(x)


def _mm_body(x_ref, w_ref, o_ref, wb_scr):
    @pl.when(pl.program_id(1) == 0)
    def _():
        wb_scr[...] = w_ref[...].astype(BF16)

    o_ref[...] = jnp.dot(x_ref[...], wb_scr[...], preferred_element_type=F32).astype(o_ref.dtype)


def _matmul(xb, w, layer, tm, tn):
    m, k = xb.shape
    n = w.shape[2]
    assert m % tm == 0 and n % tn == 0
    return pl.pallas_call(
        _mm_body,
        grid=(n // tn, m // tm),
        in_specs=[pl.BlockSpec((tm, k), lambda j, i: (i, 0)),
                  pl.BlockSpec((None, k, tn), lambda j, i: (layer, 0, j))],
        out_specs=pl.BlockSpec((tm, tn), lambda j, i: (i, j)),
        out_shape=jax.ShapeDtypeStruct((m, n), F32),
        scratch_shapes=[pltpu.VMEM((k, tn), BF16)],
        compiler_params=_cparams(2, vmem_mib=56),
        name="in_proj",
    )(xb, w)


def _proj_ln_body(*refs, part_arity, alpha, n_prompt_tiles, split_out):
    i = pl.program_id(0)
    n_in = sum(part_arity)
    a_refs = refs[:n_in]
    w_refs = refs[n_in:n_in + len(part_arity)]
    x_ref, g_ref, b_ref, o1_ref, o2_ref = refs[n_in + len(part_arity):]
    acc = None
    pos = 0
    for arity, w_ref in zip(part_arity, w_refs):
        if arity == 1:
            a = a_refs[pos][...]
        else:
            a = jnp.where(i < n_prompt_tiles, a_refs[pos][...], a_refs[pos + 1][...])
        pos += arity
        d = jnp.dot(a, w_ref[...], preferred_element_type=F32)
        acc = d if acc is None else acc + d
    y = alpha * x_ref[...] + acc
    mu = jnp.mean(y, axis=-1, keepdims=True)
    d = y - mu
    var = jnp.mean(d * d, axis=-1, keepdims=True)
    out = d * lax.rsqrt(var + LN_EPS) * g_ref[...] + b_ref[...]
    if split_out:
        @pl.when(i < n_prompt_tiles)
        def _():
            o1_ref[...] = out

        @pl.when(i >= n_prompt_tiles)
        def _():
            o2_ref[...] = out
    else:
        o1_ref[...] = out
        o2_ref[...] = out.astype(BF16)


def _proj_ln(parts, wb, layer, x, g, b, alpha, tm, m_prompt, split_out=False):
    m, d = x.shape
    assert m % tm == 0 and m_prompt % tm == 0
    npt = m_prompt // tm
    kp = parts[0][0].shape[1]
    assert all(a.shape[1] == kp for p in parts for a in p) and wb.shape[1] == kp * len(parts)
    prompt_rows = lambda i: (jnp.minimum(i, npt - 1), 0)
    sample_rows = lambda i: (jnp.maximum(i - npt, 0), 0)

    in_specs, args = [], []
    for p in parts:
        args += list(p)
        if len(p) == 1:
            in_specs.append(pl.BlockSpec((tm, kp), lambda i: (i, 0)))
        else:
            in_specs += [pl.BlockSpec((tm, kp), prompt_rows), pl.BlockSpec((tm, kp), sample_rows)]
    for part in range(len(parts)):
        in_specs.append(pl.BlockSpec((None, kp, d), lambda i, part=part: (layer, part, 0),
                                     pipeline_mode=pl.Buffered(1)))
    in_specs += [pl.BlockSpec((tm, d), lambda i: (i, 0)),
                 pl.BlockSpec((1, d), lambda i: (0, 0)),
                 pl.BlockSpec((1, d), lambda i: (0, 0))]
    if split_out:
        out_specs = [pl.BlockSpec((tm, d), prompt_rows), pl.BlockSpec((tm, d), sample_rows)]
        out_shape = [jax.ShapeDtypeStruct((m_prompt, d), F32), jax.ShapeDtypeStruct((m - m_prompt, d), F32)]
    else:
        out_specs = [pl.BlockSpec((tm, d), lambda i: (i, 0)), pl.BlockSpec((tm, d), lambda i: (i, 0))]
        out_shape = [jax.ShapeDtypeStruct((m, d), F32), jax.ShapeDtypeStruct((m, d), BF16)]
    return pl.pallas_call(
        functools.partial(_proj_ln_body, part_arity=tuple(len(p) for p in parts), alpha=alpha,
                          n_prompt_tiles=npt, split_out=split_out),
        grid=(m // tm,),
        in_specs=in_specs,
        out_specs=out_specs,
        out_shape=out_shape,
        compiler_params=_cparams(1, vmem_mib=56),
        name="proj_ln",
    )(*args, *([wb] * len(parts)), x, g.reshape(1, d), b.reshape(1, d))


def _ffn_up_body(x_ref, wu_ref, wv_ref, cw_ref, cb_ref, b0_ref, b1_ref, *rest,
                 n_prompt_tiles, tiles_per_seq, dec_seq):
    h_ref, tail_ref, fs_ref, wub_scr, wvb_scr, carry_ref, e1_scr, e2_scr, us_scr = rest[-9:]
    i = pl.program_id(1)

    @pl.when(i == 0)
    def _():
        wub_scr[...] = wu_ref[...].astype(BF16)
        wvb_scr[...] = wv_ref[...].astype(BF16)

    tm = x_ref.shape[0]
    tn = h_ref.shape[1]
    cw = min(MXU_COLS, tn)
    rs = min(FFN_ROW_SUB, tm)
    lane_blocks = cw // LANES
    row = lax.broadcasted_iota(jnp.int32, (rs, cw), 0)
    row8 = lax.broadcasted_iota(jnp.int32, (SUBLANES, cw), 0)

    def dots(c, r):
        cols = slice(c * cw, (c + 1) * cw)
        x = x_ref[r * rs:(r + 1) * rs, :]
        return (jnp.dot(x, wub_scr[:, cols], preferred_element_type=F32),
                jnp.dot(x, wvb_scr[:, cols], preferred_element_type=F32))

    def epilogue(c, r, u, v, fix):
        cols = slice(c * cw, (c + 1) * cw)
        u1, u2 = fix(pltpu.roll(u, 1, axis=0), pltpu.roll(u, 2, axis=0))
        uc = u2 * cw_ref[0:1, cols] + cb_ref[:, cols]
        uc = uc + u1 * cw_ref[1:2, cols]
        uc = uc + u * cw_ref[2:3, cols]
        th = jnp.tanh(uc * (GELU_C0 + (GELU_C0 * GELU_C1) * (uc * uc)))
        h_ref[r * rs:(r + 1) * rs, cols] = ((uc + uc * th) * (0.5 * v)).astype(h_ref.dtype)

    def sub_tile(c, r, fix):
        u, v = dots(c, r)
        epilogue(c, r, u, v, fix)
        return u

    @pl.when(i < n_prompt_tiles)
    def _():
        first = i % tiles_per_seq == 0
        n_r = tm // rs

        def prompt_fix(prev):
            def fix(r1, r2):
                p_last = prev[SUBLANES - 1:SUBLANES]
                p_prev = prev[SUBLANES - 2:SUBLANES - 1]
                h1 = jnp.where(row8 == 0, p_last, r1[:SUBLANES])
                h2 = jnp.where(row8 == 0, p_prev, jnp.where(row8 == 1, p_last, r2[:SUBLANES]))
                return (jnp.concatenate([h1, r1[SUBLANES:]], axis=0), jnp.concatenate([h2, r2[SUBLANES:]], axis=0))
            return fix

        pending = None
        for c in range(tn // cw):
            cols = slice(c * cw, (c + 1) * cw)
            prev = jnp.where(first, 0.0, carry_ref[:, cols])
            for r in range(n_r):
                u, v = dots(c, r)
                if pending is not None:
                    epilogue(*pending)
                pending = (c, r, u, v, prompt_fix(prev))
                prev = u[rs - SUBLANES:]
            carry_ref[:, cols] = prev
            tail_ref[:, cols] = prev
        epilogue(*pending)

    @pl.when(i >= n_prompt_tiles)
    def _():
        t = row % dec_seq
        n_seq = tm // dec_seq
        e1_scr[...] = jnp.zeros_like(e1_scr)
        e2_scr[...] = jnp.zeros_like(e2_scr)
        for lb in range(tn // LANES):
            lanes = slice(lb * LANES, (lb + 1) * LANES)
            e1_scr[lb, pl.ds(0, n_seq, stride=dec_seq), :] = b1_ref[:, lanes]
            e2_scr[lb, pl.ds(0, n_seq, stride=dec_seq), :] = b0_ref[:, lanes]
            e2_scr[lb, pl.ds(1, n_seq, stride=dec_seq), :] = b1_ref[:, lanes]
        for c in range(tn // cw):
            for r in range(tm // rs):
                rows = slice(r * rs, (r + 1) * rs)
                lbs = range(c * lane_blocks, (c + 1) * lane_blocks)
                e1 = jnp.concatenate([e1_scr[lb, rows, :] for lb in lbs], axis=1)
                e2 = jnp.concatenate([e2_scr[lb, rows, :] for lb in lbs], axis=1)
                u = sub_tile(c, r, lambda r1, r2: (jnp.where(t >= 1, r1, e1), jnp.where(t >= 2, r2, e2)))
                for k, lb in enumerate(lbs):
                    us_scr[lb, rows, :] = u[:, k * LANES:(k + 1) * LANES]
        for lb in range(tn // LANES):
            lanes = slice(lb * LANES, (lb + 1) * LANES)
            fs_ref[0, :, lanes] = us_scr[lb, pl.ds(dec_seq - 2, n_seq, stride=dec_seq), :]
            fs_ref[1, :, lanes] = us_scr[lb, pl.ds(dec_seq - 1, n_seq, stride=dec_seq), :]
        tail_ref[...] = jnp.zeros_like(tail_ref)


def _ffn_up(xb, w_up, layer, cw, cb, conv_state, prev_state, *, m_prompt, seq, dec_seq, tm, tn):
    m, k = xb.shape
    d_ff = w_up.shape[2] // 2
    n_layers, bd = conv_state.shape[:2]
    m_s = m - m_prompt
    assert m_prompt % tm == 0 and m_s % tm == 0 and seq % tm == 0 and d_ff % tn == 0
    assert tm % dec_seq == 0 and dec_seq >= 2 and cw.shape[0] == 3 and conv_state.shape[2] == 2
    assert tn % min(MXU_COLS, tn) == 0 and tm % min(FFN_ROW_SUB, tm) == 0 and m_s == bd * dec_seq
    npt = m_prompt // tm
    nj = d_ff // tn
    n_seq = tm // dec_seq
    state2d = conv_state.reshape(n_layers, bd, 2 * d_ff)
    in_specs = [pl.BlockSpec((tm, k), lambda j, i: (i, 0)),
                pl.BlockSpec((None, k, tn), lambda j, i: (layer, 0, j)),
                pl.BlockSpec((None, k, tn), lambda j, i: (layer, 0, j + nj)),
                pl.BlockSpec((3, tn), lambda j, i: (0, j)),
                pl.BlockSpec((1, tn), lambda j, i: (0, j)),
                pl.BlockSpec((None, n_seq, tn), lambda j, i: (layer, jnp.maximum(i - npt, 0), j)),
                pl.BlockSpec((None, n_seq, tn), lambda j, i: (layer, jnp.maximum(i - npt, 0), j + nj))]
    args = [xb, w_up, w_up, cw, cb.reshape(1, d_ff), state2d, state2d]
    fs_shape, fs_spec, extra, extra_specs, aliases = _stacked_out(
        n_layers, layer, (2, bd, d_ff), (2, n_seq, tn), lambda j, i: (0, jnp.maximum(i - npt, 0), j),
        prev_state, len(args), out_index=2)
    n_lb = tn // LANES
    return pl.pallas_call(
        functools.partial(_ffn_up_body, n_prompt_tiles=npt, tiles_per_seq=seq // tm, dec_seq=dec_seq),
        grid=(nj, m // tm),
        in_specs=in_specs + extra_specs,
        out_specs=[pl.BlockSpec((tm, tn), lambda j, i: (i, j)),
                   pl.BlockSpec((SUBLANES, tn), lambda j, i: (i, j)),
                   fs_spec],
        out_shape=[jax.ShapeDtypeStruct((m, d_ff), BF16),
                   jax.ShapeDtypeStruct((m // tm * SUBLANES, d_ff), F32),
                   fs_shape],
        input_output_aliases=aliases,
        scratch_shapes=[pltpu.VMEM((k, tn), BF16), pltpu.VMEM((k, tn), BF16), pltpu.VMEM((SUBLANES, tn), F32),
                        pltpu.VMEM((n_lb, tm, LANES), F32), pltpu.VMEM((n_lb, tm, LANES), F32),
                        pltpu.VMEM((n_lb, tm, LANES), F32)],
        compiler_params=_cparams(2),
        name="ffn_up",
    )(*args, *extra)


def _ret_body(q_ref, k_ref, v_ref, g_ref, cos_ref, sin_ref, dm_ref, xi_ref, zeta_ref, cdec_ref, s0_ref,
              *rest, n_seq, chunk, n_chunks, scale):
    o_ref, so_ref, s_scr, q_scr, kz_scr, o_scr = rest[-6:]
    n = pl.program_id(2)

    @pl.when(n == 0)
    def _():
        s_scr[...] = s0_ref[...]

    n_heads = dm_ref.shape[0]
    dk = q_ref.shape[1] // n_heads
    half = dk // 2
    cos = cos_ref[...]
    sin = sin_ref[...]

    def rot(x):
        x1 = x[:, :half]
        x2 = x[:, half:]
        return jnp.concatenate([x1 * cos - x2 * sin, x2 * cos + x1 * sin], axis=-1)

    for hh in range(n_heads):
        cols = slice(hh * dk, (hh + 1) * dk)
        q = rot(q_ref[:, cols])
        k = rot(k_ref[:, cols]) * scale
        vb = v_ref[:, cols].astype(BF16)
        inner = _dot_nt(q, k) * dm_ref[hh]
        o_scr[hh] = _dot(inner, vb)
        q_scr[hh] = q
        kz_scr[hh] = k * zeta_ref[hh]
        cdec = cdec_ref[hh]
        for gi in range(n_seq):
            rows = pl.ds(gi * chunk, chunk)
            s = s_scr[gi, hh]
            o_scr[hh, rows, :] += _dot(q_scr[hh, rows, :], s) * xi_ref[hh, rows, :]
            s_scr[gi, hh] = s * cdec + _dot_tn(kz_scr[hh, rows, :], v_ref[rows, cols])
        o = o_scr[hh]
        gate = g_ref[:, cols]
        o = o * lax.rsqrt(jnp.mean(o * o, axis=-1, keepdims=True) + NORM_EPS) * _silu(gate)
        o_ref[:, cols] = o.astype(o_ref.dtype)

    @pl.when(n == n_chunks - 1)
    def _():
        so_ref[...] = s_scr[...]


def _ret_tables(n_heads, chunk, n_seq):
    lg = jnp.log(1.0 - 2.0 ** (-5.0 - jnp.arange(n_heads, dtype=F32)))
    idx = jnp.arange(chunk, dtype=F32)
    diff = idx[:, None] - idx[None, :]
    dmask = jnp.where(diff[None] >= 0, jnp.exp(jnp.maximum(diff, 0.0)[None] * lg[:, None, None]), 0.0)
    eye = jnp.eye(n_seq, dtype=F32)
    dm = jnp.einsum('ab,hts->hatbs', eye, dmask).reshape(n_heads, n_seq * chunk, n_seq * chunk)
    xi = jnp.tile(jnp.exp((idx[None, :] + 1.0) * lg[:, None]), (1, n_seq))[:, :, None]
    zeta = jnp.tile(jnp.exp((chunk - 1.0 - idx[None, :]) * lg[:, None]), (1, n_seq))[:, :, None]
    cdec = jnp.exp(chunk * lg)[:, None, None]
    return dm, xi, zeta, cdec


def _rope_tables(pos, half):
    inv = ROPE_BASE ** (-jnp.arange(half, dtype=F32) / half)
    ang = pos.astype(F32)[:, None] * inv[None, :]
    return jnp.cos(ang), jnp.sin(ang)


def _stacked_out(n_layers, layer, shape, block, index_map, prev, n_inputs, out_index=1):
    sds = jax.ShapeDtypeStruct((n_layers,) + tuple(shape), F32)
    spec = pl.BlockSpec((None,) + tuple(block), lambda *g: (layer,) + tuple(index_map(*g)))
    if prev is None:
        return sds, spec, [], [], {}
    return sds, spec, [prev], [pl.BlockSpec(memory_space=pl.ANY)], {n_inputs: out_index}


def _retention(p, s0, pos, *, layer, n_layers, prev_state, row0, n_batch, n_seq, chunk, n_chunks, n_heads, dk,
               heads_per_step):
    rows = n_seq * chunk
    assert row0 % rows == 0
    rb0 = row0 // rows
    dm, xi, zeta, cdec = _ret_tables(n_heads, chunk, n_seq)
    cos, sin = _rope_tables(pos, dk // 2)
    if n_seq > 1:
        assert n_chunks == 1
        cos = jnp.tile(cos, (n_seq, 1))
        sin = jnp.tile(sin, (n_seq, 1))
    hps = heads_per_step
    assert n_heads % hps == 0
    hg = n_heads // hps
    w = hps * dk

    def col(off):
        return lambda b, h, n: (rb0 + b * n_chunks + n, off * hg + h)

    tab = lambda b, h, n: (h, 0, 0)
    s0_layer = layer if s0.shape[0] > 1 else 0
    in_specs = [pl.BlockSpec((rows, w), col(0)), pl.BlockSpec((rows, w), col(1)),
                pl.BlockSpec((rows, w), col(2)), pl.BlockSpec((rows, w), col(3)),
                pl.BlockSpec((rows, dk // 2), lambda b, h, n: (n, 0)),
                pl.BlockSpec((rows, dk // 2), lambda b, h, n: (n, 0)),
                pl.BlockSpec((hps, rows, rows), tab),
                pl.BlockSpec((hps, rows, 1), tab),
                pl.BlockSpec((hps, rows, 1), tab),
                pl.BlockSpec((hps, 1, 1), tab),
                pl.BlockSpec((None, n_seq, hps, dk, dk), lambda b, h, n: (s0_layer, b, h, 0, 0))]
    args = [p, p, p, p, cos, sin, dm, xi, zeta, cdec, s0]
    so_shape, so_spec, extra, extra_specs, aliases = _stacked_out(
        n_layers, layer, (n_batch * n_seq, n_heads, dk, dk), (n_seq, hps, dk, dk),
        lambda b, h, n: (b, h, 0, 0), prev_state, len(args))
    total_rows = n_batch * n_chunks * rows
    return pl.pallas_call(
        functools.partial(_ret_body, n_seq=n_seq, chunk=chunk, n_chunks=n_chunks, scale=dk ** -0.5),
        grid=(n_batch, hg, n_chunks),
        in_specs=in_specs + extra_specs,
        out_specs=[pl.BlockSpec((rows, w), lambda b, h, n: (b * n_chunks + n, h)), so_spec],
        out_shape=[jax.ShapeDtypeStruct((total_rows, n_heads * dk), BF16), so_shape],
        input_output_aliases=aliases,
        scratch_shapes=[pltpu.VMEM((n_seq, hps, dk, dk), F32), pltpu.VMEM((hps, rows, dk), F32),
                        pltpu.VMEM((hps, rows, dk), F32), pltpu.VMEM((hps, rows, dk), F32)],
        compiler_params=_cparams(3),
        name="retention",
    )(*args, *extra)


def _rg_gates(xc, wa_ref, wx_ref, ba, bx, lam):
    nb, bw, _ = wa_ref.shape
    ra, ia = [], []
    for blk in range(nb):
        xb = xc[:, blk * bw:(blk + 1) * bw].astype(BF16)
        ra.append(jnp.dot(xb, wa_ref[blk], preferred_element_type=F32))
        ia.append(jnp.dot(xb, wx_ref[blk], preferred_element_type=F32))
    r = jax.nn.sigmoid(jnp.concatenate(ra, axis=-1) + ba)
    i = jax.nn.sigmoid(jnp.concatenate(ia, axis=-1) + bx)
    log_a = -RG_C * r * jax.nn.softplus(-lam)
    a = jnp.exp(log_a)
    mult = jnp.sqrt(-jnp.tanh(log_a) * (a * a + 1.0))
    return a, mult, i


def _scan_rows(a, b, row_in_seg, seg_len):
    s = 1
    while s < seg_len:
        a_sh = pltpu.roll(a, s, axis=0)
        b_sh = pltpu.roll(b, s, axis=0)
        m = row_in_seg >= s
        b = jnp.where(m, a * b_sh + b, b)
        a = jnp.where(m, a * a_sh, a)
        s *= 2
    return a, b


def _rg_prompt_body(xr_ref, gr_ref, cw_ref, cb_ref, wa_ref, wx_ref, ba_ref, bx_ref, lam_ref,
                    o_ref, hl_ref, xt_ref, h_scr, x_scr, *, n_tiles):
    n = pl.program_id(1)

    @pl.when(n == 0)
    def _():
        h_scr[...] = jnp.zeros_like(h_scr)
        x_scr[...] = jnp.zeros_like(x_scr)

    xr = xr_ref[...]
    tc = xr.shape[0]
    row = lax.broadcasted_iota(jnp.int32, xr.shape, 0)
    prev = x_scr[...]
    n_tap = cw_ref.shape[0]
    xc = None
    for j in range(n_tap):
        d = n_tap - 1 - j
        if d == 0:
            xd = xr
        else:
            prev_d = jnp.tile(pltpu.roll(prev, d, axis=0), (tc // SUBLANES, 1))
            xd = jnp.where(row < d, prev_d, pltpu.roll(xr, d, axis=0))
        term = xd * cw_ref[j:j + 1, :]
        xc = term + cb_ref[...] if xc is None else xc + term
    a, mult, gate_i = _rg_gates(xc, wa_ref, wx_ref, ba_ref[...], bx_ref[...], lam_ref[...])
    mult = jnp.where(row + n * tc == 0, 1.0, mult)
    bterm = xc * gate_i * mult
    a_cum, b_cum = _scan_rows(a, bterm, row, tc)
    h = a_cum * h_scr[...] + b_cum
    o_ref[...] = (h * jax.nn.gelu(gr_ref[...])).astype(o_ref.dtype)
    h_scr[...] = h[tc - 1:tc, :]
    x_scr[...] = xr[tc - SUBLANES:, :]

    @pl.when(n == n_tiles - 1)
    def _():
        hl_ref[...] = h[tc - 1:tc, :]
        xt_ref[...] = xr[tc - SUBLANES:, :]


def _rg_prompt(p, cw, cb, wa, wx, ba, bx, lam, *, n_batch, seq, col0, d_rg, total_rows):
    tc = min(RG_TIME_TILE, seq)
    assert seq % tc == 0 and col0 % d_rg == 0 and (tc & (tc - 1)) == 0
    nt = seq // tc
    cb0 = col0 // d_rg
    vec = lambda a: a.reshape(1, d_rg)
    full = lambda a: pl.BlockSpec(a.shape, lambda b, n: (0,) * a.ndim)
    args = (cw, vec(cb), wa, wx, vec(ba), vec(bx), vec(lam))
    return pl.pallas_call(
        functools.partial(_rg_prompt_body, n_tiles=nt),
        grid=(n_batch, nt),
        in_specs=[pl.BlockSpec((tc, d_rg), lambda b, n: (b * nt + n, cb0)),
                  pl.BlockSpec((tc, d_rg), lambda b, n: (b * nt + n, cb0 + 1))] + [full(a) for a in args],
        out_specs=[pl.BlockSpec((tc, d_rg), lambda b, n: (b * nt + n, 0)),
                   pl.BlockSpec((None, 1, d_rg), lambda b, n: (b, 0, 0)),
                   pl.BlockSpec((None, SUBLANES, d_rg), lambda b, n: (b, 0, 0))],
        out_shape=[jax.ShapeDtypeStruct((total_rows, d_rg), BF16),
                   jax.ShapeDtypeStruct((n_batch, 1, d_rg), F32),
                   jax.ShapeDtypeStruct((n_batch, SUBLANES, d_rg), F32)],
        scratch_shapes=[pltpu.VMEM((1, d_rg), F32), pltpu.VMEM((SUBLANES, d_rg), F32)],
        compiler_params=_cparams(2),
        name="rglru_prompt",
    )(p, p, *args)


def _rg_sample_body(xr_ref, gr_ref, e_ref, h0_ref, cw_ref, cb_ref, wa_ref, wx_ref, ba_ref, bx_ref, lam_ref,
                    o_ref, h_ref, *, dec_seq, first_pos):
    xr = xr_ref[...]
    row = lax.broadcasted_iota(jnp.int32, xr.shape, 0)
    t = row % dec_seq
    n_tap = cw_ref.shape[0]
    xc = None
    for j in range(n_tap):
        d = n_tap - 1 - j
        xd = xr if d == 0 else jnp.where(t >= d, pltpu.roll(xr, d, axis=0), e_ref[d - 1])
        term = xd * cw_ref[j:j + 1, :]
        xc = term + cb_ref[...] if xc is None else xc + term
    a, mult, gate_i = _rg_gates(xc, wa_ref, wx_ref, ba_ref[...], bx_ref[...], lam_ref[...])
    if first_pos == 0:
        mult = jnp.where(t == 0, 1.0, mult)
    bterm = xc * gate_i * mult + jnp.where(t == 0, a * h0_ref[...], 0.0)
    _, h = _scan_rows(a, bterm, t, dec_seq)
    h_ref[...] = h
    o_ref[...] = (h * jax.nn.gelu(gr_ref[...])).astype(o_ref.dtype)


def _rg_sample(p, e_prev, h0_rows, cw, cb, wa, wx, ba, bx, lam, *, row0, m_s, dec_seq, col0, d_rg):
    tr = min(128, m_s)
    assert m_s % tr == 0 and row0 % tr == 0 and tr % dec_seq == 0 and (dec_seq & (dec_seq - 1)) == 0
    rb0 = row0 // tr
    cb0 = col0 // d_rg
    vec = lambda a: a.reshape(1, d_rg)
    full = lambda a: pl.BlockSpec(a.shape, lambda i: (0,) * a.ndim)
    args = (cw, vec(cb), wa, wx, vec(ba), vec(bx), vec(lam))
    n_prev = e_prev.shape[0]
    return pl.pallas_call(
        functools.partial(_rg_sample_body, dec_seq=dec_seq, first_pos=PAST_LEN),
        grid=(m_s // tr,),
        in_specs=[pl.BlockSpec((tr, d_rg), lambda i: (rb0 + i, cb0)),
                  pl.BlockSpec((tr, d_rg), lambda i: (rb0 + i, cb0 + 1)),
                  pl.BlockSpec((n_prev, tr, d_rg), lambda i: (0, i, 0)),
                  pl.BlockSpec((tr, d_rg), lambda i: (i, 0))] + [full(a) for a in args],
        out_specs=[pl.BlockSpec((tr, d_rg), lambda i: (i, 0)), pl.BlockSpec((tr, d_rg), lambda i: (i, 0))],
        out_shape=[jax.ShapeDtypeStruct((m_s, d_rg), BF16), jax.ShapeDtypeStruct((m_s, d_rg), F32)],
        compiler_params=_cparams(1),
        name="rglru_sample",
    )(p, p, e_prev, h0_rows, *args)


def _lower_bound(logit_rows, layer):
    m = logit_rows[0]
    for z in logit_rows[1:]:
        m = jnp.maximum(m, z)
    e = [jnp.exp(z - m) for z in logit_rows]
    tot = e[0]
    for x in e[1:]:
        tot = tot + x
    if layer == 0:
        return jnp.zeros_like(tot)
    num = e[1]
    for x in e[2:layer + 1]:
        num = num + x
    return num / tot


_GLA_VREGS = GLA_BLOCK // SUBLANES


def _gla_level_masks():
    r = np.arange(GLA_BLOCK)
    t = r // SUBLANES + _GLA_VREGS * (r % SUBLANES)
    tq, tk = t[:, None], t[None, :]
    n_levels = GLA_BLOCK.bit_length()
    masks = np.zeros((n_levels, GLA_BLOCK, GLA_BLOCK), np.float32)
    masks[0] = tq == tk
    for lvl in range(1, n_levels):
        gs = 1 << lvl
        masks[lvl] = (tq > tk) & (tq // gs == tk // gs) & (tq // (gs // 2) != tk // (gs // 2))
    return masks


def _gla_prompt_body(*refs, heads, layer, n_blocks):
    q_refs, f_refs, i_refs, g_refs = (refs[p * heads:(p + 1) * heads] for p in range(4))
    lbl_ref, ng_ref, msk_ref, o_ref, so_ref, st_scr, o_scr = refs[4 * heads:]
    n = pl.program_id(2)

    @pl.when(n == 0)
    def _():
        st_scr[...] = jnp.zeros_like(st_scr)

    nv = _GLA_VREGS
    sub = lax.broadcasted_iota(jnp.int32, (SUBLANES, LANES), 0)
    zero = jnp.zeros((SUBLANES, LANES), F32)
    for hh in range(heads):
        cols = slice(hh * LANES, (hh + 1) * LANES)

        def load(ref):
            return jnp.concatenate([ref[pl.ds(j, SUBLANES, stride=nv), :] for j in range(nv)], axis=0)

        def groups(x):
            return [x[SUBLANES * j:SUBLANES * (j + 1)] for j in range(nv)]

        lb = _lower_bound([lbl_ref[r:r + 1, cols] for r in range(lbl_ref.shape[0])], layer)
        q = _silu(load(q_refs[hh]))
        fg = lb + (1.0 - lb) * jax.nn.sigmoid(load(f_refs[hh]))
        kk = 1.0 - fg
        v = load(i_refs[hh])
        vb = v.astype(BF16)
        lf = groups(jnp.log(fg))

        c = [lf[0]]
        for j in range(1, nv):
            c.append(c[-1] + lf[j])
        tot = c[nv - 1]
        x = tot
        s = 1
        while s < SUBLANES:
            x = x + jnp.where(sub >= s, pltpu.roll(x, s, axis=0), 0.0)
            s *= 2
        before = x - tot
        bj = [cj + before for cj in c]
        b = jnp.concatenate(bj, axis=0)
        b_last = bj[nv - 1][SUBLANES - 1:SUBLANES, :]

        qj = groups(q)
        kj = groups(kk)
        def keep(lvl, scores, acc):
            return jnp.where(msk_ref[lvl] > 0, scores, acc)

        a_mat = keep(0, _dot_nt(q, kk), jnp.zeros((GLA_BLOCK, GLA_BLOCK), F32))
        lvl = 1
        gs = 2
        while gs <= nv:
            hs = gs // 2
            qd, kd = [], []
            for j in range(nv):
                ref = (j // gs) * gs + hs - 1
                if j % gs >= hs:
                    qd.append(qj[j] * jnp.exp(bj[j] - bj[ref]))
                    kd.append(zero)
                else:
                    kd.append(kj[j] if j == ref else kj[j] * jnp.exp(bj[ref] - bj[j]))
                    qd.append(zero)
            a_mat = keep(lvl, _dot_nt(jnp.concatenate(qd, axis=0), jnp.concatenate(kd, axis=0)), a_mat)
            lvl += 1
            gs *= 2
        m = 2
        while m <= SUBLANES:
            src = sub - sub % m + (m // 2 - 1)
            ref = zero
            for s_src in range(m // 2 - 1, SUBLANES, m):
                row_b = jnp.broadcast_to(bj[nv - 1][s_src:s_src + 1, :], (SUBLANES, LANES))
                ref = jnp.where(src == s_src, row_b, ref)
            e = [jnp.exp(-jnp.abs(bj[j] - ref)) for j in range(nv)]
            qd = jnp.concatenate([qj[j] * e[j] for j in range(nv)], axis=0)
            kd = jnp.concatenate([kj[j] * e[j] for j in range(nv)], axis=0)
            a_mat = keep(lvl, _dot_nt(qd, kd), a_mat)
            lvl += 1
            m *= 2

        st = st_scr[hh]
        o = _dot(a_mat, vb) + _dot_nt(q * jnp.exp(b), st)
        st_new = st * jnp.exp(b_last) + _dot_tn(vb, kk * jnp.exp(b_last - b))
        st_scr[hh] = st_new

        o = o * lax.rsqrt(jnp.mean(o * o, axis=-1, keepdims=True) + NORM_EPS) * ng_ref[...]
        o = o * jax.nn.sigmoid(load(g_refs[hh]))
        for j in range(nv):
            o_scr[hh, pl.ds(j, SUBLANES, stride=nv), :] = o[SUBLANES * j:SUBLANES * (j + 1)]
        o_ref[:, cols] = o_scr[hh].astype(o_ref.dtype)

    @pl.when(n == n_blocks - 1)
    def _():
        for hh in range(heads):
            so_ref[hh] = st_scr[hh].T


def _gla_prompt(p, lb_logits, norm_g, *, layer, n_batch, seq, n_heads, total_rows):
    blk = GLA_BLOCK
    hb = GLA_HEADS_PER_STEP
    assert seq % blk == 0 and n_heads % hb == 0
    nb = seq // blk
    hg = n_heads // hb
    w = hb * LANES
    masks = jnp.asarray(_gla_level_masks())

    def head_spec(part, hh):
        return pl.BlockSpec((blk, LANES), lambda b, h, n: (b * nb + n, part * n_heads + h * hb + hh))

    return pl.pallas_call(
        functools.partial(_gla_prompt_body, heads=hb, layer=layer, n_blocks=nb),
        grid=(n_batch, hg, nb),
        in_specs=[head_spec(part, hh) for part in range(4) for hh in range(hb)] + [
                  pl.BlockSpec((lb_logits.shape[0], w), lambda b, h, n: (0, h)),
                  pl.BlockSpec((1, LANES), lambda b, h, n: (0, 0)),
                  pl.BlockSpec(masks.shape, lambda b, h, n: (0, 0, 0))],
        out_specs=[pl.BlockSpec((blk, w), lambda b, h, n: (b * nb + n, h)),
                   pl.BlockSpec((None, hb, LANES, LANES), lambda b, h, n: (b, h, 0, 0))],
        out_shape=[jax.ShapeDtypeStruct((total_rows, n_heads * LANES), BF16),
                   jax.ShapeDtypeStruct((n_batch, n_heads, LANES, LANES), F32)],
        scratch_shapes=[pltpu.VMEM((hb, LANES, LANES), F32), pltpu.VMEM((hb, blk, LANES), F32)],
        compiler_params=_cparams(3),
        name="hgrn_prompt",
    )(*([p] * (4 * hb)), lb_logits, norm_g.reshape(1, LANES), masks)


def _gla_sample_body(q_ref, f_ref, i_ref, g_ref, lbl_ref, ng_ref, s0_ref, *rest, layer):
    o_ref, so_ref, qd_scr, kd_scr, o_scr = rest[-5:]
    n_g, n_t, n_h, _ = q_ref.shape
    lb = _lower_bound([lbl_ref[r] for r in range(lbl_ref.shape[0])], layer)
    q = _silu(q_ref[...])
    fg = lb + (1.0 - lb) * jax.nn.sigmoid(f_ref[...])
    kk = 1.0 - fg
    lf = jnp.log(fg)
    v = i_ref[...]
    bt = [lf[:, 0]]
    for t in range(1, n_t):
        bt.append(bt[-1] + lf[:, t])
    b_last = bt[n_t - 1]
    for t in range(n_t):
        acc = None
        for s in range(t + 1):
            w = q[:, t] * kk[:, s]
            if s < t:
                w = w * jnp.exp(bt[t] - bt[s])
            term = jnp.sum(w, axis=-1, keepdims=True) * v[:, s]
            acc = term if acc is None else acc + term
        o_scr[:, t] = acc
        qd_scr[:, t] = q[:, t] * jnp.exp(bt[t])
        kd_scr[:, t] = kk[:, t] * jnp.exp(b_last - bt[t])
    e_last = jnp.exp(b_last)
    for gi in range(n_g):
        e_cols = e_last[gi].T
        for h in range(n_h):
            s = s0_ref[gi, h]
            o_scr[gi, :, h, :] += _dot(qd_scr[gi, :, h, :], s)
            so_ref[gi, h] = s * e_cols[:, h:h + 1] + _dot_tn(kd_scr[gi, :, h, :], i_ref[gi, :, h, :])
    o = o_scr[...]
    o = o * lax.rsqrt(jnp.mean(o * o, axis=-1, keepdims=True) + NORM_EPS) * ng_ref[...]
    o_ref[...] = (o * jax.nn.sigmoid(g_ref[...])).astype(o_ref.dtype)


def _gla_sample(p4, lb_logits, norm_g, s0, *, layer, prev_state, seq0):
    _, td, h4, _ = p4.shape
    bd = s0.shape[1]
    nh = h4 // 4
    g = min(GLA_SAMPLE_GROUP, bd)
    assert bd % g == 0 and seq0 % g == 0
    blk = (g, td, nh, LANES)
    sblk = (g, nh, LANES, LANES)
    lb3 = lb_logits.reshape(lb_logits.shape[0], nh, LANES)
    part = lambda off: pl.BlockSpec(blk, lambda i: (seq0 // g + i, 0, off, 0))
    in_specs = [part(0), part(1), part(2), part(3),
                pl.BlockSpec(lb3.shape, lambda i: (0, 0, 0)),
                pl.BlockSpec((1, LANES), lambda i: (0, 0)),
                pl.BlockSpec((None,) + sblk, lambda i: (layer, i, 0, 0, 0))]
    args = [p4, p4, p4, p4, lb3, norm_g.reshape(1, LANES), s0]
    so_shape, so_spec, extra, extra_specs, aliases = _stacked_out(
        s0.shape[0], layer, (bd, nh, LANES, LANES), sblk, lambda i: (i, 0, 0, 0), prev_state, len(args))
    return pl.pallas_call(
        functools.partial(_gla_sample_body, layer=layer),
        grid=(bd // g,),
        in_specs=in_specs + extra_specs,
        out_specs=[pl.BlockSpec(blk, lambda i: (i, 0, 0, 0)), so_spec],
        out_shape=[jax.ShapeDtypeStruct((bd, td, nh, LANES), BF16), so_shape],
        input_output_aliases=aliases,
        scratch_shapes=[pltpu.VMEM(blk, F32), pltpu.VMEM(blk, F32), pltpu.VMEM(blk, F32)],
        compiler_params=_cparams(1),
        name="hgrn_sample",
    )(*args, *extra)


def _prev_rows(buf, dec_seq):
    bd, n_prev, d = buf.shape
    outs = []
    for dd in range(1, n_prev + 1):
        rows = [buf[:, n_prev - dd + t] if t < dd else jnp.zeros((bd, d), buf.dtype) for t in range(dec_seq)]
        outs.append(jnp.stack(rows, axis=1).reshape(bd * dec_seq, d))
    return outs


def _first_rows(vals, dec_seq):
    bd, d = vals.shape
    z = jnp.zeros((bd, dec_seq - 1, d), vals.dtype)
    return jnp.concatenate([vals[:, None, :], z], axis=1).reshape(bd * dec_seq, d)


def kernel(x_prompt, x_sample, state_ret, state_rglru_h, state_rglru_conv, state_hgrn, state_ffn_conv,
           ev_w_in, ev_w_out, ev_rg_conv_w, ev_rg_conv_b, ev_rg_wa, ev_rg_ba, ev_rg_wx, ev_rg_bx, ev_rg_lambda,
           od_w_in, od_w_out, od_norm_g, od_lb_logits, ln_g, ln_b, ffn_w_up, ffn_conv_w, ffn_conv_b, ffn_w_down):
    bp, tp, d_model = x_prompt.shape
    bd, td, _ = x_sample.shape
    depth = ln_g.shape[0]
    m_p, m_s = bp * tp, bd * td
    m = m_p + m_s
    alpha = (2.0 * depth) ** 0.25
    h_ret, dk_ret = state_ret.shape[2], state_ret.shape[3]
    d_ret = h_ret * dk_ret
    d_rg = state_rglru_h.shape[-1]
    h_hg = state_hgrn.shape[2]
    d_ff = ffn_conv_b.shape[-1]
    assert state_hgrn.shape[3] == LANES and state_hgrn.shape[4] == LANES and d_ret == d_rg
    assert ffn_conv_w.shape[1] == 3 and td >= 3

    x = jnp.concatenate([x_prompt.reshape(m_p, d_model), x_sample.reshape(m_s, d_model)], axis=0)
    xb = x.astype(BF16)
    pos_p = jnp.arange(tp, dtype=jnp.int32)
    pos_s = PAST_LEN + jnp.arange(td, dtype=jnp.int32)
    zero_ret = jnp.zeros((1, bp) + state_ret.shape[2:], F32)
    n_even = state_ret.shape[0]
    w_out_ev = ev_w_out.astype(BF16)
    w_out_od = od_w_out.astype(BF16)
    w_down = ffn_w_down.astype(BF16)

    ret_p = ret_s = hg_s = ff_s = None
    n_h_p, n_h_s, n_cv_p, n_cv_s, n_hg_p, n_ff_p = [], [], [], [], [], []
    tm_ff = min(ROW_TILE, m_s)
    tm_ln = min(LN_ROW_TILE, m_s)
    tm_in = min(IN_PROJ_ROW_TILE, m_s)
    for l in range(depth):
        if l % 2 == 0:
            e = l // 2
            p = _matmul(xb, ev_w_in, e, tm_in, IN_PROJ_COLS)
            chunk = RET_CHUNK if tp % RET_CHUNK == 0 else tp
            o_ret, ret_p = _retention(p, zero_ret, pos_p, layer=e, n_layers=n_even, prev_state=ret_p, row0=0,
                                      n_batch=bp, n_seq=1, chunk=chunk, n_chunks=tp // chunk, n_heads=h_ret,
                                      dk=dk_ret, heads_per_step=h_ret)
            g_ret = min(RET_SAMPLE_GROUP, bd)
            o_ret_s, ret_s = _retention(p, state_ret, pos_s, layer=e, n_layers=n_even, prev_state=ret_s,
                                        row0=m_p, n_batch=bd // g_ret, n_seq=g_ret, chunk=td, n_chunks=1,
                                        n_heads=h_ret, dk=dk_ret, heads_per_step=1)
            wa = ev_rg_wa[e].astype(BF16)
            wx = ev_rg_wx[e].astype(BF16)
            rg_args = (ev_rg_conv_w[e], ev_rg_conv_b[e], wa, wx, ev_rg_ba[e], ev_rg_bx[e], ev_rg_lambda[e])
            o_rg, hl_p, xt_p = _rg_prompt(p, *rg_args, n_batch=bp, seq=tp, col0=4 * d_ret, d_rg=d_rg,
                                          total_rows=m_p)
            e_prev = jnp.stack(_prev_rows(state_rglru_conv[e], td))
            o_rg_s, h_s = _rg_sample(p, e_prev, _first_rows(state_rglru_h[e], td), *rg_args,
                                     row0=m_p, m_s=m_s, dec_seq=td, col0=4 * d_ret, d_rg=d_rg)
            n_conv = state_rglru_conv.shape[2]
            xr_s = p[m_p:, 4 * d_ret:4 * d_ret + d_rg].reshape(bd, td, d_rg)
            n_h_p.append(hl_p[:, 0])
            n_h_s.append(h_s.reshape(bd, td, d_rg)[:, td - 1])
            n_cv_p.append(xt_p[:, SUBLANES - n_conv:])
            n_cv_s.append(xr_s[:, td - n_conv:])
            x, xb = _proj_ln([(o_ret, o_ret_s), (o_rg, o_rg_s)], w_out_ev, e, x, ln_g[l, 0], ln_b[l, 0], alpha,
                             tm_ln, m_p)
        else:
            o = l // 2
            p = _matmul(xb, od_w_in, o, tm_in, IN_PROJ_COLS)
            o_hg, g_p = _gla_prompt(p, od_lb_logits, od_norm_g[o], layer=o, n_batch=bp, seq=tp,
                                    n_heads=h_hg, total_rows=m_p)
            p4 = p[m_p:].reshape(bd, td, 4 * h_hg, LANES)
            o_hg_s, hg_s = _gla_sample(p4, od_lb_logits, od_norm_g[o], state_hgrn, layer=o, prev_state=hg_s,
                                       seq0=0)
            n_hg_p.append(g_p)
            x, xb = _proj_ln([(o_hg, o_hg_s.reshape(m_s, h_hg * LANES))], w_out_od, o, x, ln_g[l, 0], ln_b[l, 0],
                             alpha, tm_ln, m_p)
        h, tails, ff_s = _ffn_up(xb, ffn_w_up, l, ffn_conv_w[l], ffn_conv_b[l], state_ffn_conv, ff_s,
                                 m_prompt=m_p, seq=tp, dec_seq=td, tm=tm_ff, tn=512)
        tiles_per_seq = tp // tm_ff
        tails = tails.reshape(m // tm_ff, SUBLANES, d_ff)[:bp * tiles_per_seq]
        tails = tails.reshape(bp, tiles_per_seq, SUBLANES, d_ff)
        n_ff_p.append(tails[:, tiles_per_seq - 1, SUBLANES - 2:])
        last = l == depth - 1
        x, xb = _proj_ln([(h,)], w_down, l, x, ln_g[l, 1], ln_b[l, 1], alpha, tm_ln, m_p, split_out=last)

    y_prompt = x.reshape(bp, tp, d_model)
    y_sample = xb.reshape(bd, td, d_model)
    return (y_prompt, y_sample, ret_p, ret_s, jnp.stack(n_h_p), jnp.stack(n_h_s),
            jnp.stack(n_cv_p), jnp.stack(n_cv_s), jnp.stack(n_hg_p), hg_s,
            jnp.stack(n_ff_p), jnp.swapaxes(ff_s, 1, 2))
```

```python
import functools

import numpy as np
import jax
import jax.numpy as jnp
from jax import lax
from jax.experimental import pallas as pl
from jax.experimental.pallas import tpu as pltpu

F32 = jnp.float32
BF16 = jnp.bfloat16

LN_EPS = 1e-5
NORM_EPS = 1e-6
ROPE_BASE = 10000.0
RG_C = 8.0
TINY = 1e-37
GELU_C0 = 0.7978845608028654
GELU_C1 = 0.044715
PAST_LEN = 16384
LANES = 128
SUBLANES = 8
MXU_COLS = 256
ROW_TILE = 512
FFN_ROW_SUB = 256
LN_ROW_TILE = 256
IN_PROJ_ROW_TILE = 256
IN_PROJ_COLS = 2048
RET_CHUNK = 128
RET_SAMPLE_GROUP = 8
RG_TIME_TILE = 256
GLA_BLOCK = 128
GLA_HEADS_PER_STEP = 8
GLA_SAMPLE_GROUP = 4
MIB = 1024 * 1024


def _cparams(n_axes, vmem_mib=48):
    return pltpu.CompilerParams(dimension_semantics=("arbitrary",) * n_axes,
                                vmem_limit_bytes=vmem_mib * MIB)


def _dot(a, b):
    return jnp.dot(a.astype(BF16), b.astype(BF16), preferred_element_type=F32)


def _dot_nt(a, b):
    return lax.dot_general(a.astype(BF16), b.astype(BF16), (((1,), (1,)), ((), ())),
                           preferred_element_type=F32)


def _dot_tn(a, b):
    return lax.dot_general(a.astype(BF16), b.astype(BF16), (((0,), (0,)), ((), ())),
                           preferred_element_type=F32)


def _silu(x):
    return x * jax.nn.sigmoid(x)


def _mm_body(x_ref, w_ref, o_ref, wb_scr):
    @pl.when(pl.program_id(1) == 0)
    def _():
        wb_scr[...] = w_ref[...].astype(BF16)

    o_ref[...] = jnp.dot(x_ref[...], wb_scr[...], preferred_element_type=F32).astype(o_ref.dtype)


def _matmul(xb, w, layer, tm, tn):
    m, k = xb.shape
    n = w.shape[2]
    assert m % tm == 0 and n % tn == 0
    return pl.pallas_call(
        _mm_body,
        grid=(n // tn, m // tm),
        in_specs=[pl.BlockSpec((tm, k), lambda j, i: (i, 0)),
                  pl.BlockSpec((None, k, tn), lambda j, i: (layer, 0, j))],
        out_specs=pl.BlockSpec((tm, tn), lambda j, i: (i, j)),
        out_shape=jax.ShapeDtypeStruct((m, n), F32),
        scratch_shapes=[pltpu.VMEM((k, tn), BF16)],
        compiler_params=_cparams(2, vmem_mib=56),
        name="in_proj",
    )(xb, w)


def _proj_ln_body(*refs, part_arity, res_arity, alpha, n_prompt_tiles, split_out):
    i = pl.program_id(0)
    n_in = sum(part_arity)
    a_refs = refs[:n_in]
    w_refs = refs[n_in:n_in + len(part_arity)]
    x_refs = refs[n_in + len(part_arity):][:res_arity]
    g_ref, b_ref, o1_ref, o2_ref = refs[n_in + len(part_arity) + res_arity:]

    def rows_of(group):
        if len(group) == 1:
            return group[0][...]
        return jnp.where(i < n_prompt_tiles, group[0][...], group[1][...])

    acc = None
    pos = 0
    for arity, w_ref in zip(part_arity, w_refs):
        d = jnp.dot(rows_of(a_refs[pos:pos + arity]), w_ref[...], preferred_element_type=F32)
        pos += arity
        acc = d if acc is None else acc + d
    y = alpha * rows_of(x_refs) + acc
    mu = jnp.mean(y, axis=-1, keepdims=True)
    d = y - mu
    var = jnp.mean(d * d, axis=-1, keepdims=True)
    out = d * lax.rsqrt(var + LN_EPS) * g_ref[...] + b_ref[...]
    if split_out:
        @pl.when(i < n_prompt_tiles)
        def _():
            o1_ref[...] = out

        @pl.when(i >= n_prompt_tiles)
        def _():
            o2_ref[...] = out
    else:
        o1_ref[...] = out
        o2_ref[...] = out.astype(BF16)


def _proj_ln(parts, wb, layer, x, g, b, alpha, tm, m_prompt, split_out=False):
    m = sum(a.shape[0] for a in x)
    d = x[0].shape[1]
    assert m % tm == 0 and m_prompt % tm == 0
    npt = m_prompt // tm
    kp = parts[0][0].shape[1]
    assert all(a.shape[1] == kp for p in parts for a in p) and wb.shape[1] == kp * len(parts)
    prompt_rows = lambda i: (jnp.minimum(i, npt - 1), 0)
    sample_rows = lambda i: (jnp.maximum(i - npt, 0), 0)

    def row_specs(group, width):
        if len(group) == 1:
            return [pl.BlockSpec((tm, width), lambda i: (i, 0))]
        return [pl.BlockSpec((tm, width), prompt_rows), pl.BlockSpec((tm, width), sample_rows)]

    in_specs, args = [], []
    for p in parts:
        args += list(p)
        in_specs += row_specs(p, kp)
    for part in range(len(parts)):
        in_specs.append(pl.BlockSpec((None, kp, d), lambda i, part=part: (layer, part, 0),
                                     pipeline_mode=pl.Buffered(1)))
    in_specs += row_specs(x, d)
    in_specs += [pl.BlockSpec((1, d), lambda i: (0, 0)),
                 pl.BlockSpec((1, d), lambda i: (0, 0))]
    if split_out:
        out_specs = [pl.BlockSpec((tm, d), prompt_rows), pl.BlockSpec((tm, d), sample_rows)]
        out_shape = [jax.ShapeDtypeStruct((m_prompt, d), F32), jax.ShapeDtypeStruct((m - m_prompt, d), F32)]
    else:
        out_specs = [pl.BlockSpec((tm, d), lambda i: (i, 0)), pl.BlockSpec((tm, d), lambda i: (i, 0))]
        out_shape = [jax.ShapeDtypeStruct((m, d), F32), jax.ShapeDtypeStruct((m, d), BF16)]
    return pl.pallas_call(
        functools.partial(_proj_ln_body, part_arity=tuple(len(p) for p in parts), res_arity=len(x), alpha=alpha,
                          n_prompt_tiles=npt, split_out=split_out),
        grid=(m // tm,),
        in_specs=in_specs,
        out_specs=out_specs,
        out_shape=out_shape,
        compiler_params=_cparams(1, vmem_mib=56),
        name="proj_ln",
    )(*args, *([wb] * len(parts)), *x, g.reshape(1, d), b.reshape(1, d))


def _ffn_up_body(x_ref, wu_ref, wv_ref, cw_ref, cb_ref, buf_ref, *rest,
                 n_prompt_tiles, tiles_per_seq, dec_seq):
    h_ref, tail_ref, fs_ref, wub_scr, wvb_scr, carry_ref, e1_scr, e2_scr, us_scr, uh_scr = rest[-10:]
    i = pl.program_id(1)

    @pl.when(i == 0)
    def _():
        wub_scr[...] = wu_ref[...].astype(BF16)
        wvb_scr[...] = wv_ref[...].astype(BF16)

    tm = x_ref.shape[0]
    tn = h_ref.shape[1]
    cw = min(MXU_COLS, tn)
    rs = min(FFN_ROW_SUB, tm)
    lane_blocks = cw // LANES
    row = lax.broadcasted_iota(jnp.int32, (rs, cw), 0)

    def dots(c, r):
        cols = slice(c * cw, (c + 1) * cw)
        x = x_ref[r * rs:(r + 1) * rs, :]
        return (jnp.dot(x, wub_scr[:, cols], preferred_element_type=F32),
                jnp.dot(x, wvb_scr[:, cols], preferred_element_type=F32))

    def epilogue(c, r, u, u1, u2, v):
        cols = slice(c * cw, (c + 1) * cw)
        uc = u2 * cw_ref[0:1, cols] + cb_ref[:, cols]
        uc = uc + u1 * cw_ref[1:2, cols]
        uc = uc + u * cw_ref[2:3, cols]
        th = jnp.tanh(uc * (GELU_C0 + (GELU_C0 * GELU_C1) * (uc * uc)))
        h_ref[r * rs:(r + 1) * rs, cols] = ((uc + uc * th) * (0.5 * v)).astype(h_ref.dtype)

    @pl.when(i < n_prompt_tiles)
    def _():
        first = i % tiles_per_seq == 0
        n_r = tm // rs
        for c in range(tn // cw):
            cols = slice(c * cw, (c + 1) * cw)
            prev = jnp.where(first, 0.0, carry_ref[:, cols])
            for r in range(n_r):
                slot = (c * n_r + r) % 2
                u, v = dots(c, r)
                uh_scr[slot, 0:SUBLANES, :] = prev
                uh_scr[slot, SUBLANES:, :] = u
                u1 = uh_scr[slot, SUBLANES - 1:SUBLANES - 1 + rs, :]
                u2 = uh_scr[slot, SUBLANES - 2:SUBLANES - 2 + rs, :]
                epilogue(c, r, u, u1, u2, v)
                prev = u[rs - SUBLANES:]
            carry_ref[:, cols] = prev
            tail_ref[:, cols] = prev

    @pl.when(i >= n_prompt_tiles)
    def _():
        t = row % dec_seq
        n_seq = tm // dec_seq
        e1_scr[...] = jnp.zeros_like(e1_scr)
        e2_scr[...] = jnp.zeros_like(e2_scr)
        for lb in range(tn // LANES):
            lanes = slice(lb * LANES, (lb + 1) * LANES)
            e1_scr[lb, pl.ds(0, n_seq, stride=dec_seq), :] = buf_ref[:, 1, lanes]
            e2_scr[lb, pl.ds(0, n_seq, stride=dec_seq), :] = buf_ref[:, 0, lanes]
            e2_scr[lb, pl.ds(1, n_seq, stride=dec_seq), :] = buf_ref[:, 1, lanes]
        for c in range(tn // cw):
            for r in range(tm // rs):
                rows = slice(r * rs, (r + 1) * rs)
                lbs = range(c * lane_blocks, (c + 1) * lane_blocks)
                e1 = jnp.concatenate([e1_scr[lb, rows, :] for lb in lbs], axis=1)
                e2 = jnp.concatenate([e2_scr[lb, rows, :] for lb in lbs], axis=1)
                u, v = dots(c, r)
                epilogue(c, r, u, jnp.where(t >= 1, pltpu.roll(u, 1, axis=0), e1),
                         jnp.where(t >= 2, pltpu.roll(u, 2, axis=0), e2), v)
                for k, lb in enumerate(lbs):
                    us_scr[lb, rows, :] = u[:, k * LANES:(k + 1) * LANES]
        for lb in range(tn // LANES):
            lanes = slice(lb * LANES, (lb + 1) * LANES)
            fs_ref[:, 0, lanes] = us_scr[lb, pl.ds(dec_seq - 2, n_seq, stride=dec_seq), :]
            fs_ref[:, 1, lanes] = us_scr[lb, pl.ds(dec_seq - 1, n_seq, stride=dec_seq), :]
        tail_ref[...] = jnp.zeros_like(tail_ref)


def _ffn_up(xb, w_up, layer, cw, cb, conv_state, prev_state, *, m_prompt, seq, dec_seq, tm, tn):
    m, k = xb.shape
    d_ff = w_up.shape[2] // 2
    n_layers, bd = conv_state.shape[:2]
    m_s = m - m_prompt
    assert m_prompt % tm == 0 and m_s % tm == 0 and seq % tm == 0 and d_ff % tn == 0
    assert tm % dec_seq == 0 and dec_seq >= 2 and cw.shape[0] == 3 and conv_state.shape[2] == 2
    assert tn % min(MXU_COLS, tn) == 0 and tm % min(FFN_ROW_SUB, tm) == 0 and m_s == bd * dec_seq
    npt = m_prompt // tm
    nj = d_ff // tn
    n_seq = tm // dec_seq
    in_specs = [pl.BlockSpec((tm, k), lambda j, i: (i, 0)),
                pl.BlockSpec((None, k, tn), lambda j, i: (layer, 0, j)),
                pl.BlockSpec((None, k, tn), lambda j, i: (layer, 0, j + nj)),
                pl.BlockSpec((3, tn), lambda j, i: (0, j)),
                pl.BlockSpec((1, tn), lambda j, i: (0, j)),
                pl.BlockSpec((None, n_seq, 2, tn), lambda j, i: (layer, jnp.maximum(i - npt, 0), 0, j))]
    args = [xb, w_up, w_up, cw, cb.reshape(1, d_ff), conv_state]
    fs_shape, fs_spec, extra, extra_specs, aliases = _stacked_out(
        n_layers, layer, (bd, 2, d_ff), (n_seq, 2, tn), lambda j, i: (jnp.maximum(i - npt, 0), 0, j),
        prev_state, len(args), out_index=2)
    n_lb = tn // LANES
    return pl.pallas_call(
        functools.partial(_ffn_up_body, n_prompt_tiles=npt, tiles_per_seq=seq // tm, dec_seq=dec_seq),
        grid=(nj, m // tm),
        in_specs=in_specs + extra_specs,
        out_specs=[pl.BlockSpec((tm, tn), lambda j, i: (i, j)),
                   pl.BlockSpec((SUBLANES, tn), lambda j, i: (i, j)),
                   fs_spec],
        out_shape=[jax.ShapeDtypeStruct((m, d_ff), BF16),
                   jax.ShapeDtypeStruct((m // tm * SUBLANES, d_ff), F32),
                   fs_shape],
        input_output_aliases=aliases,
        scratch_shapes=[pltpu.VMEM((k, tn), BF16), pltpu.VMEM((k, tn), BF16), pltpu.VMEM((SUBLANES, tn), F32),
                        pltpu.VMEM((n_lb, tm, LANES), F32), pltpu.VMEM((n_lb, tm, LANES), F32),
                        pltpu.VMEM((n_lb, tm, LANES), F32),
                        pltpu.VMEM((2, min(FFN_ROW_SUB, tm) + SUBLANES, min(MXU_COLS, tn)), F32)],
        compiler_params=_cparams(2),
        name="ffn_up",
    )(*args, *extra)


def _ret_body(q_ref, k_ref, v_ref, g_ref, cos_ref, sin_ref, dm_ref, xi_ref, zeta_ref, cdec_ref, s0_ref,
              *rest, n_seq, chunk, n_chunks, scale):
    o_ref, so_ref, s_scr, q_scr, kz_scr, o_scr = rest[-6:]
    n = pl.program_id(2)

    @pl.when(n == 0)
    def _():
        s_scr[...] = s0_ref[...]

    n_heads = dm_ref.shape[0]
    dk = q_ref.shape[1] // n_heads
    half = dk // 2
    cos = cos_ref[...]
    sin = sin_ref[...]

    def rot(x):
        x1 = x[:, :half]
        x2 = x[:, half:]
        return jnp.concatenate([x1 * cos - x2 * sin, x2 * cos + x1 * sin], axis=-1)

    for hh in range(n_heads):
        cols = slice(hh * dk, (hh + 1) * dk)
        q = rot(q_ref[:, cols])
        k = rot(k_ref[:, cols]) * scale
        vb = v_ref[:, cols].astype(BF16)
        inner = _dot_nt(q, k) * dm_ref[hh]
        o_scr[hh] = _dot(inner, vb)
        q_scr[hh] = q
        kz_scr[hh] = k * zeta_ref[hh]
        cdec = cdec_ref[hh]
        for gi in range(n_seq):
            rows = pl.ds(gi * chunk, chunk)
            s = s_scr[gi, hh]
            o_scr[hh, rows, :] += _dot(q_scr[hh, rows, :], s) * xi_ref[hh, rows, :]
            s_scr[gi, hh] = s * cdec + _dot_tn(kz_scr[hh, rows, :], v_ref[rows, cols])
        o = o_scr[hh]
        gate = g_ref[:, cols]
        o = o * lax.rsqrt(jnp.mean(o * o, axis=-1, keepdims=True) + NORM_EPS) * _silu(gate)
        o_ref[:, cols] = o.astype(o_ref.dtype)

    @pl.when(n == n_chunks - 1)
    def _():
        so_ref[...] = s_scr[...]


def _ret_tables(n_heads, chunk, n_seq):
    lg = jnp.log(1.0 - 2.0 ** (-5.0 - jnp.arange(n_heads, dtype=F32)))
    idx = jnp.arange(chunk, dtype=F32)
    diff = idx[:, None] - idx[None, :]
    dmask = jnp.where(diff[None] >= 0, jnp.exp(jnp.maximum(diff, 0.0)[None] * lg[:, None, None]), 0.0)
    eye = jnp.eye(n_seq, dtype=F32)
    dm = jnp.einsum('ab,hts->hatbs', eye, dmask).reshape(n_heads, n_seq * chunk, n_seq * chunk)
    xi = jnp.tile(jnp.exp((idx[None, :] + 1.0) * lg[:, None]), (1, n_seq))[:, :, None]
    zeta = jnp.tile(jnp.exp((chunk - 1.0 - idx[None, :]) * lg[:, None]), (1, n_seq))[:, :, None]
    cdec = jnp.exp(chunk * lg)[:, None, None]
    return dm, xi, zeta, cdec


def _rope_tables(pos, half):
    inv = ROPE_BASE ** (-jnp.arange(half, dtype=F32) / half)
    ang = pos.astype(F32)[:, None] * inv[None, :]
    return jnp.cos(ang), jnp.sin(ang)


def _stacked_out(n_layers, layer, shape, block, index_map, prev, n_inputs, out_index=1):
    sds = jax.ShapeDtypeStruct((n_layers,) + tuple(shape), F32)
    spec = pl.BlockSpec((None,) + tuple(block), lambda *g: (layer,) + tuple(index_map(*g)))
    if prev is None:
        return sds, spec, [], [], {}
    return sds, spec, [prev], [pl.BlockSpec(memory_space=pl.ANY)], {n_inputs: out_index}


def _retention(p, s0, pos, *, layer, n_layers, prev_state, row0, n_batch, n_seq, chunk, n_chunks, n_heads, dk,
               heads_per_step):
    rows = n_seq * chunk
    assert row0 % rows == 0
    rb0 = row0 // rows
    dm, xi, zeta, cdec = _ret_tables(n_heads, chunk, n_seq)
    cos, sin = _rope_tables(pos, dk // 2)
    if n_seq > 1:
        assert n_chunks == 1
        cos = jnp.tile(cos, (n_seq, 1))
        sin = jnp.tile(sin, (n_seq, 1))
    hps = heads_per_step
    assert n_heads % hps == 0
    hg = n_heads // hps
    w = hps * dk

    def col(off):
        return lambda b, h, n: (rb0 + b * n_chunks + n, off * hg + h)

    tab = lambda b, h, n: (h, 0, 0)
    s0_layer = layer if s0.shape[0] > 1 else 0
    in_specs = [pl.BlockSpec((rows, w), col(0)), pl.BlockSpec((rows, w), col(1)),
                pl.BlockSpec((rows, w), col(2)), pl.BlockSpec((rows, w), col(3)),
                pl.BlockSpec((rows, dk // 2), lambda b, h, n: (n, 0)),
                pl.BlockSpec((rows, dk // 2), lambda b, h, n: (n, 0)),
                pl.BlockSpec((hps, rows, rows), tab),
                pl.BlockSpec((hps, rows, 1), tab),
                pl.BlockSpec((hps, rows, 1), tab),
                pl.BlockSpec((hps, 1, 1), tab),
                pl.BlockSpec((None, n_seq, hps, dk, dk), lambda b, h, n: (s0_layer, b, h, 0, 0))]
    args = [p, p, p, p, cos, sin, dm, xi, zeta, cdec, s0]
    so_shape, so_spec, extra, extra_specs, aliases = _stacked_out(
        n_layers, layer, (n_batch * n_seq, n_heads, dk, dk), (n_seq, hps, dk, dk),
        lambda b, h, n: (b, h, 0, 0), prev_state, len(args))
    total_rows = n_batch * n_chunks * rows
    return pl.pallas_call(
        functools.partial(_ret_body, n_seq=n_seq, chunk=chunk, n_chunks=n_chunks, scale=dk ** -0.5),
        grid=(n_batch, hg, n_chunks),
        in_specs=in_specs + extra_specs,
        out_specs=[pl.BlockSpec((rows, w), lambda b, h, n: (b * n_chunks + n, h)), so_spec],
        out_shape=[jax.ShapeDtypeStruct((total_rows, n_heads * dk), BF16), so_shape],
        input_output_aliases=aliases,
        scratch_shapes=[pltpu.VMEM((n_seq, hps, dk, dk), F32), pltpu.VMEM((hps, rows, dk), F32),
                        pltpu.VMEM((hps, rows, dk), F32), pltpu.VMEM((hps, rows, dk), F32)],
        compiler_params=_cparams(3),
        name="retention",
    )(*args, *extra)


def _rg_gates(xc, wa_ref, wx_ref, ba, bx, lam):
    nb, bw, _ = wa_ref.shape
    ra, ia = [], []
    for blk in range(nb):
        xb = xc[:, blk * bw:(blk + 1) * bw].astype(BF16)
        ra.append(jnp.dot(xb, wa_ref[blk], preferred_element_type=F32))
        ia.append(jnp.dot(xb, wx_ref[blk], preferred_element_type=F32))
    r = jax.nn.sigmoid(jnp.concatenate(ra, axis=-1) + ba)
    i = jax.nn.sigmoid(jnp.concatenate(ia, axis=-1) + bx)
    log_a = -RG_C * r * jax.nn.softplus(-lam)
    a = jnp.exp(log_a)
    y = -jnp.tanh(log_a) * (a * a + 1.0)
    mult = y * lax.rsqrt(jnp.maximum(y, TINY))
    return a, mult, i


def _scan_rows(a, b, row_in_seg, seg_len):
    s = 1
    while s < seg_len:
        a_sh = pltpu.roll(a, s, axis=0)
        b_sh = pltpu.roll(b, s, axis=0)
        m = row_in_seg >= s
        b = jnp.where(m, a * b_sh + b, b)
        a = jnp.where(m, a * a_sh, a)
        s *= 2
    return a, b


def _rg_prompt_body(xr_ref, gr_ref, cw_ref, cb_ref, wa_ref, wx_ref, ba_ref, bx_ref, lam_ref,
                    o_ref, hl_ref, xt_ref, h_scr, x_scr, *, n_tiles):
    n = pl.program_id(1)

    @pl.when(n == 0)
    def _():
        h_scr[...] = jnp.zeros_like(h_scr)
        x_scr[...] = jnp.zeros_like(x_scr)

    xr = xr_ref[...]
    tc = xr.shape[0]
    row = lax.broadcasted_iota(jnp.int32, xr.shape, 0)
    row8 = lax.broadcasted_iota(jnp.int32, (SUBLANES, xr.shape[1]), 0)
    prev = x_scr[...]
    n_tap = cw_ref.shape[0]
    xc = None
    for j in range(n_tap):
        d = n_tap - 1 - j
        if d == 0:
            xd = xr
        else:
            rolled = pltpu.roll(xr, d, axis=0)
            head = jnp.where(row8 < d, pltpu.roll(prev, d, axis=0), rolled[:SUBLANES])
            xd = jnp.concatenate([head, rolled[SUBLANES:]], axis=0)
        term = xd * cw_ref[j:j + 1, :]
        xc = term + cb_ref[...] if xc is None else xc + term
    a, mult, gate_i = _rg_gates(xc, wa_ref, wx_ref, ba_ref[...], bx_ref[...], lam_ref[...])
    mult = jnp.where(row + n * tc == 0, 1.0, mult)
    bterm = xc * gate_i * mult
    a_grp, b_grp = _scan_rows(a, bterm, row % SUBLANES, SUBLANES)
    h_prev = h_scr[...]
    groups = []
    for gi in range(tc // SUBLANES):
        rows = slice(gi * SUBLANES, (gi + 1) * SUBLANES)
        h_g = a_grp[rows] * h_prev + b_grp[rows]
        groups.append(h_g)
        h_prev = h_g[SUBLANES - 1:SUBLANES]
    h = jnp.concatenate(groups, axis=0)
    o_ref[...] = (h * jax.nn.gelu(gr_ref[...])).astype(o_ref.dtype)
    h_scr[...] = h[tc - 1:tc, :]
    x_scr[...] = xr[tc - SUBLANES:, :]

    @pl.when(n == n_tiles - 1)
    def _():
        hl_ref[...] = h[tc - 1:tc, :]
        xt_ref[...] = xr[tc - SUBLANES:, :]


def _rg_prompt(p, cw, cb, wa, wx, ba, bx, lam, *, n_batch, seq, col0, d_rg, total_rows):
    tc = min(RG_TIME_TILE, seq)
    assert seq % tc == 0 and col0 % d_rg == 0 and (tc & (tc - 1)) == 0
    nt = seq // tc
    cb0 = col0 // d_rg
    vec = lambda a: a.reshape(1, d_rg)
    full = lambda a: pl.BlockSpec(a.shape, lambda b, n: (0,) * a.ndim)
    args = (cw, vec(cb), wa, wx, vec(ba), vec(bx), vec(lam))
    return pl.pallas_call(
        functools.partial(_rg_prompt_body, n_tiles=nt),
        grid=(n_batch, nt),
        in_specs=[pl.BlockSpec((tc, d_rg), lambda b, n: (b * nt + n, cb0)),
                  pl.BlockSpec((tc, d_rg), lambda b, n: (b * nt + n, cb0 + 1))] + [full(a) for a in args],
        out_specs=[pl.BlockSpec((tc, d_rg), lambda b, n: (b * nt + n, 0)),
                   pl.BlockSpec((None, 1, d_rg), lambda b, n: (b, 0, 0)),
                   pl.BlockSpec((None, SUBLANES, d_rg), lambda b, n: (b, 0, 0))],
        out_shape=[jax.ShapeDtypeStruct((total_rows, d_rg), BF16),
                   jax.ShapeDtypeStruct((n_batch, 1, d_rg), F32),
                   jax.ShapeDtypeStruct((n_batch, SUBLANES, d_rg), F32)],
        scratch_shapes=[pltpu.VMEM((1, d_rg), F32), pltpu.VMEM((SUBLANES, d_rg), F32)],
        compiler_params=_cparams(2),
        name="rglru_prompt",
    )(p, p, *args)


def _rg_sample_body(xr_ref, gr_ref, e_ref, h0_ref, cw_ref, cb_ref, wa_ref, wx_ref, ba_ref, bx_ref, lam_ref,
                    o_ref, h_ref, *, dec_seq, first_pos):
    xr = xr_ref[...]
    row = lax.broadcasted_iota(jnp.int32, xr.shape, 0)
    t = row % dec_seq
    n_tap = cw_ref.shape[0]
    xc = None
    for j in range(n_tap):
        d = n_tap - 1 - j
        xd = xr if d == 0 else jnp.where(t >= d, pltpu.roll(xr, d, axis=0), e_ref[d - 1])
        term = xd * cw_ref[j:j + 1, :]
        xc = term + cb_ref[...] if xc is None else xc + term
    a, mult, gate_i = _rg_gates(xc, wa_ref, wx_ref, ba_ref[...], bx_ref[...], lam_ref[...])
    if first_pos == 0:
        mult = jnp.where(t == 0, 1.0, mult)
    bterm = xc * gate_i * mult + jnp.where(t == 0, a * h0_ref[...], 0.0)
    _, h = _scan_rows(a, bterm, t, dec_seq)
    h_ref[...] = h
    o_ref[...] = (h * jax.nn.gelu(gr_ref[...])).astype(o_ref.dtype)


def _rg_sample(p, e_prev, h0_rows, cw, cb, wa, wx, ba, bx, lam, *, row0, m_s, dec_seq, col0, d_rg):
    tr = min(128, m_s)
    assert m_s % tr == 0 and row0 % tr == 0 and tr % dec_seq == 0 and (dec_seq & (dec_seq - 1)) == 0
    rb0 = row0 // tr
    cb0 = col0 // d_rg
    vec = lambda a: a.reshape(1, d_rg)
    full = lambda a: pl.BlockSpec(a.shape, lambda i: (0,) * a.ndim)
    args = (cw, vec(cb), wa, wx, vec(ba), vec(bx), vec(lam))
    n_prev = e_prev.shape[0]
    return pl.pallas_call(
        functools.partial(_rg_sample_body, dec_seq=dec_seq, first_pos=PAST_LEN),
        grid=(m_s // tr,),
        in_specs=[pl.BlockSpec((tr, d_rg), lambda i: (rb0 + i, cb0)),
                  pl.BlockSpec((tr, d_rg), lambda i: (rb0 + i, cb0 + 1)),
                  pl.BlockSpec((n_prev, tr, d_rg), lambda i: (0, i, 0)),
                  pl.BlockSpec((tr, d_rg), lambda i: (i, 0))] + [full(a) for a in args],
        out_specs=[pl.BlockSpec((tr, d_rg), lambda i: (i, 0)), pl.BlockSpec((tr, d_rg), lambda i: (i, 0))],
        out_shape=[jax.ShapeDtypeStruct((m_s, d_rg), BF16), jax.ShapeDtypeStruct((m_s, d_rg), F32)],
        compiler_params=_cparams(1),
        name="rglru_sample",
    )(p, p, e_prev, h0_rows, *args)


def _lower_bound(logit_rows, layer):
    m = logit_rows[0]
    for z in logit_rows[1:]:
        m = jnp.maximum(m, z)
    e = [jnp.exp(z - m) for z in logit_rows]
    tot = e[0]
    for x in e[1:]:
        tot = tot + x
    if layer == 0:
        return jnp.zeros_like(tot)
    num = e[1]
    for x in e[2:layer + 1]:
        num = num + x
    return num / tot


_GLA_VREGS = GLA_BLOCK // SUBLANES


def _gla_level_masks():
    r = np.arange(GLA_BLOCK)
    t = r // SUBLANES + _GLA_VREGS * (r % SUBLANES)
    tq, tk = t[:, None], t[None, :]
    n_levels = GLA_BLOCK.bit_length()
    masks = np.zeros((n_levels, GLA_BLOCK, GLA_BLOCK), np.float32)
    masks[0] = tq == tk
    for lvl in range(1, n_levels):
        gs = 1 << lvl
        masks[lvl] = (tq > tk) & (tq // gs == tk // gs) & (tq // (gs // 2) != tk // (gs // 2))
    return masks


def _gla_prompt_body(*refs, heads, layer, n_blocks):
    q_refs, f_refs, i_refs, g_refs = (refs[p * heads:(p + 1) * heads] for p in range(4))
    lbl_ref, ng_ref, msk_ref, o_ref, so_ref, st_scr, o_scr = refs[4 * heads:]
    n = pl.program_id(2)

    @pl.when(n == 0)
    def _():
        st_scr[...] = jnp.zeros_like(st_scr)

    nv = _GLA_VREGS
    sub = lax.broadcasted_iota(jnp.int32, (SUBLANES, LANES), 0)
    zero = jnp.zeros((SUBLANES, LANES), F32)
    for hh in range(heads):
        cols = slice(hh * LANES, (hh + 1) * LANES)

        def load(ref):
            return jnp.concatenate([ref[pl.ds(j, SUBLANES, stride=nv), :] for j in range(nv)], axis=0)

        def groups(x):
            return [x[SUBLANES * j:SUBLANES * (j + 1)] for j in range(nv)]

        lb = _lower_bound([lbl_ref[r:r + 1, cols] for r in range(lbl_ref.shape[0])], layer)
        q = _silu(load(q_refs[hh]))
        fg = lb + (1.0 - lb) * jax.nn.sigmoid(load(f_refs[hh]))
        kk = 1.0 - fg
        v = load(i_refs[hh])
        vb = v.astype(BF16)
        lf = groups(jnp.log(fg))

        c = [lf[0]]
        for j in range(1, nv):
            c.append(c[-1] + lf[j])
        tot = c[nv - 1]
        x = tot
        s = 1
        while s < SUBLANES:
            x = x + jnp.where(sub >= s, pltpu.roll(x, s, axis=0), 0.0)
            s *= 2
        before = x - tot
        bj = [cj + before for cj in c]
        b = jnp.concatenate(bj, axis=0)
        b_last = bj[nv - 1][SUBLANES - 1:SUBLANES, :]

        qj = groups(q)
        kj = groups(kk)
        def keep(lvl, scores, acc):
            return jnp.where(msk_ref[lvl] > 0, scores, acc)

        a_mat = keep(0, _dot_nt(q, kk), jnp.zeros((GLA_BLOCK, GLA_BLOCK), F32))
        lvl = 1
        gs = 2
        while gs <= nv:
            hs = gs // 2
            qd, kd = [], []
            for j in range(nv):
                ref = (j // gs) * gs + hs - 1
                if j % gs >= hs:
                    qd.append(qj[j] * jnp.exp(bj[j] - bj[ref]))
                    kd.append(zero)
                else:
                    kd.append(kj[j] if j == ref else kj[j] * jnp.exp(bj[ref] - bj[j]))
                    qd.append(zero)
            a_mat = keep(lvl, _dot_nt(jnp.concatenate(qd, axis=0), jnp.concatenate(kd, axis=0)), a_mat)
            lvl += 1
            gs *= 2
        m = 2
        while m <= SUBLANES:
            src = sub - sub % m + (m // 2 - 1)
            ref = zero
            for s_src in range(m // 2 - 1, SUBLANES, m):
                row_b = jnp.broadcast_to(bj[nv - 1][s_src:s_src + 1, :], (SUBLANES, LANES))
                ref = jnp.where(src == s_src, row_b, ref)
            e = [jnp.exp(-jnp.abs(bj[j] - ref)) for j in range(nv)]
            qd = jnp.concatenate([qj[j] * e[j] for j in range(nv)], axis=0)
            kd = jnp.concatenate([kj[j] * e[j] for j in range(nv)], axis=0)
            a_mat = keep(lvl, _dot_nt(qd, kd), a_mat)
            lvl += 1
            m *= 2

        st = st_scr[hh]
        o = _dot(a_mat, vb) + _dot_nt(q * jnp.exp(b), st)
        st_new = st * jnp.exp(b_last) + _dot_tn(vb, kk * jnp.exp(b_last - b))
        st_scr[hh] = st_new

        o = o * lax.rsqrt(jnp.mean(o * o, axis=-1, keepdims=True) + NORM_EPS) * ng_ref[...]
        o = o * jax.nn.sigmoid(load(g_refs[hh]))
        for j in range(nv):
            o_scr[hh, pl.ds(j, SUBLANES, stride=nv), :] = o[SUBLANES * j:SUBLANES * (j + 1)]
        o_ref[:, cols] = o_scr[hh].astype(o_ref.dtype)

    @pl.when(n == n_blocks - 1)
    def _():
        for hh in range(heads):
            so_ref[hh] = st_scr[hh].T


def _gla_prompt(p, lb_logits, norm_g, *, layer, n_batch, seq, n_heads, total_rows):
    blk = GLA_BLOCK
    hb = GLA_HEADS_PER_STEP
    assert seq % blk == 0 and n_heads % hb == 0
    nb = seq // blk
    hg = n_heads // hb
    w = hb * LANES
    masks = jnp.asarray(_gla_level_masks())

    def head_spec(part, hh):
        return pl.BlockSpec((blk, LANES), lambda b, h, n: (b * nb + n, part * n_heads + h * hb + hh))

    return pl.pallas_call(
        functools.partial(_gla_prompt_body, heads=hb, layer=layer, n_blocks=nb),
        grid=(n_batch, hg, nb),
        in_specs=[head_spec(part, hh) for part in range(4) for hh in range(hb)] + [
                  pl.BlockSpec((lb_logits.shape[0], w), lambda b, h, n: (0, h)),
                  pl.BlockSpec((1, LANES), lambda b, h, n: (0, 0)),
                  pl.BlockSpec(masks.shape, lambda b, h, n: (0, 0, 0))],
        out_specs=[pl.BlockSpec((blk, w), lambda b, h, n: (b * nb + n, h)),
                   pl.BlockSpec((None, hb, LANES, LANES), lambda b, h, n: (b, h, 0, 0))],
        out_shape=[jax.ShapeDtypeStruct((total_rows, n_heads * LANES), BF16),
                   jax.ShapeDtypeStruct((n_batch, n_heads, LANES, LANES), F32)],
        scratch_shapes=[pltpu.VMEM((hb, LANES, LANES), F32), pltpu.VMEM((hb, blk, LANES), F32)],
        compiler_params=_cparams(3),
        name="hgrn_prompt",
    )(*([p] * (4 * hb)), lb_logits, norm_g.reshape(1, LANES), masks)


def _gla_sample_body(q_ref, f_ref, i_ref, g_ref, lbl_ref, ng_ref, s0_ref, *rest, layer):
    o_ref, so_ref, qd_scr, kd_scr, o_scr = rest[-5:]
    n_g, n_t, n_h, _ = q_ref.shape
    lb = _lower_bound([lbl_ref[r] for r in range(lbl_ref.shape[0])], layer)
    q = _silu(q_ref[...])
    fg = lb + (1.0 - lb) * jax.nn.sigmoid(f_ref[...])
    kk = 1.0 - fg
    lf = jnp.log(fg)
    v = i_ref[...]
    bt = [lf[:, 0]]
    for t in range(1, n_t):
        bt.append(bt[-1] + lf[:, t])
    b_last = bt[n_t - 1]
    for t in range(n_t):
        acc = None
        for s in range(t + 1):
            w = q[:, t] * kk[:, s]
            if s < t:
                w = w * jnp.exp(bt[t] - bt[s])
            term = jnp.sum(w, axis=-1, keepdims=True) * v[:, s]
            acc = term if acc is None else acc + term
        o_scr[:, t] = acc
        qd_scr[:, t] = q[:, t] * jnp.exp(bt[t])
        kd_scr[:, t] = kk[:, t] * jnp.exp(b_last - bt[t])
    e_last = jnp.exp(b_last)
    for gi in range(n_g):
        e_cols = e_last[gi].T
        for h in range(n_h):
            s = s0_ref[gi, h]
            o_scr[gi, :, h, :] += _dot(qd_scr[gi, :, h, :], s)
            so_ref[gi, h] = s * e_cols[:, h:h + 1] + _dot_tn(kd_scr[gi, :, h, :], i_ref[gi, :, h, :])
    o = o_scr[...]
    o = o * lax.rsqrt(jnp.mean(o * o, axis=-1, keepdims=True) + NORM_EPS) * ng_ref[...]
    o_ref[...] = (o * jax.nn.sigmoid(g_ref[...])).astype(o_ref.dtype)


def _gla_sample(p4, lb_logits, norm_g, s0, *, layer, prev_state, seq0):
    _, td, h4, _ = p4.shape
    bd = s0.shape[1]
    nh = h4 // 4
    g = min(GLA_SAMPLE_GROUP, bd)
    assert bd % g == 0 and seq0 % g == 0
    blk = (g, td, nh, LANES)
    sblk = (g, nh, LANES, LANES)
    lb3 = lb_logits.reshape(lb_logits.shape[0], nh, LANES)
    part = lambda off: pl.BlockSpec(blk, lambda i: (seq0 // g + i, 0, off, 0))
    in_specs = [part(0), part(1), part(2), part(3),
                pl.BlockSpec(lb3.shape, lambda i: (0, 0, 0)),
                pl.BlockSpec((1, LANES), lambda i: (0, 0)),
                pl.BlockSpec((None,) + sblk, lambda i: (layer, i, 0, 0, 0))]
    args = [p4, p4, p4, p4, lb3, norm_g.reshape(1, LANES), s0]
    so_shape, so_spec, extra, extra_specs, aliases = _stacked_out(
        s0.shape[0], layer, (bd, nh, LANES, LANES), sblk, lambda i: (i, 0, 0, 0), prev_state, len(args))
    return pl.pallas_call(
        functools.partial(_gla_sample_body, layer=layer),
        grid=(bd // g,),
        in_specs=in_specs + extra_specs,
        out_specs=[pl.BlockSpec(blk, lambda i: (i, 0, 0, 0)), so_spec],
        out_shape=[jax.ShapeDtypeStruct((bd, td, nh, LANES), BF16), so_shape],
        input_output_aliases=aliases,
        scratch_shapes=[pltpu.VMEM(blk, F32), pltpu.VMEM(blk, F32), pltpu.VMEM(blk, F32)],
        compiler_params=_cparams(1),
        name="hgrn_sample",
    )(*args, *extra)


def _prev_rows(buf, dec_seq):
    bd, n_prev, d = buf.shape
    outs = []
    for dd in range(1, n_prev + 1):
        rows = [buf[:, n_prev - dd + t] if t < dd else jnp.zeros((bd, d), buf.dtype) for t in range(dec_seq)]
        outs.append(jnp.stack(rows, axis=1).reshape(bd * dec_seq, d))
    return outs


def _first_rows(vals, dec_seq):
    bd, d = vals.shape
    z = jnp.zeros((bd, dec_seq - 1, d), vals.dtype)
    return jnp.concatenate([vals[:, None, :], z], axis=1).reshape(bd * dec_seq, d)


def kernel(x_prompt, x_sample, state_ret, state_rglru_h, state_rglru_conv, state_hgrn, state_ffn_conv,
           ev_w_in, ev_w_out, ev_rg_conv_w, ev_rg_conv_b, ev_rg_wa, ev_rg_ba, ev_rg_wx, ev_rg_bx, ev_rg_lambda,
           od_w_in, od_w_out, od_norm_g, od_lb_logits, ln_g, ln_b, ffn_w_up, ffn_conv_w, ffn_conv_b, ffn_w_down):
    bp, tp, d_model = x_prompt.shape
    bd, td, _ = x_sample.shape
    depth = ln_g.shape[0]
    m_p, m_s = bp * tp, bd * td
    m = m_p + m_s
    alpha = (2.0 * depth) ** 0.25
    h_ret, dk_ret = state_ret.shape[2], state_ret.shape[3]
    d_ret = h_ret * dk_ret
    d_rg = state_rglru_h.shape[-1]
    h_hg = state_hgrn.shape[2]
    d_ff = ffn_conv_b.shape[-1]
    assert state_hgrn.shape[3] == LANES and state_hgrn.shape[4] == LANES and d_ret == d_rg
    assert ffn_conv_w.shape[1] == 3 and td >= 3

    x = (x_prompt.reshape(m_p, d_model), x_sample.reshape(m_s, d_model))
    xb = jnp.concatenate(x, axis=0).astype(BF16)
    pos_p = jnp.arange(tp, dtype=jnp.int32)
    pos_s = PAST_LEN + jnp.arange(td, dtype=jnp.int32)
    zero_ret = jnp.zeros((1, bp) + state_ret.shape[2:], F32)
    n_even = state_ret.shape[0]
    w_out_ev = ev_w_out.astype(BF16)
    w_out_od = od_w_out.astype(BF16)
    w_down = ffn_w_down.astype(BF16)

    ret_p = ret_s = hg_s = ff_s = None
    n_h_p, n_h_s, n_cv_p, n_cv_s, n_hg_p, n_ff_p = [], [], [], [], [], []
    tm_ff = min(ROW_TILE, m_s)
    tm_ln = min(LN_ROW_TILE, m_s)
    tm_in = min(IN_PROJ_ROW_TILE, m_s)
    for l in range(depth):
        if l % 2 == 0:
            e = l // 2
            p = _matmul(xb, ev_w_in, e, tm_in, IN_PROJ_COLS)
            chunk = RET_CHUNK if tp % RET_CHUNK == 0 else tp
            o_ret, ret_p = _retention(p, zero_ret, pos_p, layer=e, n_layers=n_even, prev_state=ret_p, row0=0,
                                      n_batch=bp, n_seq=1, chunk=chunk, n_chunks=tp // chunk, n_heads=h_ret,
                                      dk=dk_ret, heads_per_step=h_ret)
            g_ret = min(RET_SAMPLE_GROUP, bd)
            o_ret_s, ret_s = _retention(p, state_ret, pos_s, layer=e, n_layers=n_even, prev_state=ret_s,
                                        row0=m_p, n_batch=bd // g_ret, n_seq=g_ret, chunk=td, n_chunks=1,
                                        n_heads=h_ret, dk=dk_ret, heads_per_step=1)
            wa = ev_rg_wa[e].astype(BF16)
            wx = ev_rg_wx[e].astype(BF16)
            rg_args = (ev_rg_conv_w[e], ev_rg_conv_b[e], wa, wx, ev_rg_ba[e], ev_rg_bx[e], ev_rg_lambda[e])
            o_rg, hl_p, xt_p = _rg_prompt(p, *rg_args, n_batch=bp, seq=tp, col0=4 * d_ret, d_rg=d_rg,
                                          total_rows=m_p)
            e_prev = jnp.stack(_prev_rows(state_rglru_conv[e], td))
            o_rg_s, h_s = _rg_sample(p, e_prev, _first_rows(state_rglru_h[e], td), *rg_args,
                                     row0=m_p, m_s=m_s, dec_seq=td, col0=4 * d_ret, d_rg=d_rg)
            n_conv = state_rglru_conv.shape[2]
            xr_s = p[m_p:, 4 * d_ret:4 * d_ret + d_rg].reshape(bd, td, d_rg)
            n_h_p.append(hl_p[:, 0])
            n_h_s.append(h_s.reshape(bd, td, d_rg)[:, td - 1])
            n_cv_p.append(xt_p[:, SUBLANES - n_conv:])
            n_cv_s.append(xr_s[:, td - n_conv:])
            x_new, xb = _proj_ln([(o_ret, o_ret_s), (o_rg, o_rg_s)], w_out_ev, e, x, ln_g[l, 0], ln_b[l, 0],
                                 alpha, tm_ln, m_p)
        else:
            o = l // 2
            p = _matmul(xb, od_w_in, o, tm_in, IN_PROJ_COLS)
            o_hg, g_p = _gla_prompt(p, od_lb_logits, od_norm_g[o], layer=o, n_batch=bp, seq=tp,
                                    n_heads=h_hg, total_rows=m_p)
            p4 = p[m_p:].reshape(bd, td, 4 * h_hg, LANES)
            o_hg_s, hg_s = _gla_sample(p4, od_lb_logits, od_norm_g[o], state_hgrn, layer=o, prev_state=hg_s,
                                       seq0=0)
            n_hg_p.append(g_p)
            x_new, xb = _proj_ln([(o_hg, o_hg_s.reshape(m_s, h_hg * LANES))], w_out_od, o, x, ln_g[l, 0],
                                 ln_b[l, 0], alpha, tm_ln, m_p)
        x = (x_new,)
        h, tails, ff_s = _ffn_up(xb, ffn_w_up, l, ffn_conv_w[l], ffn_conv_b[l], state_ffn_conv, ff_s,
                                 m_prompt=m_p, seq=tp, dec_seq=td, tm=tm_ff, tn=512)
        tiles_per_seq = tp // tm_ff
        tails = tails.reshape(m // tm_ff, SUBLANES, d_ff)[:bp * tiles_per_seq]
        tails = tails.reshape(bp, tiles_per_seq, SUBLANES, d_ff)
        n_ff_p.append(tails[:, tiles_per_seq - 1, SUBLANES - 2:])
        last = l == depth - 1
        x_new, xb = _proj_ln([(h,)], w_down, l, x, ln_g[l, 1], ln_b[l, 1], alpha, tm_ln, m_p, split_out=last)
        x = (x_new,)

    y_prompt = x_new.reshape(bp, tp, d_model)
    y_sample = xb.reshape(bd, td, d_model)
    return (y_prompt, y_sample, ret_p, ret_s, jnp.stack(n_h_p), jnp.stack(n_h_s),
            jnp.stack(n_cv_p), jnp.stack(n_cv_s), jnp.stack(n_hg_p), hg_s,
            jnp.stack(n_ff_p), ff_s)
```

```python
import functools

import numpy as np
import jax
import jax.numpy as jnp
from jax import lax
from jax.experimental import pallas as pl
from jax.experimental.pallas import tpu as pltpu

F32 = jnp.float32
BF16 = jnp.bfloat16

LN_EPS = 1e-5
NORM_EPS = 1e-6
ROPE_BASE = 10000.0
RG_C = 8.0
TINY = 1e-37
GELU_C0 = 0.7978845608028654
GELU_C1 = 0.044715
PAST_LEN = 16384
LANES = 128
SUBLANES = 8
MXU_COLS = 256
ROW_TILE = 512
FFN_ROW_SUB = 256
LN_ROW_TILE = 256
MIX_LN_ROW_TILE = 512
IN_PROJ_ROW_TILE = 512
IN_PROJ_COLS = 2048
RET_CHUNK = 256
RET_SAMPLE_GROUP = 8
RG_TIME_TILE = 256
GLA_BLOCK = 128
GLA_HEADS_PER_STEP = 8
GLA_SAMPLE_GROUP = 4
MIB = 1024 * 1024


def _cparams(n_axes, vmem_mib=48):
    return pltpu.CompilerParams(dimension_semantics=("arbitrary",) * n_axes,
                                vmem_limit_bytes=vmem_mib * MIB)


def _dot(a, b):
    return jnp.dot(a.astype(BF16), b.astype(BF16), preferred_element_type=F32)


def _dot_nt(a, b):
    return lax.dot_general(a.astype(BF16), b.astype(BF16), (((1,), (1,)), ((), ())),
                           preferred_element_type=F32)


def _dot_tn(a, b):
    return lax.dot_general(a.astype(BF16), b.astype(BF16), (((0,), (0,)), ((), ())),
                           preferred_element_type=F32)


def _silu(x):
    return x * jax.nn.sigmoid(x)


def _mm_body(x_ref, w_ref, o_ref, wb_scr):
    @pl.when(pl.program_id(1) == 0)
    def _():
        wb_scr[...] = w_ref[...].astype(BF16)

    o_ref[...] = jnp.dot(x_ref[...], wb_scr[...], preferred_element_type=F32).astype(o_ref.dtype)


def _matmul(xb, w, layer, tm, tn):
    m, k = xb.shape
    n = w.shape[2]
    assert m % tm == 0 and n % tn == 0
    return pl.pallas_call(
        _mm_body,
        grid=(n // tn, m // tm),
        in_specs=[pl.BlockSpec((tm, k), lambda j, i: (i, 0)),
                  pl.BlockSpec((None, k, tn), lambda j, i: (layer, 0, j))],
        out_specs=pl.BlockSpec((tm, tn), lambda j, i: (i, j)),
        out_shape=jax.ShapeDtypeStruct((m, n), F32),
        scratch_shapes=[pltpu.VMEM((k, tn), BF16)],
        compiler_params=_cparams(2, vmem_mib=56),
        name="in_proj",
    )(xb, w)


def _proj_ln_body(*refs, part_arity, res_arity, alpha, n_prompt_tiles, split_out):
    i = pl.program_id(0)
    n_in = sum(part_arity)
    a_refs = refs[:n_in]
    w_refs = refs[n_in:n_in + len(part_arity)]
    x_refs = refs[n_in + len(part_arity):][:res_arity]
    g_ref, b_ref, o1_ref, o2_ref = refs[n_in + len(part_arity) + res_arity:]

    def rows_of(group):
        if len(group) == 1:
            return group[0][...]
        return jnp.where(i < n_prompt_tiles, group[0][...], group[1][...])

    acc = None
    pos = 0
    for arity, w_ref in zip(part_arity, w_refs):
        d = jnp.dot(rows_of(a_refs[pos:pos + arity]), w_ref[...], preferred_element_type=F32)
        pos += arity
        acc = d if acc is None else acc + d
    y = alpha * rows_of(x_refs) + acc
    mu = jnp.mean(y, axis=-1, keepdims=True)
    d = y - mu
    var = jnp.mean(d * d, axis=-1, keepdims=True)
    out = d * lax.rsqrt(var + LN_EPS) * g_ref[...] + b_ref[...]
    if split_out:
        @pl.when(i < n_prompt_tiles)
        def _():
            o1_ref[...] = out

        @pl.when(i >= n_prompt_tiles)
        def _():
            o2_ref[...] = out
    else:
        o1_ref[...] = out
        o2_ref[...] = out.astype(BF16)


def _proj_ln(parts, wb, layer, x, g, b, alpha, tm, m_prompt, split_out=False):
    m = sum(a.shape[0] for a in x)
    d = x[0].shape[1]
    assert m % tm == 0 and m_prompt % tm == 0
    npt = m_prompt // tm
    kp = parts[0][0].shape[1]
    assert all(a.shape[1] == kp for p in parts for a in p) and wb.shape[1] == kp * len(parts)
    prompt_rows = lambda i: (jnp.minimum(i, npt - 1), 0)
    sample_rows = lambda i: (jnp.maximum(i - npt, 0), 0)

    def row_specs(group, width):
        if len(group) == 1:
            return [pl.BlockSpec((tm, width), lambda i: (i, 0))]
        return [pl.BlockSpec((tm, width), prompt_rows), pl.BlockSpec((tm, width), sample_rows)]

    in_specs, args = [], []
    for p in parts:
        args += list(p)
        in_specs += row_specs(p, kp)
    for part in range(len(parts)):
        in_specs.append(pl.BlockSpec((None, kp, d), lambda i, part=part: (layer, part, 0),
                                     pipeline_mode=pl.Buffered(1)))
    in_specs += row_specs(x, d)
    in_specs += [pl.BlockSpec((1, d), lambda i: (0, 0)),
                 pl.BlockSpec((1, d), lambda i: (0, 0))]
    if split_out:
        out_specs = [pl.BlockSpec((tm, d), prompt_rows), pl.BlockSpec((tm, d), sample_rows)]
        out_shape = [jax.ShapeDtypeStruct((m_prompt, d), F32), jax.ShapeDtypeStruct((m - m_prompt, d), F32)]
    else:
        out_specs = [pl.BlockSpec((tm, d), lambda i: (i, 0)), pl.BlockSpec((tm, d), lambda i: (i, 0))]
        out_shape = [jax.ShapeDtypeStruct((m, d), F32), jax.ShapeDtypeStruct((m, d), BF16)]
    return pl.pallas_call(
        functools.partial(_proj_ln_body, part_arity=tuple(len(p) for p in parts), res_arity=len(x), alpha=alpha,
                          n_prompt_tiles=npt, split_out=split_out),
        grid=(m // tm,),
        in_specs=in_specs,
        out_specs=out_specs,
        out_shape=out_shape,
        compiler_params=_cparams(1, vmem_mib=56),
        name="proj_ln",
    )(*args, *([wb] * len(parts)), *x, g.reshape(1, d), b.reshape(1, d))


def _ffn_up_body(x_ref, wu_ref, wv_ref, cw_ref, cb_ref, buf_ref, *rest,
                 n_prompt_tiles, tiles_per_seq, dec_seq):
    h_ref, tail_ref, fs_ref, wub_scr, wvb_scr, carry_ref, e1_scr, e2_scr, us_scr, uh_scr = rest[-10:]
    i = pl.program_id(1)

    @pl.when(i == 0)
    def _():
        wub_scr[...] = wu_ref[...].astype(BF16)
        wvb_scr[...] = wv_ref[...].astype(BF16)

    tm = x_ref.shape[0]
    tn = h_ref.shape[1]
    cw = min(MXU_COLS, tn)
    rs = min(FFN_ROW_SUB, tm)
    lane_blocks = cw // LANES
    row = lax.broadcasted_iota(jnp.int32, (rs, cw), 0)

    def dots(c, r):
        cols = slice(c * cw, (c + 1) * cw)
        x = x_ref[r * rs:(r + 1) * rs, :]
        return (jnp.dot(x, wub_scr[:, cols], preferred_element_type=F32),
                jnp.dot(x, wvb_scr[:, cols], preferred_element_type=F32))

    def epilogue(c, r, u, u1, u2, v):
        cols = slice(c * cw, (c + 1) * cw)
        uc = u2 * cw_ref[0:1, cols] + cb_ref[:, cols]
        uc = uc + u1 * cw_ref[1:2, cols]
        uc = uc + u * cw_ref[2:3, cols]
        th = jnp.tanh(uc * (GELU_C0 + (GELU_C0 * GELU_C1) * (uc * uc)))
        h_ref[r * rs:(r + 1) * rs, cols] = ((uc + uc * th) * (0.5 * v)).astype(h_ref.dtype)

    @pl.when(i < n_prompt_tiles)
    def _():
        first = i % tiles_per_seq == 0
        n_r = tm // rs
        for c in range(tn // cw):
            cols = slice(c * cw, (c + 1) * cw)
            prev = jnp.where(first, 0.0, carry_ref[:, cols])
            for r in range(n_r):
                slot = (c * n_r + r) % 2
                u, v = dots(c, r)
                uh_scr[slot, 0:SUBLANES, :] = prev
                uh_scr[slot, SUBLANES:, :] = u
                u1 = uh_scr[slot, SUBLANES - 1:SUBLANES - 1 + rs, :]
                u2 = uh_scr[slot, SUBLANES - 2:SUBLANES - 2 + rs, :]
                epilogue(c, r, u, u1, u2, v)
                prev = u[rs - SUBLANES:]
            carry_ref[:, cols] = prev
            tail_ref[:, cols] = prev

    @pl.when(i >= n_prompt_tiles)
    def _():
        t = row % dec_seq
        n_seq = tm // dec_seq
        e1_scr[...] = jnp.zeros_like(e1_scr)
        e2_scr[...] = jnp.zeros_like(e2_scr)
        for lb in range(tn // LANES):
            lanes = slice(lb * LANES, (lb + 1) * LANES)
            e1_scr[lb, pl.ds(0, n_seq, stride=dec_seq), :] = buf_ref[:, 1, lanes]
            e2_scr[lb, pl.ds(0, n_seq, stride=dec_seq), :] = buf_ref[:, 0, lanes]
            e2_scr[lb, pl.ds(1, n_seq, stride=dec_seq), :] = buf_ref[:, 1, lanes]
        for c in range(tn // cw):
            for r in range(tm // rs):
                rows = slice(r * rs, (r + 1) * rs)
                lbs = range(c * lane_blocks, (c + 1) * lane_blocks)
                e1 = jnp.concatenate([e1_scr[lb, rows, :] for lb in lbs], axis=1)
                e2 = jnp.concatenate([e2_scr[lb, rows, :] for lb in lbs], axis=1)
                u, v = dots(c, r)
                epilogue(c, r, u, jnp.where(t >= 1, pltpu.roll(u, 1, axis=0), e1),
                         jnp.where(t >= 2, pltpu.roll(u, 2, axis=0), e2), v)
                for k, lb in enumerate(lbs):
                    us_scr[lb, rows, :] = u[:, k * LANES:(k + 1) * LANES]
        for lb in range(tn // LANES):
            lanes = slice(lb * LANES, (lb + 1) * LANES)
            fs_ref[:, 0, lanes] = us_scr[lb, pl.ds(dec_seq - 2, n_seq, stride=dec_seq), :]
            fs_ref[:, 1, lanes] = us_scr[lb, pl.ds(dec_seq - 1, n_seq, stride=dec_seq), :]
        tail_ref[...] = jnp.zeros_like(tail_ref)


def _ffn_up(xb, w_up, layer, cw, cb, conv_state, prev_state, *, m_prompt, seq, dec_seq, tm, tn):
    m, k = xb.shape
    d_ff = w_up.shape[2] // 2
    n_layers, bd = conv_state.shape[:2]
    m_s = m - m_prompt
    assert m_prompt % tm == 0 and m_s % tm == 0 and seq % tm == 0 and d_ff % tn == 0
    assert tm % dec_seq == 0 and dec_seq >= 2 and cw.shape[0] == 3 and conv_state.shape[2] == 2
    assert tn % min(MXU_COLS, tn) == 0 and tm % min(FFN_ROW_SUB, tm) == 0 and m_s == bd * dec_seq
    npt = m_prompt // tm
    nj = d_ff // tn
    n_seq = tm // dec_seq
    in_specs = [pl.BlockSpec((tm, k), lambda j, i: (i, 0)),
                pl.BlockSpec((None, k, tn), lambda j, i: (layer, 0, j)),
                pl.BlockSpec((None, k, tn), lambda j, i: (layer, 0, j + nj)),
                pl.BlockSpec((3, tn), lambda j, i: (0, j)),
                pl.BlockSpec((1, tn), lambda j, i: (0, j)),
                pl.BlockSpec((None, n_seq, 2, tn), lambda j, i: (layer, jnp.maximum(i - npt, 0), 0, j))]
    args = [xb, w_up, w_up, cw, cb.reshape(1, d_ff), conv_state]
    fs_shape, fs_spec, extra, extra_specs, aliases = _stacked_out(
        n_layers, layer, (bd, 2, d_ff), (n_seq, 2, tn), lambda j, i: (jnp.maximum(i - npt, 0), 0, j),
        prev_state, len(args), out_index=2)
    n_lb = tn // LANES
    return pl.pallas_call(
        functools.partial(_ffn_up_body, n_prompt_tiles=npt, tiles_per_seq=seq // tm, dec_seq=dec_seq),
        grid=(nj, m // tm),
        in_specs=in_specs + extra_specs,
        out_specs=[pl.BlockSpec((tm, tn), lambda j, i: (i, j)),
                   pl.BlockSpec((SUBLANES, tn), lambda j, i: (i, j)),
                   fs_spec],
        out_shape=[jax.ShapeDtypeStruct((m, d_ff), BF16),
                   jax.ShapeDtypeStruct((m // tm * SUBLANES, d_ff), F32),
                   fs_shape],
        input_output_aliases=aliases,
        scratch_shapes=[pltpu.VMEM((k, tn), BF16), pltpu.VMEM((k, tn), BF16), pltpu.VMEM((SUBLANES, tn), F32),
                        pltpu.VMEM((n_lb, tm, LANES), F32), pltpu.VMEM((n_lb, tm, LANES), F32),
                        pltpu.VMEM((n_lb, tm, LANES), F32),
                        pltpu.VMEM((2, min(FFN_ROW_SUB, tm) + SUBLANES, min(MXU_COLS, tn)), F32)],
        compiler_params=_cparams(2),
        name="ffn_up",
    )(*args, *extra)


def _ret_body(q_ref, k_ref, v_ref, g_ref, cos_ref, sin_ref, dm_ref, xi_ref, zeta_ref, cdec_ref, s0_ref,
              *rest, n_seq, chunk, n_chunks, scale):
    o_ref, so_ref, s_scr, q_scr, kz_scr, o_scr = rest[-6:]
    n = pl.program_id(2)

    @pl.when(n == 0)
    def _():
        s_scr[...] = s0_ref[...]

    n_heads = dm_ref.shape[0]
    dk = q_ref.shape[1] // n_heads
    half = dk // 2
    cos = cos_ref[...]
    sin = sin_ref[...]

    def rot(x):
        x1 = x[:, :half]
        x2 = x[:, half:]
        return jnp.concatenate([x1 * cos - x2 * sin, x2 * cos + x1 * sin], axis=-1)

    for hh in range(n_heads):
        cols = slice(hh * dk, (hh + 1) * dk)
        q = rot(q_ref[:, cols])
        k = rot(k_ref[:, cols]) * scale
        vb = v_ref[:, cols].astype(BF16)
        inner = _dot_nt(q, k) * dm_ref[hh]
        o_scr[hh] = _dot(inner, vb)
        q_scr[hh] = q
        kz_scr[hh] = k * zeta_ref[hh]
        cdec = cdec_ref[hh]
        for gi in range(n_seq):
            rows = pl.ds(gi * chunk, chunk)
            s = s_scr[gi, hh]
            o_scr[hh, rows, :] += _dot(q_scr[hh, rows, :], s) * xi_ref[hh, rows, :]
            s_scr[gi, hh] = s * cdec + _dot_tn(kz_scr[hh, rows, :], v_ref[rows, cols])
        o = o_scr[hh]
        gate = g_ref[:, cols]
        o = o * lax.rsqrt(jnp.mean(o * o, axis=-1, keepdims=True) + NORM_EPS) * _silu(gate)
        o_ref[:, cols] = o.astype(o_ref.dtype)

    @pl.when(n == n_chunks - 1)
    def _():
        so_ref[...] = s_scr[...]


def _ret_tables(n_heads, chunk, n_seq):
    lg = jnp.log(1.0 - 2.0 ** (-5.0 - jnp.arange(n_heads, dtype=F32)))
    idx = jnp.arange(chunk, dtype=F32)
    diff = idx[:, None] - idx[None, :]
    dmask = jnp.where(diff[None] >= 0, jnp.exp(jnp.maximum(diff, 0.0)[None] * lg[:, None, None]), 0.0)
    eye = jnp.eye(n_seq, dtype=F32)
    dm = jnp.einsum('ab,hts->hatbs', eye, dmask).reshape(n_heads, n_seq * chunk, n_seq * chunk)
    xi = jnp.tile(jnp.exp((idx[None, :] + 1.0) * lg[:, None]), (1, n_seq))[:, :, None]
    zeta = jnp.tile(jnp.exp((chunk - 1.0 - idx[None, :]) * lg[:, None]), (1, n_seq))[:, :, None]
    cdec = jnp.exp(chunk * lg)[:, None, None]
    return dm, xi, zeta, cdec


def _rope_tables(pos, half):
    inv = ROPE_BASE ** (-jnp.arange(half, dtype=F32) / half)
    ang = pos.astype(F32)[:, None] * inv[None, :]
    return jnp.cos(ang), jnp.sin(ang)


def _stacked_out(n_layers, layer, shape, block, index_map, prev, n_inputs, out_index=1):
    sds = jax.ShapeDtypeStruct((n_layers,) + tuple(shape), F32)
    spec = pl.BlockSpec((None,) + tuple(block), lambda *g: (layer,) + tuple(index_map(*g)))
    if prev is None:
        return sds, spec, [], [], {}
    return sds, spec, [prev], [pl.BlockSpec(memory_space=pl.ANY)], {n_inputs: out_index}


def _retention(p, s0, pos, *, layer, n_layers, prev_state, row0, n_batch, n_seq, chunk, n_chunks, n_heads, dk,
               heads_per_step):
    rows = n_seq * chunk
    assert row0 % rows == 0
    rb0 = row0 // rows
    dm, xi, zeta, cdec = _ret_tables(n_heads, chunk, n_seq)
    cos, sin = _rope_tables(pos, dk // 2)
    if n_seq > 1:
        assert n_chunks == 1
        cos = jnp.tile(cos, (n_seq, 1))
        sin = jnp.tile(sin, (n_seq, 1))
    hps = heads_per_step
    assert n_heads % hps == 0
    hg = n_heads // hps
    w = hps * dk

    def col(off):
        return lambda b, h, n: (rb0 + b * n_chunks + n, off * hg + h)

    tab = lambda b, h, n: (h, 0, 0)
    s0_layer = layer if s0.shape[0] > 1 else 0
    in_specs = [pl.BlockSpec((rows, w), col(0)), pl.BlockSpec((rows, w), col(1)),
                pl.BlockSpec((rows, w), col(2)), pl.BlockSpec((rows, w), col(3)),
                pl.BlockSpec((rows, dk // 2), lambda b, h, n: (n, 0)),
                pl.BlockSpec((rows, dk // 2), lambda b, h, n: (n, 0)),
                pl.BlockSpec((hps, rows, rows), tab),
                pl.BlockSpec((hps, rows, 1), tab),
                pl.BlockSpec((hps, rows, 1), tab),
                pl.BlockSpec((hps, 1, 1), tab),
                pl.BlockSpec((None, n_seq, hps, dk, dk), lambda b, h, n: (s0_layer, b, h, 0, 0))]
    args = [p, p, p, p, cos, sin, dm, xi, zeta, cdec, s0]
    so_shape, so_spec, extra, extra_specs, aliases = _stacked_out(
        n_layers, layer, (n_batch * n_seq, n_heads, dk, dk), (n_seq, hps, dk, dk),
        lambda b, h, n: (b, h, 0, 0), prev_state, len(args))
    total_rows = n_batch * n_chunks * rows
    return pl.pallas_call(
        functools.partial(_ret_body, n_seq=n_seq, chunk=chunk, n_chunks=n_chunks, scale=dk ** -0.5),
        grid=(n_batch, hg, n_chunks),
        in_specs=in_specs + extra_specs,
        out_specs=[pl.BlockSpec((rows, w), lambda b, h, n: (b * n_chunks + n, h)), so_spec],
        out_shape=[jax.ShapeDtypeStruct((total_rows, n_heads * dk), BF16), so_shape],
        input_output_aliases=aliases,
        scratch_shapes=[pltpu.VMEM((n_seq, hps, dk, dk), F32), pltpu.VMEM((hps, rows, dk), F32),
                        pltpu.VMEM((hps, rows, dk), F32), pltpu.VMEM((hps, rows, dk), F32)],
        compiler_params=_cparams(3),
        name="retention",
    )(*args, *extra)


def _rg_gates(xc, wa_ref, wx_ref, ba, bx, lam):
    nb, bw, _ = wa_ref.shape
    ra, ia = [], []
    for blk in range(nb):
        xb = xc[:, blk * bw:(blk + 1) * bw].astype(BF16)
        ra.append(jnp.dot(xb, wa_ref[blk], preferred_element_type=F32))
        ia.append(jnp.dot(xb, wx_ref[blk], preferred_element_type=F32))
    r = jax.nn.sigmoid(jnp.concatenate(ra, axis=-1) + ba)
    i = jax.nn.sigmoid(jnp.concatenate(ia, axis=-1) + bx)
    log_a = -RG_C * r * jax.nn.softplus(-lam)
    a = jnp.exp(log_a)
    y = -jnp.tanh(log_a) * (a * a + 1.0)
    mult = y * lax.rsqrt(jnp.maximum(y, TINY))
    return a, mult, i


def _scan_rows(a, b, row_in_seg, seg_len):
    s = 1
    while s < seg_len:
        a_sh = pltpu.roll(a, s, axis=0)
        b_sh = pltpu.roll(b, s, axis=0)
        m = row_in_seg >= s
        b = jnp.where(m, a * b_sh + b, b)
        a = jnp.where(m, a * a_sh, a)
        s *= 2
    return a, b


def _rg_prompt_body(xr_ref, gr_ref, cw_ref, cb_ref, wa_ref, wx_ref, ba_ref, bx_ref, lam_ref,
                    o_ref, hl_ref, xt_ref, h_scr, x_scr, *, n_tiles):
    n = pl.program_id(1)

    @pl.when(n == 0)
    def _():
        h_scr[...] = jnp.zeros_like(h_scr)
        x_scr[...] = jnp.zeros_like(x_scr)

    xr = xr_ref[...]
    tc = xr.shape[0]
    row = lax.broadcasted_iota(jnp.int32, xr.shape, 0)
    row8 = lax.broadcasted_iota(jnp.int32, (SUBLANES, xr.shape[1]), 0)
    prev = x_scr[...]
    n_tap = cw_ref.shape[0]
    xc = None
    for j in range(n_tap):
        d = n_tap - 1 - j
        if d == 0:
            xd = xr
        else:
            rolled = pltpu.roll(xr, d, axis=0)
            head = jnp.where(row8 < d, pltpu.roll(prev, d, axis=0), rolled[:SUBLANES])
            xd = jnp.concatenate([head, rolled[SUBLANES:]], axis=0)
        term = xd * cw_ref[j:j + 1, :]
        xc = term + cb_ref[...] if xc is None else xc + term
    a, mult, gate_i = _rg_gates(xc, wa_ref, wx_ref, ba_ref[...], bx_ref[...], lam_ref[...])
    mult = jnp.where(row + n * tc == 0, 1.0, mult)
    bterm = xc * gate_i * mult
    a_grp, b_grp = _scan_rows(a, bterm, row % SUBLANES, SUBLANES)
    h_prev = h_scr[...]
    groups = []
    for gi in range(tc // SUBLANES):
        rows = slice(gi * SUBLANES, (gi + 1) * SUBLANES)
        h_g = a_grp[rows] * h_prev + b_grp[rows]
        groups.append(h_g)
        h_prev = h_g[SUBLANES - 1:SUBLANES]
    h = jnp.concatenate(groups, axis=0)
    o_ref[...] = (h * jax.nn.gelu(gr_ref[...])).astype(o_ref.dtype)
    h_scr[...] = h[tc - 1:tc, :]
    x_scr[...] = xr[tc - SUBLANES:, :]

    @pl.when(n == n_tiles - 1)
    def _():
        hl_ref[...] = h[tc - 1:tc, :]
        xt_ref[...] = xr[tc - SUBLANES:, :]


def _rg_prompt(p, cw, cb, wa, wx, ba, bx, lam, *, n_batch, seq, col0, d_rg, total_rows):
    tc = min(RG_TIME_TILE, seq)
    assert seq % tc == 0 and col0 % d_rg == 0 and (tc & (tc - 1)) == 0
    nt = seq // tc
    cb0 = col0 // d_rg
    vec = lambda a: a.reshape(1, d_rg)
    full = lambda a: pl.BlockSpec(a.shape, lambda b, n: (0,) * a.ndim)
    args = (cw, vec(cb), wa, wx, vec(ba), vec(bx), vec(lam))
    return pl.pallas_call(
        functools.partial(_rg_prompt_body, n_tiles=nt),
        grid=(n_batch, nt),
        in_specs=[pl.BlockSpec((tc, d_rg), lambda b, n: (b * nt + n, cb0)),
                  pl.BlockSpec((tc, d_rg), lambda b, n: (b * nt + n, cb0 + 1))] + [full(a) for a in args],
        out_specs=[pl.BlockSpec((tc, d_rg), lambda b, n: (b * nt + n, 0)),
                   pl.BlockSpec((None, 1, d_rg), lambda b, n: (b, 0, 0)),
                   pl.BlockSpec((None, SUBLANES, d_rg), lambda b, n: (b, 0, 0))],
        out_shape=[jax.ShapeDtypeStruct((total_rows, d_rg), BF16),
                   jax.ShapeDtypeStruct((n_batch, 1, d_rg), F32),
                   jax.ShapeDtypeStruct((n_batch, SUBLANES, d_rg), F32)],
        scratch_shapes=[pltpu.VMEM((1, d_rg), F32), pltpu.VMEM((SUBLANES, d_rg), F32)],
        compiler_params=_cparams(2),
        name="rglru_prompt",
    )(p, p, *args)


def _rg_sample_body(xr_ref, gr_ref, e_ref, h0_ref, cw_ref, cb_ref, wa_ref, wx_ref, ba_ref, bx_ref, lam_ref,
                    o_ref, h_ref, *, dec_seq, first_pos):
    xr = xr_ref[...]
    row = lax.broadcasted_iota(jnp.int32, xr.shape, 0)
    t = row % dec_seq
    n_tap = cw_ref.shape[0]
    xc = None
    for j in range(n_tap):
        d = n_tap - 1 - j
        xd = xr if d == 0 else jnp.where(t >= d, pltpu.roll(xr, d, axis=0), e_ref[d - 1])
        term = xd * cw_ref[j:j + 1, :]
        xc = term + cb_ref[...] if xc is None else xc + term
    a, mult, gate_i = _rg_gates(xc, wa_ref, wx_ref, ba_ref[...], bx_ref[...], lam_ref[...])
    if first_pos == 0:
        mult = jnp.where(t == 0, 1.0, mult)
    bterm = xc * gate_i * mult + jnp.where(t == 0, a * h0_ref[...], 0.0)
    _, h = _scan_rows(a, bterm, t, dec_seq)
    h_ref[...] = h
    o_ref[...] = (h * jax.nn.gelu(gr_ref[...])).astype(o_ref.dtype)


def _rg_sample(p, e_prev, h0_rows, cw, cb, wa, wx, ba, bx, lam, *, row0, m_s, dec_seq, col0, d_rg):
    tr = min(128, m_s)
    assert m_s % tr == 0 and row0 % tr == 0 and tr % dec_seq == 0 and (dec_seq & (dec_seq - 1)) == 0
    rb0 = row0 // tr
    cb0 = col0 // d_rg
    vec = lambda a: a.reshape(1, d_rg)
    full = lambda a: pl.BlockSpec(a.shape, lambda i: (0,) * a.ndim)
    args = (cw, vec(cb), wa, wx, vec(ba), vec(bx), vec(lam))
    n_prev = e_prev.shape[0]
    return pl.pallas_call(
        functools.partial(_rg_sample_body, dec_seq=dec_seq, first_pos=PAST_LEN),
        grid=(m_s // tr,),
        in_specs=[pl.BlockSpec((tr, d_rg), lambda i: (rb0 + i, cb0)),
                  pl.BlockSpec((tr, d_rg), lambda i: (rb0 + i, cb0 + 1)),
                  pl.BlockSpec((n_prev, tr, d_rg), lambda i: (0, i, 0)),
                  pl.BlockSpec((tr, d_rg), lambda i: (i, 0))] + [full(a) for a in args],
        out_specs=[pl.BlockSpec((tr, d_rg), lambda i: (i, 0)), pl.BlockSpec((tr, d_rg), lambda i: (i, 0))],
        out_shape=[jax.ShapeDtypeStruct((m_s, d_rg), BF16), jax.ShapeDtypeStruct((m_s, d_rg), F32)],
        compiler_params=_cparams(1),
        name="rglru_sample",
    )(p, p, e_prev, h0_rows, *args)


def _lower_bound(logit_rows, layer):
    m = logit_rows[0]
    for z in logit_rows[1:]:
        m = jnp.maximum(m, z)
    e = [jnp.exp(z - m) for z in logit_rows]
    tot = e[0]
    for x in e[1:]:
        tot = tot + x
    if layer == 0:
        return jnp.zeros_like(tot)
    num = e[1]
    for x in e[2:layer + 1]:
        num = num + x
    return num / tot


_GLA_VREGS = GLA_BLOCK // SUBLANES


def _gla_level_masks():
    r = np.arange(GLA_BLOCK)
    t = r // SUBLANES + _GLA_VREGS * (r % SUBLANES)
    tq, tk = t[:, None], t[None, :]
    n_levels = GLA_BLOCK.bit_length()
    masks = np.zeros((n_levels, GLA_BLOCK, GLA_BLOCK), np.float32)
    masks[0] = tq == tk
    for lvl in range(1, n_levels):
        gs = 1 << lvl
        masks[lvl] = (tq > tk) & (tq // gs == tk // gs) & (tq // (gs // 2) != tk // (gs // 2))
    return masks


def _gla_prompt_body(*refs, heads, layer, n_blocks):
    q_refs, f_refs, i_refs, g_refs = (refs[p * heads:(p + 1) * heads] for p in range(4))
    lbl_ref, ng_ref, msk_ref, o_ref, so_ref, st_scr, o_scr = refs[4 * heads:]
    n = pl.program_id(2)

    @pl.when(n == 0)
    def _():
        st_scr[...] = jnp.zeros_like(st_scr)

    nv = _GLA_VREGS
    sub = lax.broadcasted_iota(jnp.int32, (SUBLANES, LANES), 0)
    zero = jnp.zeros((SUBLANES, LANES), F32)
    for hh in range(heads):
        cols = slice(hh * LANES, (hh + 1) * LANES)

        def load(ref):
            return jnp.concatenate([ref[pl.ds(j, SUBLANES, stride=nv), :] for j in range(nv)], axis=0)

        def groups(x):
            return [x[SUBLANES * j:SUBLANES * (j + 1)] for j in range(nv)]

        lb = _lower_bound([lbl_ref[r:r + 1, cols] for r in range(lbl_ref.shape[0])], layer)
        q = _silu(load(q_refs[hh]))
        fg = lb + (1.0 - lb) * jax.nn.sigmoid(load(f_refs[hh]))
        kk = 1.0 - fg
        v = load(i_refs[hh])
        vb = v.astype(BF16)
        lf = groups(jnp.log(fg))

        c = [lf[0]]
        for j in range(1, nv):
            c.append(c[-1] + lf[j])
        tot = c[nv - 1]
        x = tot
        s = 1
        while s < SUBLANES:
            x = x + jnp.where(sub >= s, pltpu.roll(x, s, axis=0), 0.0)
            s *= 2
        before = x - tot
        bj = [cj + before for cj in c]
        b = jnp.concatenate(bj, axis=0)
        b_last = bj[nv - 1][SUBLANES - 1:SUBLANES, :]

        qj = groups(q)
        kj = groups(kk)
        def keep(lvl, scores, acc):
            return jnp.where(msk_ref[lvl] > 0, scores, acc)

        a_mat = keep(0, _dot_nt(q, kk), jnp.zeros((GLA_BLOCK, GLA_BLOCK), F32))
        lvl = 1
        gs = 2
        while gs <= nv:
            hs = gs // 2
            qd, kd = [], []
            for j in range(nv):
                ref = (j // gs) * gs + hs - 1
                if j % gs >= hs:
                    qd.append(qj[j] * jnp.exp(bj[j] - bj[ref]))
                    kd.append(zero)
                else:
                    kd.append(kj[j] if j == ref else kj[j] * jnp.exp(bj[ref] - bj[j]))
                    qd.append(zero)
            a_mat = keep(lvl, _dot_nt(jnp.concatenate(qd, axis=0), jnp.concatenate(kd, axis=0)), a_mat)
            lvl += 1
            gs *= 2
        m = 2
        while m <= SUBLANES:
            src = sub - sub % m + (m // 2 - 1)
            ref = zero
            for s_src in range(m // 2 - 1, SUBLANES, m):
                row_b = jnp.broadcast_to(bj[nv - 1][s_src:s_src + 1, :], (SUBLANES, LANES))
                ref = jnp.where(src == s_src, row_b, ref)
            e = [jnp.exp(-jnp.abs(bj[j] - ref)) for j in range(nv)]
            qd = jnp.concatenate([qj[j] * e[j] for j in range(nv)], axis=0)
            kd = jnp.concatenate([kj[j] * e[j] for j in range(nv)], axis=0)
            a_mat = keep(lvl, _dot_nt(qd, kd), a_mat)
            lvl += 1
            m *= 2

        st = st_scr[hh]
        o = _dot(a_mat, vb) + _dot_nt(q * jnp.exp(b), st)
        st_new = st * jnp.exp(b_last) + _dot_tn(vb, kk * jnp.exp(b_last - b))
        st_scr[hh] = st_new

        o = o * lax.rsqrt(jnp.mean(o * o, axis=-1, keepdims=True) + NORM_EPS) * ng_ref[...]
        o = o * jax.nn.sigmoid(load(g_refs[hh]))
        for j in range(nv):
            o_scr[hh, pl.ds(j, SUBLANES, stride=nv), :] = o[SUBLANES * j:SUBLANES * (j + 1)]
        o_ref[:, cols] = o_scr[hh].astype(o_ref.dtype)

    @pl.when(n == n_blocks - 1)
    def _():
        for hh in range(heads):
            so_ref[hh] = st_scr[hh].T


def _gla_prompt(p, lb_logits, norm_g, *, layer, n_batch, seq, n_heads, total_rows):
    blk = GLA_BLOCK
    hb = GLA_HEADS_PER_STEP
    assert seq % blk == 0 and n_heads % hb == 0
    nb = seq // blk
    hg = n_heads // hb
    w = hb * LANES
    masks = jnp.asarray(_gla_level_masks())

    def head_spec(part, hh):
        return pl.BlockSpec((blk, LANES), lambda b, h, n: (b * nb + n, part * n_heads + h * hb + hh))

    return pl.pallas_call(
        functools.partial(_gla_prompt_body, heads=hb, layer=layer, n_blocks=nb),
        grid=(n_batch, hg, nb),
        in_specs=[head_spec(part, hh) for part in range(4) for hh in range(hb)] + [
                  pl.BlockSpec((lb_logits.shape[0], w), lambda b, h, n: (0, h)),
                  pl.BlockSpec((1, LANES), lambda b, h, n: (0, 0)),
                  pl.BlockSpec(masks.shape, lambda b, h, n: (0, 0, 0))],
        out_specs=[pl.BlockSpec((blk, w), lambda b, h, n: (b * nb + n, h)),
                   pl.BlockSpec((None, hb, LANES, LANES), lambda b, h, n: (b, h, 0, 0))],
        out_shape=[jax.ShapeDtypeStruct((total_rows, n_heads * LANES), BF16),
                   jax.ShapeDtypeStruct((n_batch, n_heads, LANES, LANES), F32)],
        scratch_shapes=[pltpu.VMEM((hb, LANES, LANES), F32), pltpu.VMEM((hb, blk, LANES), F32)],
        compiler_params=_cparams(3),
        name="hgrn_prompt",
    )(*([p] * (4 * hb)), lb_logits, norm_g.reshape(1, LANES), masks)


def _gla_sample_body(q_ref, f_ref, i_ref, g_ref, lbl_ref, ng_ref, s0_ref, *rest, layer):
    o_ref, so_ref, qd_scr, kd_scr, o_scr = rest[-5:]
    n_g, n_t, n_h, _ = q_ref.shape
    lb = _lower_bound([lbl_ref[r] for r in range(lbl_ref.shape[0])], layer)
    q = _silu(q_ref[...])
    fg = lb + (1.0 - lb) * jax.nn.sigmoid(f_ref[...])
    kk = 1.0 - fg
    lf = jnp.log(fg)
    v = i_ref[...]
    bt = [lf[:, 0]]
    for t in range(1, n_t):
        bt.append(bt[-1] + lf[:, t])
    b_last = bt[n_t - 1]
    for t in range(n_t):
        acc = None
        for s in range(t + 1):
            w = q[:, t] * kk[:, s]
            if s < t:
                w = w * jnp.exp(bt[t] - bt[s])
            term = jnp.sum(w, axis=-1, keepdims=True) * v[:, s]
            acc = term if acc is None else acc + term
        o_scr[:, t] = acc
        qd_scr[:, t] = q[:, t] * jnp.exp(bt[t])
        kd_scr[:, t] = kk[:, t] * jnp.exp(b_last - bt[t])
    e_last = jnp.exp(b_last)
    for gi in range(n_g):
        e_cols = e_last[gi].T
        for h in range(n_h):
            s = s0_ref[gi, h]
            o_scr[gi, :, h, :] += _dot(qd_scr[gi, :, h, :], s)
            so_ref[gi, h] = s * e_cols[:, h:h + 1] + _dot_tn(kd_scr[gi, :, h, :], i_ref[gi, :, h, :])
    o = o_scr[...]
    o = o * lax.rsqrt(jnp.mean(o * o, axis=-1, keepdims=True) + NORM_EPS) * ng_ref[...]
    o_ref[...] = (o * jax.nn.sigmoid(g_ref[...])).astype(o_ref.dtype)


def _gla_sample(p4, lb_logits, norm_g, s0, *, layer, prev_state, seq0):
    _, td, h4, _ = p4.shape
    bd = s0.shape[1]
    nh = h4 // 4
    g = min(GLA_SAMPLE_GROUP, bd)
    assert bd % g == 0 and seq0 % g == 0
    blk = (g, td, nh, LANES)
    sblk = (g, nh, LANES, LANES)
    lb3 = lb_logits.reshape(lb_logits.shape[0], nh, LANES)
    part = lambda off: pl.BlockSpec(blk, lambda i: (seq0 // g + i, 0, off, 0))
    in_specs = [part(0), part(1), part(2), part(3),
                pl.BlockSpec(lb3.shape, lambda i: (0, 0, 0)),
                pl.BlockSpec((1, LANES), lambda i: (0, 0)),
                pl.BlockSpec((None,) + sblk, lambda i: (layer, i, 0, 0, 0))]
    args = [p4, p4, p4, p4, lb3, norm_g.reshape(1, LANES), s0]
    so_shape, so_spec, extra, extra_specs, aliases = _stacked_out(
        s0.shape[0], layer, (bd, nh, LANES, LANES), sblk, lambda i: (i, 0, 0, 0), prev_state, len(args))
    return pl.pallas_call(
        functools.partial(_gla_sample_body, layer=layer),
        grid=(bd // g,),
        in_specs=in_specs + extra_specs,
        out_specs=[pl.BlockSpec(blk, lambda i: (i, 0, 0, 0)), so_spec],
        out_shape=[jax.ShapeDtypeStruct((bd, td, nh, LANES), BF16), so_shape],
        input_output_aliases=aliases,
        scratch_shapes=[pltpu.VMEM(blk, F32), pltpu.VMEM(blk, F32), pltpu.VMEM(blk, F32)],
        compiler_params=_cparams(1),
        name="hgrn_sample",
    )(*args, *extra)


def _prev_rows(buf, dec_seq):
    bd, n_prev, d = buf.shape
    outs = []
    for dd in range(1, n_prev + 1):
        rows = [buf[:, n_prev - dd + t] if t < dd else jnp.zeros((bd, d), buf.dtype) for t in range(dec_seq)]
        outs.append(jnp.stack(rows, axis=1).reshape(bd * dec_seq, d))
    return outs


def _first_rows(vals, dec_seq):
    bd, d = vals.shape
    z = jnp.zeros((bd, dec_seq - 1, d), vals.dtype)
    return jnp.concatenate([vals[:, None, :], z], axis=1).reshape(bd * dec_seq, d)


def kernel(x_prompt, x_sample, state_ret, state_rglru_h, state_rglru_conv, state_hgrn, state_ffn_conv,
           ev_w_in, ev_w_out, ev_rg_conv_w, ev_rg_conv_b, ev_rg_wa, ev_rg_ba, ev_rg_wx, ev_rg_bx, ev_rg_lambda,
           od_w_in, od_w_out, od_norm_g, od_lb_logits, ln_g, ln_b, ffn_w_up, ffn_conv_w, ffn_conv_b, ffn_w_down):
    bp, tp, d_model = x_prompt.shape
    bd, td, _ = x_sample.shape
    depth = ln_g.shape[0]
    m_p, m_s = bp * tp, bd * td
    m = m_p + m_s
    alpha = (2.0 * depth) ** 0.25
    h_ret, dk_ret = state_ret.shape[2], state_ret.shape[3]
    d_ret = h_ret * dk_ret
    d_rg = state_rglru_h.shape[-1]
    h_hg = state_hgrn.shape[2]
    d_ff = ffn_conv_b.shape[-1]
    assert state_hgrn.shape[3] == LANES and state_hgrn.shape[4] == LANES and d_ret == d_rg
    assert ffn_conv_w.shape[1] == 3 and td >= 3

    x = (x_prompt.reshape(m_p, d_model), x_sample.reshape(m_s, d_model))
    xb = jnp.concatenate(x, axis=0).astype(BF16)
    pos_p = jnp.arange(tp, dtype=jnp.int32)
    pos_s = PAST_LEN + jnp.arange(td, dtype=jnp.int32)
    zero_ret = jnp.zeros((1, bp) + state_ret.shape[2:], F32)
    n_even = state_ret.shape[0]
    w_out_ev = ev_w_out.astype(BF16)
    w_out_od = od_w_out.astype(BF16)
    w_down = ffn_w_down.astype(BF16)

    ret_p = ret_s = hg_s = ff_s = None
    n_h_p, n_h_s, n_cv_p, n_cv_s, n_hg_p, n_ff_p = [], [], [], [], [], []
    tm_ff = min(ROW_TILE, m_s)
    tm_ln = min(LN_ROW_TILE, m_s)
    tm_mix = min(MIX_LN_ROW_TILE, m_s)
    tm_in = min(IN_PROJ_ROW_TILE, m_s)
    for l in range(depth):
        if l % 2 == 0:
            e = l // 2
            p = _matmul(xb, ev_w_in, e, tm_in, IN_PROJ_COLS)
            chunk = RET_CHUNK if tp % RET_CHUNK == 0 else tp
            o_ret, ret_p = _retention(p, zero_ret, pos_p, layer=e, n_layers=n_even, prev_state=ret_p, row0=0,
                                      n_batch=bp, n_seq=1, chunk=chunk, n_chunks=tp // chunk, n_heads=h_ret,
                                      dk=dk_ret, heads_per_step=h_ret)
            g_ret = min(RET_SAMPLE_GROUP, bd)
            o_ret_s, ret_s = _retention(p, state_ret, pos_s, layer=e, n_layers=n_even, prev_state=ret_s,
                                        row0=m_p, n_batch=bd // g_ret, n_seq=g_ret, chunk=td, n_chunks=1,
                                        n_heads=h_ret, dk=dk_ret, heads_per_step=1)
            wa = ev_rg_wa[e].astype(BF16)
            wx = ev_rg_wx[e].astype(BF16)
            rg_args = (ev_rg_conv_w[e], ev_rg_conv_b[e], wa, wx, ev_rg_ba[e], ev_rg_bx[e], ev_rg_lambda[e])
            o_rg, hl_p, xt_p = _rg_prompt(p, *rg_args, n_batch=bp, seq=tp, col0=4 * d_ret, d_rg=d_rg,
                                          total_rows=m_p)
            e_prev = jnp.stack(_prev_rows(state_rglru_conv[e], td))
            o_rg_s, h_s = _rg_sample(p, e_prev, _first_rows(state_rglru_h[e], td), *rg_args,
                                     row0=m_p, m_s=m_s, dec_seq=td, col0=4 * d_ret, d_rg=d_rg)
            n_conv = state_rglru_conv.shape[2]
            xr_s = p[m_p:, 4 * d_ret:4 * d_ret + d_rg].reshape(bd, td, d_rg)
            n_h_p.append(hl_p[:, 0])
            n_h_s.append(h_s.reshape(bd, td, d_rg)[:, td - 1])
            n_cv_p.append(xt_p[:, SUBLANES - n_conv:])
            n_cv_s.append(xr_s[:, td - n_conv:])
            x_new, xb = _proj_ln([(o_ret, o_ret_s), (o_rg, o_rg_s)], w_out_ev, e, x, ln_g[l, 0], ln_b[l, 0],
                                 alpha, tm_mix, m_p)
        else:
            o = l // 2
            p = _matmul(xb, od_w_in, o, tm_in, IN_PROJ_COLS)
            o_hg, g_p = _gla_prompt(p, od_lb_logits, od_norm_g[o], layer=o, n_batch=bp, seq=tp,
                                    n_heads=h_hg, total_rows=m_p)
            p4 = p[m_p:].reshape(bd, td, 4 * h_hg, LANES)
            o_hg_s, hg_s = _gla_sample(p4, od_lb_logits, od_norm_g[o], state_hgrn, layer=o, prev_state=hg_s,
                                       seq0=0)
            n_hg_p.append(g_p)
            x_new, xb = _proj_ln([(o_hg, o_hg_s.reshape(m_s, h_hg * LANES))], w_out_od, o, x, ln_g[l, 0],
                                 ln_b[l, 0], alpha, tm_mix, m_p)
        x = (x_new,)
        h, tails, ff_s = _ffn_up(xb, ffn_w_up, l, ffn_conv_w[l], ffn_conv_b[l], state_ffn_conv, ff_s,
                                 m_prompt=m_p, seq=tp, dec_seq=td, tm=tm_ff, tn=512)
        tiles_per_seq = tp // tm_ff
        tails = tails.reshape(m // tm_ff, SUBLANES, d_ff)[:bp * tiles_per_seq]
        tails = tails.reshape(bp, tiles_per_seq, SUBLANES, d_ff)
        n_ff_p.append(tails[:, tiles_per_seq - 1, SUBLANES - 2:])
        last = l == depth - 1
        x_new, xb = _proj_ln([(h,)], w_down, l, x, ln_g[l, 1], ln_b[l, 1], alpha, tm_ln, m_p, split_out=last)
        x = (x_new,)

    y_prompt = x_new.reshape(bp, tp, d_model)
    y_sample = xb.reshape(bd, td, d_model)
    return (y_prompt, y_sample, ret_p, ret_s, jnp.stack(n_h_p), jnp.stack(n_h_s),
            jnp.stack(n_cv_p), jnp.stack(n_cv_s), jnp.stack(n_hg_p), hg_s,
            jnp.stack(n_ff_p), ff_s)
```

```python
import functools

import numpy as np
import jax
import jax.numpy as jnp
from jax import lax
from jax.experimental import pallas as pl
from jax.experimental.pallas import tpu as pltpu

F32 = jnp.float32
BF16 = jnp.bfloat16

LN_EPS = 1e-5
NORM_EPS = 1e-6
ROPE_BASE = 10000.0
RG_C = 8.0
TINY = 1e-37
GELU_C0 = 0.7978845608028654
GELU_C1 = 0.044715
PAST_LEN = 16384
LANES = 128
SUBLANES = 8
MXU_COLS = 256
ROW_TILE = 512
FFN_ROW_SUB = 256
LN_ROW_TILE = 256
MIX_LN_ROW_TILE = 512
IN_PROJ_ROW_TILE = 512
IN_PROJ_COLS = 2048
RET_CHUNK = 256
RET_SAMPLE_GROUP = 8
RG_TIME_TILE = 256
GLA_BLOCK = 128
GLA_HEADS_PER_STEP = 8
GLA_SAMPLE_GROUP = 4
MIB = 1024 * 1024


def _cparams(n_axes, vmem_mib=48):
    return pltpu.CompilerParams(dimension_semantics=("arbitrary",) * n_axes,
                                vmem_limit_bytes=vmem_mib * MIB)


def _dot(a, b):
    return jnp.dot(a.astype(BF16), b.astype(BF16), preferred_element_type=F32)


def _dot_nt(a, b):
    return lax.dot_general(a.astype(BF16), b.astype(BF16), (((1,), (1,)), ((), ())),
                           preferred_element_type=F32)


def _dot_tn(a, b):
    return lax.dot_general(a.astype(BF16), b.astype(BF16), (((0,), (0,)), ((), ())),
                           preferred_element_type=F32)


def _silu(x):
    return x * jax.nn.sigmoid(x)


def _mm_body(x_ref, w_ref, o_ref, wb_scr):
    @pl.when(pl.program_id(1) == 0)
    def _():
        wb_scr[...] = w_ref[...].astype(BF16)

    res = jnp.dot(x_ref[...], wb_scr[...], preferred_element_type=F32)
    if len(o_ref.shape) == 2:
        o_ref[...] = res
    else:
        for c in range(o_ref.shape[0]):
            o_ref[c] = res[:, c * LANES:(c + 1) * LANES]


def _matmul(xb, w, layer, tm, tn, column_major=False):
    m, k = xb.shape
    n = w.shape[2]
    assert m % tm == 0 and n % tn == 0 and tn % LANES == 0
    if column_major:
        out_spec = pl.BlockSpec((tn // LANES, tm, LANES), lambda j, i: (j, i, 0))
        out_shape = jax.ShapeDtypeStruct((n // LANES, m, LANES), F32)
    else:
        out_spec = pl.BlockSpec((tm, tn), lambda j, i: (i, j))
        out_shape = jax.ShapeDtypeStruct((m, n), F32)
    return pl.pallas_call(
        _mm_body,
        grid=(n // tn, m // tm),
        in_specs=[pl.BlockSpec((tm, k), lambda j, i: (i, 0)),
                  pl.BlockSpec((None, k, tn), lambda j, i: (layer, 0, j))],
        out_specs=out_spec,
        out_shape=out_shape,
        scratch_shapes=[pltpu.VMEM((k, tn), BF16)],
        compiler_params=_cparams(2, vmem_mib=56),
        name="in_proj",
    )(xb, w)


def _proj_ln_body(*refs, part_arity, res_arity, alpha, n_prompt_tiles, split_out):
    i = pl.program_id(0)
    n_in = sum(part_arity)
    a_refs = refs[:n_in]
    w_refs = refs[n_in:n_in + len(part_arity)]
    x_refs = refs[n_in + len(part_arity):][:res_arity]
    g_ref, b_ref, o1_ref, o2_ref = refs[n_in + len(part_arity) + res_arity:]

    def rows_of(group):
        if len(group) == 1:
            return group[0][...]
        return jnp.where(i < n_prompt_tiles, group[0][...], group[1][...])

    acc = None
    pos = 0
    for arity, w_ref in zip(part_arity, w_refs):
        d = jnp.dot(rows_of(a_refs[pos:pos + arity]), w_ref[...], preferred_element_type=F32)
        pos += arity
        acc = d if acc is None else acc + d
    y = alpha * rows_of(x_refs) + acc
    mu = jnp.mean(y, axis=-1, keepdims=True)
    d = y - mu
    var = jnp.mean(d * d, axis=-1, keepdims=True)
    out = d * lax.rsqrt(var + LN_EPS) * g_ref[...] + b_ref[...]
    if split_out:
        @pl.when(i < n_prompt_tiles)
        def _():
            o1_ref[...] = out

        @pl.when(i >= n_prompt_tiles)
        def _():
            o2_ref[...] = out
    else:
        o1_ref[...] = out
        o2_ref[...] = out.astype(BF16)


def _proj_ln(parts, wb, layer, x, g, b, alpha, tm, m_prompt, split_out=False):
    m = sum(a.shape[0] for a in x)
    d = x[0].shape[1]
    assert m % tm == 0 and m_prompt % tm == 0
    npt = m_prompt // tm
    kp = parts[0][0].shape[1]
    assert all(a.shape[1] == kp for p in parts for a in p) and wb.shape[1] == kp * len(parts)
    prompt_rows = lambda i: (jnp.minimum(i, npt - 1), 0)
    sample_rows = lambda i: (jnp.maximum(i - npt, 0), 0)

    def row_specs(group, width):
        if len(group) == 1:
            return [pl.BlockSpec((tm, width), lambda i: (i, 0))]
        return [pl.BlockSpec((tm, width), prompt_rows), pl.BlockSpec((tm, width), sample_rows)]

    in_specs, args = [], []
    for p in parts:
        args += list(p)
        in_specs += row_specs(p, kp)
    for part in range(len(parts)):
        in_specs.append(pl.BlockSpec((None, kp, d), lambda i, part=part: (layer, part, 0),
                                     pipeline_mode=pl.Buffered(1)))
    in_specs += row_specs(x, d)
    in_specs += [pl.BlockSpec((1, d), lambda i: (0, 0)),
                 pl.BlockSpec((1, d), lambda i: (0, 0))]
    if split_out:
        out_specs = [pl.BlockSpec((tm, d), prompt_rows), pl.BlockSpec((tm, d), sample_rows)]
        out_shape = [jax.ShapeDtypeStruct((m_prompt, d), F32), jax.ShapeDtypeStruct((m - m_prompt, d), F32)]
    else:
        out_specs = [pl.BlockSpec((tm, d), lambda i: (i, 0)), pl.BlockSpec((tm, d), lambda i: (i, 0))]
        out_shape = [jax.ShapeDtypeStruct((m, d), F32), jax.ShapeDtypeStruct((m, d), BF16)]
    return pl.pallas_call(
        functools.partial(_proj_ln_body, part_arity=tuple(len(p) for p in parts), res_arity=len(x), alpha=alpha,
                          n_prompt_tiles=npt, split_out=split_out),
        grid=(m // tm,),
        in_specs=in_specs,
        out_specs=out_specs,
        out_shape=out_shape,
        compiler_params=_cparams(1, vmem_mib=56),
        name="proj_ln",
    )(*args, *([wb] * len(parts)), *x, g.reshape(1, d), b.reshape(1, d))


def _ffn_up_body(x_ref, wu_ref, wv_ref, cw_ref, cb_ref, buf_ref, *rest,
                 n_prompt_tiles, tiles_per_seq, dec_seq):
    h_ref, tail_ref, fs_ref, wub_scr, wvb_scr, carry_ref, e1_scr, e2_scr, us_scr, uh_scr = rest[-10:]
    i = pl.program_id(1)

    @pl.when(i == 0)
    def _():
        wub_scr[...] = wu_ref[...].astype(BF16)
        wvb_scr[...] = wv_ref[...].astype(BF16)

    tm = x_ref.shape[0]
    tn = h_ref.shape[1]
    cw = min(MXU_COLS, tn)
    rs = min(FFN_ROW_SUB, tm)
    lane_blocks = cw // LANES
    row = lax.broadcasted_iota(jnp.int32, (rs, cw), 0)

    def dots(c, r):
        cols = slice(c * cw, (c + 1) * cw)
        x = x_ref[r * rs:(r + 1) * rs, :]
        return (jnp.dot(x, wub_scr[:, cols], preferred_element_type=F32),
                jnp.dot(x, wvb_scr[:, cols], preferred_element_type=F32))

    def epilogue(c, r, u, u1, u2, v):
        cols = slice(c * cw, (c + 1) * cw)
        uc = u2 * cw_ref[0:1, cols] + cb_ref[:, cols]
        uc = uc + u1 * cw_ref[1:2, cols]
        uc = uc + u * cw_ref[2:3, cols]
        th = jnp.tanh(uc * (GELU_C0 + (GELU_C0 * GELU_C1) * (uc * uc)))
        h_ref[r * rs:(r + 1) * rs, cols] = ((uc + uc * th) * (0.5 * v)).astype(h_ref.dtype)

    @pl.when(i < n_prompt_tiles)
    def _():
        first = i % tiles_per_seq == 0
        n_r = tm // rs
        for c in range(tn // cw):
            cols = slice(c * cw, (c + 1) * cw)
            prev = jnp.where(first, 0.0, carry_ref[:, cols])
            for r in range(n_r):
                slot = (c * n_r + r) % 2
                u, v = dots(c, r)
                uh_scr[slot, 0:SUBLANES, :] = prev
                uh_scr[slot, SUBLANES:, :] = u
                u1 = uh_scr[slot, SUBLANES - 1:SUBLANES - 1 + rs, :]
                u2 = uh_scr[slot, SUBLANES - 2:SUBLANES - 2 + rs, :]
                epilogue(c, r, u, u1, u2, v)
                prev = u[rs - SUBLANES:]
            carry_ref[:, cols] = prev
            tail_ref[:, cols] = prev

    @pl.when(i >= n_prompt_tiles)
    def _():
        t = row % dec_seq
        n_seq = tm // dec_seq
        e1_scr[...] = jnp.zeros_like(e1_scr)
        e2_scr[...] = jnp.zeros_like(e2_scr)
        for lb in range(tn // LANES):
            lanes = slice(lb * LANES, (lb + 1) * LANES)
            e1_scr[lb, pl.ds(0, n_seq, stride=dec_seq), :] = buf_ref[:, 1, lanes]
            e2_scr[lb, pl.ds(0, n_seq, stride=dec_seq), :] = buf_ref[:, 0, lanes]
            e2_scr[lb, pl.ds(1, n_seq, stride=dec_seq), :] = buf_ref[:, 1, lanes]
        for c in range(tn // cw):
            for r in range(tm // rs):
                rows = slice(r * rs, (r + 1) * rs)
                lbs = range(c * lane_blocks, (c + 1) * lane_blocks)
                e1 = jnp.concatenate([e1_scr[lb, rows, :] for lb in lbs], axis=1)
                e2 = jnp.concatenate([e2_scr[lb, rows, :] for lb in lbs], axis=1)
                u, v = dots(c, r)
                epilogue(c, r, u, jnp.where(t >= 1, pltpu.roll(u, 1, axis=0), e1),
                         jnp.where(t >= 2, pltpu.roll(u, 2, axis=0), e2), v)
                for k, lb in enumerate(lbs):
                    us_scr[lb, rows, :] = u[:, k * LANES:(k + 1) * LANES]
        for lb in range(tn // LANES):
            lanes = slice(lb * LANES, (lb + 1) * LANES)
            fs_ref[:, 0, lanes] = us_scr[lb, pl.ds(dec_seq - 2, n_seq, stride=dec_seq), :]
            fs_ref[:, 1, lanes] = us_scr[lb, pl.ds(dec_seq - 1, n_seq, stride=dec_seq), :]
        tail_ref[...] = jnp.zeros_like(tail_ref)


def _ffn_up(xb, w_up, layer, cw, cb, conv_state, prev_state, *, m_prompt, seq, dec_seq, tm, tn):
    m, k = xb.shape
    d_ff = w_up.shape[2] // 2
    n_layers, bd = conv_state.shape[:2]
    m_s = m - m_prompt
    assert m_prompt % tm == 0 and m_s % tm == 0 and seq % tm == 0 and d_ff % tn == 0
    assert tm % dec_seq == 0 and dec_seq >= 2 and cw.shape[0] == 3 and conv_state.shape[2] == 2
    assert tn % min(MXU_COLS, tn) == 0 and tm % min(FFN_ROW_SUB, tm) == 0 and m_s == bd * dec_seq
    npt = m_prompt // tm
    nj = d_ff // tn
    n_seq = tm // dec_seq
    in_specs = [pl.BlockSpec((tm, k), lambda j, i: (i, 0)),
                pl.BlockSpec((None, k, tn), lambda j, i: (layer, 0, j)),
                pl.BlockSpec((None, k, tn), lambda j, i: (layer, 0, j + nj)),
                pl.BlockSpec((3, tn), lambda j, i: (0, j)),
                pl.BlockSpec((1, tn), lambda j, i: (0, j)),
                pl.BlockSpec((None, n_seq, 2, tn), lambda j, i: (layer, jnp.maximum(i - npt, 0), 0, j))]
    args = [xb, w_up, w_up, cw, cb.reshape(1, d_ff), conv_state]
    fs_shape, fs_spec, extra, extra_specs, aliases = _stacked_out(
        n_layers, layer, (bd, 2, d_ff), (n_seq, 2, tn), lambda j, i: (jnp.maximum(i - npt, 0), 0, j),
        prev_state, len(args), out_index=2)
    n_lb = tn // LANES
    return pl.pallas_call(
        functools.partial(_ffn_up_body, n_prompt_tiles=npt, tiles_per_seq=seq // tm, dec_seq=dec_seq),
        grid=(nj, m // tm),
        in_specs=in_specs + extra_specs,
        out_specs=[pl.BlockSpec((tm, tn), lambda j, i: (i, j)),
                   pl.BlockSpec((SUBLANES, tn), lambda j, i: (i, j)),
                   fs_spec],
        out_shape=[jax.ShapeDtypeStruct((m, d_ff), BF16),
                   jax.ShapeDtypeStruct((m // tm * SUBLANES, d_ff), F32),
                   fs_shape],
        input_output_aliases=aliases,
        scratch_shapes=[pltpu.VMEM((k, tn), BF16), pltpu.VMEM((k, tn), BF16), pltpu.VMEM((SUBLANES, tn), F32),
                        pltpu.VMEM((n_lb, tm, LANES), F32), pltpu.VMEM((n_lb, tm, LANES), F32),
                        pltpu.VMEM((n_lb, tm, LANES), F32),
                        pltpu.VMEM((2, min(FFN_ROW_SUB, tm) + SUBLANES, min(MXU_COLS, tn)), F32)],
        compiler_params=_cparams(2),
        name="ffn_up",
    )(*args, *extra)


def _ret_body(q_ref, k_ref, v_ref, g_ref, cos_ref, sin_ref, dm_ref, xi_ref, zeta_ref, cdec_ref, s0_ref,
              *rest, n_seq, chunk, n_chunks, scale):
    o_ref, so_ref, s_scr, q_scr, kz_scr, o_scr = rest[-6:]
    n = pl.program_id(2)

    @pl.when(n == 0)
    def _():
        s_scr[...] = s0_ref[...]

    n_heads = dm_ref.shape[0]
    dk = q_ref.shape[1] // n_heads
    half = dk // 2
    cos = cos_ref[...]
    sin = sin_ref[...]

    def rot(x):
        x1 = x[:, :half]
        x2 = x[:, half:]
        return jnp.concatenate([x1 * cos - x2 * sin, x2 * cos + x1 * sin], axis=-1)

    for hh in range(n_heads):
        cols = slice(hh * dk, (hh + 1) * dk)
        q = rot(q_ref[:, cols])
        k = rot(k_ref[:, cols]) * scale
        vb = v_ref[:, cols].astype(BF16)
        inner = _dot_nt(q, k) * dm_ref[hh]
        o_scr[hh] = _dot(inner, vb)
        q_scr[hh] = q
        kz_scr[hh] = k * zeta_ref[hh]
        cdec = cdec_ref[hh]
        for gi in range(n_seq):
            rows = pl.ds(gi * chunk, chunk)
            s = s_scr[gi, hh]
            o_scr[hh, rows, :] += _dot(q_scr[hh, rows, :], s) * xi_ref[hh, rows, :]
            s_scr[gi, hh] = s * cdec + _dot_tn(kz_scr[hh, rows, :], v_ref[rows, cols])
        o = o_scr[hh]
        gate = g_ref[:, cols]
        o = o * lax.rsqrt(jnp.mean(o * o, axis=-1, keepdims=True) + NORM_EPS) * _silu(gate)
        o_ref[:, cols] = o.astype(o_ref.dtype)

    @pl.when(n == n_chunks - 1)
    def _():
        so_ref[...] = s_scr[...]


def _ret_tables(n_heads, chunk, n_seq):
    lg = jnp.log(1.0 - 2.0 ** (-5.0 - jnp.arange(n_heads, dtype=F32)))
    idx = jnp.arange(chunk, dtype=F32)
    diff = idx[:, None] - idx[None, :]
    dmask = jnp.where(diff[None] >= 0, jnp.exp(jnp.maximum(diff, 0.0)[None] * lg[:, None, None]), 0.0)
    eye = jnp.eye(n_seq, dtype=F32)
    dm = jnp.einsum('ab,hts->hatbs', eye, dmask).reshape(n_heads, n_seq * chunk, n_seq * chunk)
    xi = jnp.tile(jnp.exp((idx[None, :] + 1.0) * lg[:, None]), (1, n_seq))[:, :, None]
    zeta = jnp.tile(jnp.exp((chunk - 1.0 - idx[None, :]) * lg[:, None]), (1, n_seq))[:, :, None]
    cdec = jnp.exp(chunk * lg)[:, None, None]
    return dm, xi, zeta, cdec


def _rope_tables(pos, half):
    inv = ROPE_BASE ** (-jnp.arange(half, dtype=F32) / half)
    ang = pos.astype(F32)[:, None] * inv[None, :]
    return jnp.cos(ang), jnp.sin(ang)


def _stacked_out(n_layers, layer, shape, block, index_map, prev, n_inputs, out_index=1):
    sds = jax.ShapeDtypeStruct((n_layers,) + tuple(shape), F32)
    spec = pl.BlockSpec((None,) + tuple(block), lambda *g: (layer,) + tuple(index_map(*g)))
    if prev is None:
        return sds, spec, [], [], {}
    return sds, spec, [prev], [pl.BlockSpec(memory_space=pl.ANY)], {n_inputs: out_index}


def _retention(p, s0, pos, *, layer, n_layers, prev_state, row0, n_batch, n_seq, chunk, n_chunks, n_heads, dk,
               heads_per_step):
    rows = n_seq * chunk
    assert row0 % rows == 0
    rb0 = row0 // rows
    dm, xi, zeta, cdec = _ret_tables(n_heads, chunk, n_seq)
    cos, sin = _rope_tables(pos, dk // 2)
    if n_seq > 1:
        assert n_chunks == 1
        cos = jnp.tile(cos, (n_seq, 1))
        sin = jnp.tile(sin, (n_seq, 1))
    hps = heads_per_step
    assert n_heads % hps == 0
    hg = n_heads // hps
    w = hps * dk

    def col(off):
        return lambda b, h, n: (rb0 + b * n_chunks + n, off * hg + h)

    tab = lambda b, h, n: (h, 0, 0)
    s0_layer = layer if s0.shape[0] > 1 else 0
    in_specs = [pl.BlockSpec((rows, w), col(0)), pl.BlockSpec((rows, w), col(1)),
                pl.BlockSpec((rows, w), col(2)), pl.BlockSpec((rows, w), col(3)),
                pl.BlockSpec((rows, dk // 2), lambda b, h, n: (n, 0)),
                pl.BlockSpec((rows, dk // 2), lambda b, h, n: (n, 0)),
                pl.BlockSpec((hps, rows, rows), tab),
                pl.BlockSpec((hps, rows, 1), tab),
                pl.BlockSpec((hps, rows, 1), tab),
                pl.BlockSpec((hps, 1, 1), tab),
                pl.BlockSpec((None, n_seq, hps, dk, dk), lambda b, h, n: (s0_layer, b, h, 0, 0))]
    args = [p, p, p, p, cos, sin, dm, xi, zeta, cdec, s0]
    so_shape, so_spec, extra, extra_specs, aliases = _stacked_out(
        n_layers, layer, (n_batch * n_seq, n_heads, dk, dk), (n_seq, hps, dk, dk),
        lambda b, h, n: (b, h, 0, 0), prev_state, len(args))
    total_rows = n_batch * n_chunks * rows
    return pl.pallas_call(
        functools.partial(_ret_body, n_seq=n_seq, chunk=chunk, n_chunks=n_chunks, scale=dk ** -0.5),
        grid=(n_batch, hg, n_chunks),
        in_specs=in_specs + extra_specs,
        out_specs=[pl.BlockSpec((rows, w), lambda b, h, n: (b * n_chunks + n, h)), so_spec],
        out_shape=[jax.ShapeDtypeStruct((total_rows, n_heads * dk), BF16), so_shape],
        input_output_aliases=aliases,
        scratch_shapes=[pltpu.VMEM((n_seq, hps, dk, dk), F32), pltpu.VMEM((hps, rows, dk), F32),
                        pltpu.VMEM((hps, rows, dk), F32), pltpu.VMEM((hps, rows, dk), F32)],
        compiler_params=_cparams(3),
        name="retention",
    )(*args, *extra)


def _rg_gates(xc, wa_ref, wx_ref, ba, bx, lam):
    nb, bw, _ = wa_ref.shape
    ra, ia = [], []
    for blk in range(nb):
        xb = xc[:, blk * bw:(blk + 1) * bw].astype(BF16)
        ra.append(jnp.dot(xb, wa_ref[blk], preferred_element_type=F32))
        ia.append(jnp.dot(xb, wx_ref[blk], preferred_element_type=F32))
    r = jax.nn.sigmoid(jnp.concatenate(ra, axis=-1) + ba)
    i = jax.nn.sigmoid(jnp.concatenate(ia, axis=-1) + bx)
    log_a = -RG_C * r * jax.nn.softplus(-lam)
    a = jnp.exp(log_a)
    y = -jnp.tanh(log_a) * (a * a + 1.0)
    mult = y * lax.rsqrt(jnp.maximum(y, TINY))
    return a, mult, i


def _scan_rows(a, b, row_in_seg, seg_len):
    s = 1
    while s < seg_len:
        a_sh = pltpu.roll(a, s, axis=0)
        b_sh = pltpu.roll(b, s, axis=0)
        m = row_in_seg >= s
        b = jnp.where(m, a * b_sh + b, b)
        a = jnp.where(m, a * a_sh, a)
        s *= 2
    return a, b


def _rg_prompt_body(xr_ref, gr_ref, cw_ref, cb_ref, wa_ref, wx_ref, ba_ref, bx_ref, lam_ref,
                    o_ref, hl_ref, xt_ref, h_scr, x_scr, *, n_tiles):
    n = pl.program_id(1)

    @pl.when(n == 0)
    def _():
        h_scr[...] = jnp.zeros_like(h_scr)
        x_scr[...] = jnp.zeros_like(x_scr)

    xr = xr_ref[...]
    tc = xr.shape[0]
    row = lax.broadcasted_iota(jnp.int32, xr.shape, 0)
    row8 = lax.broadcasted_iota(jnp.int32, (SUBLANES, xr.shape[1]), 0)
    prev = x_scr[...]
    n_tap = cw_ref.shape[0]
    xc = None
    for j in range(n_tap):
        d = n_tap - 1 - j
        if d == 0:
            xd = xr
        else:
            rolled = pltpu.roll(xr, d, axis=0)
            head = jnp.where(row8 < d, pltpu.roll(prev, d, axis=0), rolled[:SUBLANES])
            xd = jnp.concatenate([head, rolled[SUBLANES:]], axis=0)
        term = xd * cw_ref[j:j + 1, :]
        xc = term + cb_ref[...] if xc is None else xc + term
    a, mult, gate_i = _rg_gates(xc, wa_ref, wx_ref, ba_ref[...], bx_ref[...], lam_ref[...])
    mult = jnp.where(row + n * tc == 0, 1.0, mult)
    bterm = xc * gate_i * mult
    a_grp, b_grp = _scan_rows(a, bterm, row % SUBLANES, SUBLANES)
    h_prev = h_scr[...]
    groups = []
    for gi in range(tc // SUBLANES):
        rows = slice(gi * SUBLANES, (gi + 1) * SUBLANES)
        h_g = a_grp[rows] * h_prev + b_grp[rows]
        groups.append(h_g)
        h_prev = h_g[SUBLANES - 1:SUBLANES]
    h = jnp.concatenate(groups, axis=0)
    o_ref[...] = (h * jax.nn.gelu(gr_ref[...])).astype(o_ref.dtype)
    h_scr[...] = h[tc - 1:tc, :]
    x_scr[...] = xr[tc - SUBLANES:, :]

    @pl.when(n == n_tiles - 1)
    def _():
        hl_ref[...] = h[tc - 1:tc, :]
        xt_ref[...] = xr[tc - SUBLANES:, :]


def _rg_prompt(p, cw, cb, wa, wx, ba, bx, lam, *, n_batch, seq, col0, d_rg, total_rows):
    tc = min(RG_TIME_TILE, seq)
    assert seq % tc == 0 and col0 % d_rg == 0 and (tc & (tc - 1)) == 0
    nt = seq // tc
    cb0 = col0 // d_rg
    vec = lambda a: a.reshape(1, d_rg)
    full = lambda a: pl.BlockSpec(a.shape, lambda b, n: (0,) * a.ndim)
    args = (cw, vec(cb), wa, wx, vec(ba), vec(bx), vec(lam))
    return pl.pallas_call(
        functools.partial(_rg_prompt_body, n_tiles=nt),
        grid=(n_batch, nt),
        in_specs=[pl.BlockSpec((tc, d_rg), lambda b, n: (b * nt + n, cb0)),
                  pl.BlockSpec((tc, d_rg), lambda b, n: (b * nt + n, cb0 + 1))] + [full(a) for a in args],
        out_specs=[pl.BlockSpec((tc, d_rg), lambda b, n: (b * nt + n, 0)),
                   pl.BlockSpec((None, 1, d_rg), lambda b, n: (b, 0, 0)),
                   pl.BlockSpec((None, SUBLANES, d_rg), lambda b, n: (b, 0, 0))],
        out_shape=[jax.ShapeDtypeStruct((total_rows, d_rg), BF16),
                   jax.ShapeDtypeStruct((n_batch, 1, d_rg), F32),
                   jax.ShapeDtypeStruct((n_batch, SUBLANES, d_rg), F32)],
        scratch_shapes=[pltpu.VMEM((1, d_rg), F32), pltpu.VMEM((SUBLANES, d_rg), F32)],
        compiler_params=_cparams(2),
        name="rglru_prompt",
    )(p, p, *args)


def _rg_sample_body(xr_ref, gr_ref, e_ref, h0_ref, cw_ref, cb_ref, wa_ref, wx_ref, ba_ref, bx_ref, lam_ref,
                    o_ref, h_ref, *, dec_seq, first_pos):
    xr = xr_ref[...]
    row = lax.broadcasted_iota(jnp.int32, xr.shape, 0)
    t = row % dec_seq
    n_tap = cw_ref.shape[0]
    xc = None
    for j in range(n_tap):
        d = n_tap - 1 - j
        xd = xr if d == 0 else jnp.where(t >= d, pltpu.roll(xr, d, axis=0), e_ref[d - 1])
        term = xd * cw_ref[j:j + 1, :]
        xc = term + cb_ref[...] if xc is None else xc + term
    a, mult, gate_i = _rg_gates(xc, wa_ref, wx_ref, ba_ref[...], bx_ref[...], lam_ref[...])
    if first_pos == 0:
        mult = jnp.where(t == 0, 1.0, mult)
    bterm = xc * gate_i * mult + jnp.where(t == 0, a * h0_ref[...], 0.0)
    _, h = _scan_rows(a, bterm, t, dec_seq)
    h_ref[...] = h
    o_ref[...] = (h * jax.nn.gelu(gr_ref[...])).astype(o_ref.dtype)


def _rg_sample(p, e_prev, h0_rows, cw, cb, wa, wx, ba, bx, lam, *, row0, m_s, dec_seq, col0, d_rg):
    tr = min(128, m_s)
    assert m_s % tr == 0 and row0 % tr == 0 and tr % dec_seq == 0 and (dec_seq & (dec_seq - 1)) == 0
    rb0 = row0 // tr
    cb0 = col0 // d_rg
    vec = lambda a: a.reshape(1, d_rg)
    full = lambda a: pl.BlockSpec(a.shape, lambda i: (0,) * a.ndim)
    args = (cw, vec(cb), wa, wx, vec(ba), vec(bx), vec(lam))
    n_prev = e_prev.shape[0]
    return pl.pallas_call(
        functools.partial(_rg_sample_body, dec_seq=dec_seq, first_pos=PAST_LEN),
        grid=(m_s // tr,),
        in_specs=[pl.BlockSpec((tr, d_rg), lambda i: (rb0 + i, cb0)),
                  pl.BlockSpec((tr, d_rg), lambda i: (rb0 + i, cb0 + 1)),
                  pl.BlockSpec((n_prev, tr, d_rg), lambda i: (0, i, 0)),
                  pl.BlockSpec((tr, d_rg), lambda i: (i, 0))] + [full(a) for a in args],
        out_specs=[pl.BlockSpec((tr, d_rg), lambda i: (i, 0)), pl.BlockSpec((tr, d_rg), lambda i: (i, 0))],
        out_shape=[jax.ShapeDtypeStruct((m_s, d_rg), BF16), jax.ShapeDtypeStruct((m_s, d_rg), F32)],
        compiler_params=_cparams(1),
        name="rglru_sample",
    )(p, p, e_prev, h0_rows, *args)


def _lower_bound(logit_rows, layer):
    m = logit_rows[0]
    for z in logit_rows[1:]:
        m = jnp.maximum(m, z)
    e = [jnp.exp(z - m) for z in logit_rows]
    tot = e[0]
    for x in e[1:]:
        tot = tot + x
    if layer == 0:
        return jnp.zeros_like(tot)
    num = e[1]
    for x in e[2:layer + 1]:
        num = num + x
    return num / tot


_GLA_VREGS = GLA_BLOCK // SUBLANES


def _gla_level_masks():
    r = np.arange(GLA_BLOCK)
    t = r // SUBLANES + _GLA_VREGS * (r % SUBLANES)
    tq, tk = t[:, None], t[None, :]
    n_levels = GLA_BLOCK.bit_length()
    masks = np.zeros((n_levels, GLA_BLOCK, GLA_BLOCK), np.float32)
    masks[0] = tq == tk
    for lvl in range(1, n_levels):
        gs = 1 << lvl
        masks[lvl] = (tq > tk) & (tq // gs == tk // gs) & (tq // (gs // 2) != tk // (gs // 2))
    return masks


def _gla_prompt_body(q_ref, f_ref, i_ref, g_ref, lbl_ref, ng_ref, msk_ref, o_ref, so_ref, st_scr, o_scr,
                     *, heads, layer, n_blocks):
    n = pl.program_id(2)

    @pl.when(n == 0)
    def _():
        st_scr[...] = jnp.zeros_like(st_scr)

    nv = _GLA_VREGS
    sub = lax.broadcasted_iota(jnp.int32, (SUBLANES, LANES), 0)
    zero = jnp.zeros((SUBLANES, LANES), F32)
    for hh in range(heads):
        cols = slice(hh * LANES, (hh + 1) * LANES)

        def load(ref):
            return jnp.concatenate([ref[hh, pl.ds(j, SUBLANES, stride=nv), :] for j in range(nv)], axis=0)

        def groups(x):
            return [x[SUBLANES * j:SUBLANES * (j + 1)] for j in range(nv)]

        lb = _lower_bound([lbl_ref[r:r + 1, cols] for r in range(lbl_ref.shape[0])], layer)
        q = _silu(load(q_ref))
        fg = lb + (1.0 - lb) * jax.nn.sigmoid(load(f_ref))
        kk = 1.0 - fg
        v = load(i_ref)
        vb = v.astype(BF16)
        lf = groups(jnp.log(fg))

        c = [lf[0]]
        for j in range(1, nv):
            c.append(c[-1] + lf[j])
        tot = c[nv - 1]
        x = tot
        s = 1
        while s < SUBLANES:
            x = x + jnp.where(sub >= s, pltpu.roll(x, s, axis=0), 0.0)
            s *= 2
        before = x - tot
        bj = [cj + before for cj in c]
        b = jnp.concatenate(bj, axis=0)
        b_last = bj[nv - 1][SUBLANES - 1:SUBLANES, :]

        qj = groups(q)
        kj = groups(kk)
        def keep(lvl, scores, acc):
            return jnp.where(msk_ref[lvl] > 0, scores, acc)

        a_mat = keep(0, _dot_nt(q, kk), jnp.zeros((GLA_BLOCK, GLA_BLOCK), F32))
        lvl = 1
        gs = 2
        while gs <= nv:
            hs = gs // 2
            qd, kd = [], []
            for j in range(nv):
                ref = (j // gs) * gs + hs - 1
                if j % gs >= hs:
                    qd.append(qj[j] * jnp.exp(bj[j] - bj[ref]))
                    kd.append(zero)
                else:
                    kd.append(kj[j] if j == ref else kj[j] * jnp.exp(bj[ref] - bj[j]))
                    qd.append(zero)
            a_mat = keep(lvl, _dot_nt(jnp.concatenate(qd, axis=0), jnp.concatenate(kd, axis=0)), a_mat)
            lvl += 1
            gs *= 2
        m = 2
        while m <= SUBLANES:
            src = sub - sub % m + (m // 2 - 1)
            ref = zero
            for s_src in range(m // 2 - 1, SUBLANES, m):
                row_b = jnp.broadcast_to(bj[nv - 1][s_src:s_src + 1, :], (SUBLANES, LANES))
                ref = jnp.where(src == s_src, row_b, ref)
            e = [jnp.exp(-jnp.abs(bj[j] - ref)) for j in range(nv)]
            qd = jnp.concatenate([qj[j] * e[j] for j in range(nv)], axis=0)
            kd = jnp.concatenate([kj[j] * e[j] for j in range(nv)], axis=0)
            a_mat = keep(lvl, _dot_nt(qd, kd), a_mat)
            lvl += 1
            m *= 2

        st = st_scr[hh]
        o = _dot(a_mat, vb) + _dot_nt(q * jnp.exp(b), st)
        st_new = st * jnp.exp(b_last) + _dot_tn(vb, kk * jnp.exp(b_last - b))
        st_scr[hh] = st_new

        o = o * lax.rsqrt(jnp.mean(o * o, axis=-1, keepdims=True) + NORM_EPS) * ng_ref[...]
        o = o * jax.nn.sigmoid(load(g_ref))
        for j in range(nv):
            o_scr[hh, pl.ds(j, SUBLANES, stride=nv), :] = o[SUBLANES * j:SUBLANES * (j + 1)]
        o_ref[:, cols] = o_scr[hh].astype(o_ref.dtype)

    @pl.when(n == n_blocks - 1)
    def _():
        for hh in range(heads):
            so_ref[hh] = st_scr[hh].T


def _gla_prompt(p3, lb_logits, norm_g, *, layer, n_batch, seq, n_heads, total_rows):
    blk = GLA_BLOCK
    hb = GLA_HEADS_PER_STEP
    assert seq % blk == 0 and n_heads % hb == 0
    nb = seq // blk
    hg = n_heads // hb
    w = hb * LANES
    masks = jnp.asarray(_gla_level_masks())

    def part_spec(part):
        return pl.BlockSpec((hb, blk, LANES), lambda b, h, n: (part * hg + h, b * nb + n, 0))

    return pl.pallas_call(
        functools.partial(_gla_prompt_body, heads=hb, layer=layer, n_blocks=nb),
        grid=(n_batch, hg, nb),
        in_specs=[part_spec(part) for part in range(4)] + [
                  pl.BlockSpec((lb_logits.shape[0], w), lambda b, h, n: (0, h)),
                  pl.BlockSpec((1, LANES), lambda b, h, n: (0, 0)),
                  pl.BlockSpec(masks.shape, lambda b, h, n: (0, 0, 0))],
        out_specs=[pl.BlockSpec((blk, w), lambda b, h, n: (b * nb + n, h)),
                   pl.BlockSpec((None, hb, LANES, LANES), lambda b, h, n: (b, h, 0, 0))],
        out_shape=[jax.ShapeDtypeStruct((total_rows, n_heads * LANES), BF16),
                   jax.ShapeDtypeStruct((n_batch, n_heads, LANES, LANES), F32)],
        scratch_shapes=[pltpu.VMEM((hb, LANES, LANES), F32), pltpu.VMEM((hb, blk, LANES), F32)],
        compiler_params=_cparams(3),
        name="hgrn_prompt",
    )(p3, p3, p3, p3, lb_logits, norm_g.reshape(1, LANES), masks)


def _gla_sample_body(q_ref, f_ref, i_ref, g_ref, lbl_ref, ng_ref, s0_ref, *rest, layer):
    o_ref, so_ref, qd_scr, kd_scr, o_scr = rest[-5:]
    n_g, n_t, n_h, _ = q_ref.shape
    lb = _lower_bound([lbl_ref[r] for r in range(lbl_ref.shape[0])], layer)
    q = _silu(q_ref[...])
    fg = lb + (1.0 - lb) * jax.nn.sigmoid(f_ref[...])
    kk = 1.0 - fg
    lf = jnp.log(fg)
    v = i_ref[...]
    bt = [lf[:, 0]]
    for t in range(1, n_t):
        bt.append(bt[-1] + lf[:, t])
    b_last = bt[n_t - 1]
    for t in range(n_t):
        acc = None
        for s in range(t + 1):
            w = q[:, t] * kk[:, s]
            if s < t:
                w = w * jnp.exp(bt[t] - bt[s])
            term = jnp.sum(w, axis=-1, keepdims=True) * v[:, s]
            acc = term if acc is None else acc + term
        o_scr[:, t] = acc
        qd_scr[:, t] = q[:, t] * jnp.exp(bt[t])
        kd_scr[:, t] = kk[:, t] * jnp.exp(b_last - bt[t])
    e_last = jnp.exp(b_last)
    for gi in range(n_g):
        e_cols = e_last[gi].T
        for h in range(n_h):
            s = s0_ref[gi, h]
            o_scr[gi, :, h, :] += _dot(qd_scr[gi, :, h, :], s)
            so_ref[gi, h] = s * e_cols[:, h:h + 1] + _dot_tn(kd_scr[gi, :, h, :], i_ref[gi, :, h, :])
    o = o_scr[...]
    o = o * lax.rsqrt(jnp.mean(o * o, axis=-1, keepdims=True) + NORM_EPS) * ng_ref[...]
    o_ref[...] = (o * jax.nn.sigmoid(g_ref[...])).astype(o_ref.dtype)


def _gla_sample(p4, lb_logits, norm_g, s0, *, layer, prev_state, seq0):
    _, td, h4, _ = p4.shape
    bd = s0.shape[1]
    nh = h4 // 4
    g = min(GLA_SAMPLE_GROUP, bd)
    assert bd % g == 0 and seq0 % g == 0
    blk = (g, td, nh, LANES)
    sblk = (g, nh, LANES, LANES)
    lb3 = lb_logits.reshape(lb_logits.shape[0], nh, LANES)
    part = lambda off: pl.BlockSpec(blk, lambda i: (seq0 // g + i, 0, off, 0))
    in_specs = [part(0), part(1), part(2), part(3),
                pl.BlockSpec(lb3.shape, lambda i: (0, 0, 0)),
                pl.BlockSpec((1, LANES), lambda i: (0, 0)),
                pl.BlockSpec((None,) + sblk, lambda i: (layer, i, 0, 0, 0))]
    args = [p4, p4, p4, p4, lb3, norm_g.reshape(1, LANES), s0]
    so_shape, so_spec, extra, extra_specs, aliases = _stacked_out(
        s0.shape[0], layer, (bd, nh, LANES, LANES), sblk, lambda i: (i, 0, 0, 0), prev_state, len(args))
    return pl.pallas_call(
        functools.partial(_gla_sample_body, layer=layer),
        grid=(bd // g,),
        in_specs=in_specs + extra_specs,
        out_specs=[pl.BlockSpec(blk, lambda i: (i, 0, 0, 0)), so_spec],
        out_shape=[jax.ShapeDtypeStruct((bd, td, nh, LANES), BF16), so_shape],
        input_output_aliases=aliases,
        scratch_shapes=[pltpu.VMEM(blk, F32), pltpu.VMEM(blk, F32), pltpu.VMEM(blk, F32)],
        compiler_params=_cparams(1),
        name="hgrn_sample",
    )(*args, *extra)


def _prev_rows(buf, dec_seq):
    bd, n_prev, d = buf.shape
    outs = []
    for dd in range(1, n_prev + 1):
        rows = [buf[:, n_prev - dd + t] if t < dd else jnp.zeros((bd, d), buf.dtype) for t in range(dec_seq)]
        outs.append(jnp.stack(rows, axis=1).reshape(bd * dec_seq, d))
    return outs


def _first_rows(vals, dec_seq):
    bd, d = vals.shape
    z = jnp.zeros((bd, dec_seq - 1, d), vals.dtype)
    return jnp.concatenate([vals[:, None, :], z], axis=1).reshape(bd * dec_seq, d)


def kernel(x_prompt, x_sample, state_ret, state_rglru_h, state_rglru_conv, state_hgrn, state_ffn_conv,
           ev_w_in, ev_w_out, ev_rg_conv_w, ev_rg_conv_b, ev_rg_wa, ev_rg_ba, ev_rg_wx, ev_rg_bx, ev_rg_lambda,
           od_w_in, od_w_out, od_norm_g, od_lb_logits, ln_g, ln_b, ffn_w_up, ffn_conv_w, ffn_conv_b, ffn_w_down):
    bp, tp, d_model = x_prompt.shape
    bd, td, _ = x_sample.shape
    depth = ln_g.shape[0]
    m_p, m_s = bp * tp, bd * td
    m = m_p + m_s
    alpha = (2.0 * depth) ** 0.25
    h_ret, dk_ret = state_ret.shape[2], state_ret.shape[3]
    d_ret = h_ret * dk_ret
    d_rg = state_rglru_h.shape[-1]
    h_hg = state_hgrn.shape[2]
    d_ff = ffn_conv_b.shape[-1]
    assert state_hgrn.shape[3] == LANES and state_hgrn.shape[4] == LANES and d_ret == d_rg
    assert ffn_conv_w.shape[1] == 3 and td >= 3

    x = (x_prompt.reshape(m_p, d_model), x_sample.reshape(m_s, d_model))
    xb = jnp.concatenate(x, axis=0).astype(BF16)
    pos_p = jnp.arange(tp, dtype=jnp.int32)
    pos_s = PAST_LEN + jnp.arange(td, dtype=jnp.int32)
    zero_ret = jnp.zeros((1, bp) + state_ret.shape[2:], F32)
    n_even = state_ret.shape[0]
    w_out_ev = ev_w_out.astype(BF16)
    w_out_od = od_w_out.astype(BF16)
    w_down = ffn_w_down.astype(BF16)

    ret_p = ret_s = hg_s = ff_s = None
    n_h_p, n_h_s, n_cv_p, n_cv_s, n_hg_p, n_ff_p = [], [], [], [], [], []
    tm_ff = min(ROW_TILE, m_s)
    tm_ln = min(LN_ROW_TILE, m_s)
    tm_mix = min(MIX_LN_ROW_TILE, m_s)
    tm_in = min(IN_PROJ_ROW_TILE, m_s)
    for l in range(depth):
        if l % 2 == 0:
            e = l // 2
            p = _matmul(xb, ev_w_in, e, tm_in, IN_PROJ_COLS)
            chunk = RET_CHUNK if tp % RET_CHUNK == 0 else tp
            o_ret, ret_p = _retention(p, zero_ret, pos_p, layer=e, n_layers=n_even, prev_state=ret_p, row0=0,
                                      n_batch=bp, n_seq=1, chunk=chunk, n_chunks=tp // chunk, n_heads=h_ret,
                                      dk=dk_ret, heads_per_step=h_ret)
            g_ret = min(RET_SAMPLE_GROUP, bd)
            o_ret_s, ret_s = _retention(p, state_ret, pos_s, layer=e, n_layers=n_even, prev_state=ret_s,
                                        row0=m_p, n_batch=bd // g_ret, n_seq=g_ret, chunk=td, n_chunks=1,
                                        n_heads=h_ret, dk=dk_ret, heads_per_step=1)
            wa = ev_rg_wa[e].astype(BF16)
            wx = ev_rg_wx[e].astype(BF16)
            rg_args = (ev_rg_conv_w[e], ev_rg_conv_b[e], wa, wx, ev_rg_ba[e], ev_rg_bx[e], ev_rg_lambda[e])
            o_rg, hl_p, xt_p = _rg_prompt(p, *rg_args, n_batch=bp, seq=tp, col0=4 * d_ret, d_rg=d_rg,
                                          total_rows=m_p)
            e_prev = jnp.stack(_prev_rows(state_rglru_conv[e], td))
            o_rg_s, h_s = _rg_sample(p, e_prev, _first_rows(state_rglru_h[e], td), *rg_args,
                                     row0=m_p, m_s=m_s, dec_seq=td, col0=4 * d_ret, d_rg=d_rg)
            n_conv = state_rglru_conv.shape[2]
            xr_s = p[m_p:, 4 * d_ret:4 * d_ret + d_rg].reshape(bd, td, d_rg)
            n_h_p.append(hl_p[:, 0])
            n_h_s.append(h_s.reshape(bd, td, d_rg)[:, td - 1])
            n_cv_p.append(xt_p[:, SUBLANES - n_conv:])
            n_cv_s.append(xr_s[:, td - n_conv:])
            x_new, xb = _proj_ln([(o_ret, o_ret_s), (o_rg, o_rg_s)], w_out_ev, e, x, ln_g[l, 0], ln_b[l, 0],
                                 alpha, tm_mix, m_p)
        else:
            o = l // 2
            p3 = _matmul(xb, od_w_in, o, tm_in, IN_PROJ_COLS, column_major=True)
            o_hg, g_p = _gla_prompt(p3, od_lb_logits, od_norm_g[o], layer=o, n_batch=bp, seq=tp,
                                    n_heads=h_hg, total_rows=m_p)
            p4 = jnp.swapaxes(p3[:, m_p:], 0, 1).reshape(bd, td, 4 * h_hg, LANES)
            o_hg_s, hg_s = _gla_sample(p4, od_lb_logits, od_norm_g[o], state_hgrn, layer=o, prev_state=hg_s,
                                       seq0=0)
            n_hg_p.append(g_p)
            x_new, xb = _proj_ln([(o_hg, o_hg_s.reshape(m_s, h_hg * LANES))], w_out_od, o, x, ln_g[l, 0],
                                 ln_b[l, 0], alpha, tm_mix, m_p)
        x = (x_new,)
        h, tails, ff_s = _ffn_up(xb, ffn_w_up, l, ffn_conv_w[l], ffn_conv_b[l], state_ffn_conv, ff_s,
                                 m_prompt=m_p, seq=tp, dec_seq=td, tm=tm_ff, tn=512)
        tiles_per_seq = tp // tm_ff
        tails = tails.reshape(m // tm_ff, SUBLANES, d_ff)[:bp * tiles_per_seq]
        tails = tails.reshape(bp, tiles_per_seq, SUBLANES, d_ff)
        n_ff_p.append(tails[:, tiles_per_seq - 1, SUBLANES - 2:])
        last = l == depth - 1
        x_new, xb = _proj_ln([(h,)], w_down, l, x, ln_g[l, 1], ln_b[l, 1], alpha, tm_ln, m_p, split_out=last)
        x = (x_new,)

    y_prompt = x_new.reshape(bp, tp, d_model)
    y_sample = xb.reshape(bd, td, d_model)
    return (y_prompt, y_sample, ret_p, ret_s, jnp.stack(n_h_p), jnp.stack(n_h_s),
            jnp.stack(n_cv_p), jnp.stack(n_cv_s), jnp.stack(n_hg_p), hg_s,
            jnp.stack(n_ff_p), ff_s)
```

```python
import functools

import numpy as np
import jax
import jax.numpy as jnp
from jax import lax
from jax.experimental import pallas as pl
from jax.experimental.pallas import tpu as pltpu

F32 = jnp.float32
BF16 = jnp.bfloat16

LN_EPS = 1e-5
NORM_EPS = 1e-6
ROPE_BASE = 10000.0
RG_C = 8.0
TINY = 1e-37
GELU_C0 = 0.7978845608028654
GELU_C1 = 0.044715
PAST_LEN = 16384
LANES = 128
SUBLANES = 8
MXU_COLS = 256
ROW_TILE = 512
FFN_ROW_SUB = 128
LN_ROW_TILE = 256
MIX_LN_ROW_TILE = 512
IN_PROJ_ROW_TILE = 512
IN_PROJ_COLS = 2048
RET_CHUNK = 256
RET_SAMPLE_GROUP = 8
RG_TIME_TILE = 256
GLA_BLOCK = 128
GLA_HEADS_PER_STEP = 8
GLA_SAMPLE_GROUP = 4
MIB = 1024 * 1024


def _cparams(n_axes, vmem_mib=48):
    return pltpu.CompilerParams(dimension_semantics=("arbitrary",) * n_axes,
                                vmem_limit_bytes=vmem_mib * MIB)


def _dot(a, b):
    return jnp.dot(a.astype(BF16), b.astype(BF16), preferred_element_type=F32)


def _dot_nt(a, b):
    return lax.dot_general(a.astype(BF16), b.astype(BF16), (((1,), (1,)), ((), ())),
                           preferred_element_type=F32)


def _dot_tn(a, b):
    return lax.dot_general(a.astype(BF16), b.astype(BF16), (((0,), (0,)), ((), ())),
                           preferred_element_type=F32)


def _silu(x):
    return x * jax.nn.sigmoid(x)


def _mm_body(x_ref, w_ref, o_ref, wb_scr):
    @pl.when(pl.program_id(1) == 0)
    def _():
        wb_scr[...] = w_ref[...].astype(BF16)

    res = jnp.dot(x_ref[...], wb_scr[...], preferred_element_type=F32)
    if len(o_ref.shape) == 2:
        o_ref[...] = res
    else:
        for c in range(o_ref.shape[0]):
            o_ref[c] = res[:, c * LANES:(c + 1) * LANES]


def _matmul(xb, w, layer, tm, tn, column_major=False):
    m, k = xb.shape
    n = w.shape[2]
    assert m % tm == 0 and n % tn == 0 and tn % LANES == 0
    if column_major:
        out_spec = pl.BlockSpec((tn // LANES, tm, LANES), lambda j, i: (j, i, 0))
        out_shape = jax.ShapeDtypeStruct((n // LANES, m, LANES), F32)
    else:
        out_spec = pl.BlockSpec((tm, tn), lambda j, i: (i, j))
        out_shape = jax.ShapeDtypeStruct((m, n), F32)
    return pl.pallas_call(
        _mm_body,
        grid=(n // tn, m // tm),
        in_specs=[pl.BlockSpec((tm, k), lambda j, i: (i, 0)),
                  pl.BlockSpec((None, k, tn), lambda j, i: (layer, 0, j))],
        out_specs=out_spec,
        out_shape=out_shape,
        scratch_shapes=[pltpu.VMEM((k, tn), BF16)],
        compiler_params=_cparams(2, vmem_mib=56),
        name="in_proj",
    )(xb, w)


def _proj_ln_body(*refs, part_arity, res_arity, alpha, n_prompt_tiles, split_out):
    i = pl.program_id(0)
    n_in = sum(part_arity)
    a_refs = refs[:n_in]
    w_refs = refs[n_in:n_in + len(part_arity)]
    x_refs = refs[n_in + len(part_arity):][:res_arity]
    g_ref, b_ref, o1_ref, o2_ref = refs[n_in + len(part_arity) + res_arity:]

    def rows_of(group):
        if len(group) == 1:
            return group[0][...]
        return jnp.where(i < n_prompt_tiles, group[0][...], group[1][...])

    acc = None
    pos = 0
    for arity, w_ref in zip(part_arity, w_refs):
        d = jnp.dot(rows_of(a_refs[pos:pos + arity]), w_ref[...], preferred_element_type=F32)
        pos += arity
        acc = d if acc is None else acc + d
    y = alpha * rows_of(x_refs) + acc
    mu = jnp.mean(y, axis=-1, keepdims=True)
    d = y - mu
    var = jnp.mean(d * d, axis=-1, keepdims=True)
    out = d * lax.rsqrt(var + LN_EPS) * g_ref[...] + b_ref[...]
    if split_out:
        @pl.when(i < n_prompt_tiles)
        def _():
            o1_ref[...] = out

        @pl.when(i >= n_prompt_tiles)
        def _():
            o2_ref[...] = out
    else:
        o1_ref[...] = out
        o2_ref[...] = out.astype(BF16)


def _proj_ln(parts, wb, layer, x, g, b, alpha, tm, m_prompt, split_out=False):
    m = sum(a.shape[0] for a in x)
    d = x[0].shape[1]
    assert m % tm == 0 and m_prompt % tm == 0
    npt = m_prompt // tm
    kp = parts[0][0].shape[1]
    assert all(a.shape[1] == kp for p in parts for a in p) and wb.shape[1] == kp * len(parts)
    prompt_rows = lambda i: (jnp.minimum(i, npt - 1), 0)
    sample_rows = lambda i: (jnp.maximum(i - npt, 0), 0)

    def row_specs(group, width):
        if len(group) == 1:
            return [pl.BlockSpec((tm, width), lambda i: (i, 0))]
        return [pl.BlockSpec((tm, width), prompt_rows), pl.BlockSpec((tm, width), sample_rows)]

    in_specs, args = [], []
    for p in parts:
        args += list(p)
        in_specs += row_specs(p, kp)
    for part in range(len(parts)):
        in_specs.append(pl.BlockSpec((None, kp, d), lambda i, part=part: (layer, part, 0),
                                     pipeline_mode=pl.Buffered(1)))
    in_specs += row_specs(x, d)
    in_specs += [pl.BlockSpec((1, d), lambda i: (0, 0)),
                 pl.BlockSpec((1, d), lambda i: (0, 0))]
    if split_out:
        out_specs = [pl.BlockSpec((tm, d), prompt_rows), pl.BlockSpec((tm, d), sample_rows)]
        out_shape = [jax.ShapeDtypeStruct((m_prompt, d), F32), jax.ShapeDtypeStruct((m - m_prompt, d), F32)]
    else:
        out_specs = [pl.BlockSpec((tm, d), lambda i: (i, 0)), pl.BlockSpec((tm, d), lambda i: (i, 0))]
        out_shape = [jax.ShapeDtypeStruct((m, d), F32), jax.ShapeDtypeStruct((m, d), BF16)]
    return pl.pallas_call(
        functools.partial(_proj_ln_body, part_arity=tuple(len(p) for p in parts), res_arity=len(x), alpha=alpha,
                          n_prompt_tiles=npt, split_out=split_out),
        grid=(m // tm,),
        in_specs=in_specs,
        out_specs=out_specs,
        out_shape=out_shape,
        compiler_params=_cparams(1, vmem_mib=56),
        name="proj_ln",
    )(*args, *([wb] * len(parts)), *x, g.reshape(1, d), b.reshape(1, d))


def _ffn_up_body(x_ref, wu_ref, wv_ref, cw_ref, cb_ref, buf_ref, *rest,
                 n_prompt_tiles, tiles_per_seq, dec_seq):
    h_ref, tail_ref, fs_ref, wub_scr, wvb_scr, carry_ref, e1_scr, e2_scr, us_scr, uh_scr = rest[-10:]
    i = pl.program_id(1)

    @pl.when(i == 0)
    def _():
        wub_scr[...] = wu_ref[...].astype(BF16)
        wvb_scr[...] = wv_ref[...].astype(BF16)

    tm = x_ref.shape[0]
    tn = h_ref.shape[1]
    cw = min(MXU_COLS, tn)
    rs = min(FFN_ROW_SUB, tm)
    lane_blocks = cw // LANES
    row = lax.broadcasted_iota(jnp.int32, (rs, cw), 0)

    def dots(c, r):
        cols = slice(c * cw, (c + 1) * cw)
        x = x_ref[r * rs:(r + 1) * rs, :]
        return (jnp.dot(x, wub_scr[:, cols], preferred_element_type=F32),
                jnp.dot(x, wvb_scr[:, cols], preferred_element_type=F32))

    def epilogue(c, r, u, u1, u2, v):
        cols = slice(c * cw, (c + 1) * cw)
        uc = u2 * cw_ref[0:1, cols] + cb_ref[:, cols]
        uc = uc + u1 * cw_ref[1:2, cols]
        uc = uc + u * cw_ref[2:3, cols]
        th = jnp.tanh(uc * (GELU_C0 + (GELU_C0 * GELU_C1) * (uc * uc)))
        h_ref[r * rs:(r + 1) * rs, cols] = ((uc + uc * th) * (0.5 * v)).astype(h_ref.dtype)

    @pl.when(i < n_prompt_tiles)
    def _():
        first = i % tiles_per_seq == 0
        n_r = tm // rs
        for c in range(tn // cw):
            cols = slice(c * cw, (c + 1) * cw)
            prev = jnp.where(first, 0.0, carry_ref[:, cols])
            for r in range(n_r):
                slot = (c * n_r + r) % 2
                u, v = dots(c, r)
                uh_scr[slot, 0:SUBLANES, :] = prev
                uh_scr[slot, SUBLANES:, :] = u
                u1 = uh_scr[slot, SUBLANES - 1:SUBLANES - 1 + rs, :]
                u2 = uh_scr[slot, SUBLANES - 2:SUBLANES - 2 + rs, :]
                epilogue(c, r, u, u1, u2, v)
                prev = u[rs - SUBLANES:]
            carry_ref[:, cols] = prev
            tail_ref[:, cols] = prev

    @pl.when(i >= n_prompt_tiles)
    def _():
        t = row % dec_seq
        n_seq = tm // dec_seq
        e1_scr[...] = jnp.zeros_like(e1_scr)
        e2_scr[...] = jnp.zeros_like(e2_scr)
        for lb in range(tn // LANES):
            lanes = slice(lb * LANES, (lb + 1) * LANES)
            e1_scr[lb, pl.ds(0, n_seq, stride=dec_seq), :] = buf_ref[:, 1, lanes]
            e2_scr[lb, pl.ds(0, n_seq, stride=dec_seq), :] = buf_ref[:, 0, lanes]
            e2_scr[lb, pl.ds(1, n_seq, stride=dec_seq), :] = buf_ref[:, 1, lanes]
        for c in range(tn // cw):
            for r in range(tm // rs):
                rows = slice(r * rs, (r + 1) * rs)
                lbs = range(c * lane_blocks, (c + 1) * lane_blocks)
                e1 = jnp.concatenate([e1_scr[lb, rows, :] for lb in lbs], axis=1)
                e2 = jnp.concatenate([e2_scr[lb, rows, :] for lb in lbs], axis=1)
                u, v = dots(c, r)
                epilogue(c, r, u, jnp.where(t >= 1, pltpu.roll(u, 1, axis=0), e1),
                         jnp.where(t >= 2, pltpu.roll(u, 2, axis=0), e2), v)
                for k, lb in enumerate(lbs):
                    us_scr[lb, rows, :] = u[:, k * LANES:(k + 1) * LANES]
        for lb in range(tn // LANES):
            lanes = slice(lb * LANES, (lb + 1) * LANES)
            fs_ref[:, 0, lanes] = us_scr[lb, pl.ds(dec_seq - 2, n_seq, stride=dec_seq), :]
            fs_ref[:, 1, lanes] = us_scr[lb, pl.ds(dec_seq - 1, n_seq, stride=dec_seq), :]
        tail_ref[...] = jnp.zeros_like(tail_ref)


def _ffn_up(xb, w_up, layer, cw, cb, conv_state, prev_state, *, m_prompt, seq, dec_seq, tm, tn):
    m, k = xb.shape
    d_ff = w_up.shape[2] // 2
    n_layers, bd = conv_state.shape[:2]
    m_s = m - m_prompt
    assert m_prompt % tm == 0 and m_s % tm == 0 and seq % tm == 0 and d_ff % tn == 0
    assert tm % dec_seq == 0 and dec_seq >= 2 and cw.shape[0] == 3 and conv_state.shape[2] == 2
    assert tn % min(MXU_COLS, tn) == 0 and tm % min(FFN_ROW_SUB, tm) == 0 and m_s == bd * dec_seq
    npt = m_prompt // tm
    nj = d_ff // tn
    n_seq = tm // dec_seq
    in_specs = [pl.BlockSpec((tm, k), lambda j, i: (i, 0)),
                pl.BlockSpec((None, k, tn), lambda j, i: (layer, 0, j)),
                pl.BlockSpec((None, k, tn), lambda j, i: (layer, 0, j + nj)),
                pl.BlockSpec((3, tn), lambda j, i: (0, j)),
                pl.BlockSpec((1, tn), lambda j, i: (0, j)),
                pl.BlockSpec((None, n_seq, 2, tn), lambda j, i: (layer, jnp.maximum(i - npt, 0), 0, j))]
    args = [xb, w_up, w_up, cw, cb.reshape(1, d_ff), conv_state]
    fs_shape, fs_spec, extra, extra_specs, aliases = _stacked_out(
        n_layers, layer, (bd, 2, d_ff), (n_seq, 2, tn), lambda j, i: (jnp.maximum(i - npt, 0), 0, j),
        prev_state, len(args), out_index=2)
    n_lb = tn // LANES
    return pl.pallas_call(
        functools.partial(_ffn_up_body, n_prompt_tiles=npt, tiles_per_seq=seq // tm, dec_seq=dec_seq),
        grid=(nj, m // tm),
        in_specs=in_specs + extra_specs,
        out_specs=[pl.BlockSpec((tm, tn), lambda j, i: (i, j)),
                   pl.BlockSpec((SUBLANES, tn), lambda j, i: (i, j)),
                   fs_spec],
        out_shape=[jax.ShapeDtypeStruct((m, d_ff), BF16),
                   jax.ShapeDtypeStruct((m // tm * SUBLANES, d_ff), F32),
                   fs_shape],
        input_output_aliases=aliases,
        scratch_shapes=[pltpu.VMEM((k, tn), BF16), pltpu.VMEM((k, tn), BF16), pltpu.VMEM((SUBLANES, tn), F32),
                        pltpu.VMEM((n_lb, tm, LANES), F32), pltpu.VMEM((n_lb, tm, LANES), F32),
                        pltpu.VMEM((n_lb, tm, LANES), F32),
                        pltpu.VMEM((2, min(FFN_ROW_SUB, tm) + SUBLANES, min(MXU_COLS, tn)), F32)],
        compiler_params=_cparams(2),
        name="ffn_up",
    )(*args, *extra)


def _ret_body(q_ref, k_ref, v_ref, g_ref, cos_ref, sin_ref, dm_ref, xi_ref, zeta_ref, cdec_ref, s0_ref,
              *rest, n_seq, chunk, n_chunks, scale):
    o_ref, so_ref, s_scr, q_scr, kz_scr, o_scr = rest[-6:]
    n = pl.program_id(2)

    @pl.when(n == 0)
    def _():
        s_scr[...] = s0_ref[...]

    n_heads = dm_ref.shape[0]
    dk = q_ref.shape[1] // n_heads
    half = dk // 2
    cos = cos_ref[...]
    sin = sin_ref[...]

    def rot(x):
        x1 = x[:, :half]
        x2 = x[:, half:]
        return jnp.concatenate([x1 * cos - x2 * sin, x2 * cos + x1 * sin], axis=-1)

    for hh in range(n_heads):
        cols = slice(hh * dk, (hh + 1) * dk)
        q = rot(q_ref[:, cols])
        k = rot(k_ref[:, cols]) * scale
        vb = v_ref[:, cols].astype(BF16)
        inner = _dot_nt(q, k) * dm_ref[hh]
        o_scr[hh] = _dot(inner, vb)
        q_scr[hh] = q
        kz_scr[hh] = k * zeta_ref[hh]
        cdec = cdec_ref[hh]
        for gi in range(n_seq):
            rows = pl.ds(gi * chunk, chunk)
            s = s_scr[gi, hh]
            o_scr[hh, rows, :] += _dot(q_scr[hh, rows, :], s) * xi_ref[hh, rows, :]
            s_scr[gi, hh] = s * cdec + _dot_tn(kz_scr[hh, rows, :], v_ref[rows, cols])
        o = o_scr[hh]
        gate = g_ref[:, cols]
        o = o * lax.rsqrt(jnp.mean(o * o, axis=-1, keepdims=True) + NORM_EPS) * _silu(gate)
        o_ref[:, cols] = o.astype(o_ref.dtype)

    @pl.when(n == n_chunks - 1)
    def _():
        so_ref[...] = s_scr[...]


def _ret_tables(n_heads, chunk, n_seq):
    lg = jnp.log(1.0 - 2.0 ** (-5.0 - jnp.arange(n_heads, dtype=F32)))
    idx = jnp.arange(chunk, dtype=F32)
    diff = idx[:, None] - idx[None, :]
    dmask = jnp.where(diff[None] >= 0, jnp.exp(jnp.maximum(diff, 0.0)[None] * lg[:, None, None]), 0.0)
    eye = jnp.eye(n_seq, dtype=F32)
    dm = jnp.einsum('ab,hts->hatbs', eye, dmask).reshape(n_heads, n_seq * chunk, n_seq * chunk)
    xi = jnp.tile(jnp.exp((idx[None, :] + 1.0) * lg[:, None]), (1, n_seq))[:, :, None]
    zeta = jnp.tile(jnp.exp((chunk - 1.0 - idx[None, :]) * lg[:, None]), (1, n_seq))[:, :, None]
    cdec = jnp.exp(chunk * lg)[:, None, None]
    return dm, xi, zeta, cdec


def _rope_tables(pos, half):
    inv = ROPE_BASE ** (-jnp.arange(half, dtype=F32) / half)
    ang = pos.astype(F32)[:, None] * inv[None, :]
    return jnp.cos(ang), jnp.sin(ang)


def _stacked_out(n_layers, layer, shape, block, index_map, prev, n_inputs, out_index=1):
    sds = jax.ShapeDtypeStruct((n_layers,) + tuple(shape), F32)
    spec = pl.BlockSpec((None,) + tuple(block), lambda *g: (layer,) + tuple(index_map(*g)))
    if prev is None:
        return sds, spec, [], [], {}
    return sds, spec, [prev], [pl.BlockSpec(memory_space=pl.ANY)], {n_inputs: out_index}


def _retention(p, s0, pos, *, layer, n_layers, prev_state, row0, n_batch, n_seq, chunk, n_chunks, n_heads, dk,
               heads_per_step):
    rows = n_seq * chunk
    assert row0 % rows == 0
    rb0 = row0 // rows
    dm, xi, zeta, cdec = _ret_tables(n_heads, chunk, n_seq)
    cos, sin = _rope_tables(pos, dk // 2)
    if n_seq > 1:
        assert n_chunks == 1
        cos = jnp.tile(cos, (n_seq, 1))
        sin = jnp.tile(sin, (n_seq, 1))
    hps = heads_per_step
    assert n_heads % hps == 0
    hg = n_heads // hps
    w = hps * dk

    def col(off):
        return lambda b, h, n: (rb0 + b * n_chunks + n, off * hg + h)

    tab = lambda b, h, n: (h, 0, 0)
    s0_layer = layer if s0.shape[0] > 1 else 0
    in_specs = [pl.BlockSpec((rows, w), col(0)), pl.BlockSpec((rows, w), col(1)),
                pl.BlockSpec((rows, w), col(2)), pl.BlockSpec((rows, w), col(3)),
                pl.BlockSpec((rows, dk // 2), lambda b, h, n: (n, 0)),
                pl.BlockSpec((rows, dk // 2), lambda b, h, n: (n, 0)),
                pl.BlockSpec((hps, rows, rows), tab),
                pl.BlockSpec((hps, rows, 1), tab),
                pl.BlockSpec((hps, rows, 1), tab),
                pl.BlockSpec((hps, 1, 1), tab),
                pl.BlockSpec((None, n_seq, hps, dk, dk), lambda b, h, n: (s0_layer, b, h, 0, 0))]
    args = [p, p, p, p, cos, sin, dm, xi, zeta, cdec, s0]
    so_shape, so_spec, extra, extra_specs, aliases = _stacked_out(
        n_layers, layer, (n_batch * n_seq, n_heads, dk, dk), (n_seq, hps, dk, dk),
        lambda b, h, n: (b, h, 0, 0), prev_state, len(args))
    total_rows = n_batch * n_chunks * rows
    return pl.pallas_call(
        functools.partial(_ret_body, n_seq=n_seq, chunk=chunk, n_chunks=n_chunks, scale=dk ** -0.5),
        grid=(n_batch, hg, n_chunks),
        in_specs=in_specs + extra_specs,
        out_specs=[pl.BlockSpec((rows, w), lambda b, h, n: (b * n_chunks + n, h)), so_spec],
        out_shape=[jax.ShapeDtypeStruct((total_rows, n_heads * dk), BF16), so_shape],
        input_output_aliases=aliases,
        scratch_shapes=[pltpu.VMEM((n_seq, hps, dk, dk), F32), pltpu.VMEM((hps, rows, dk), F32),
                        pltpu.VMEM((hps, rows, dk), F32), pltpu.VMEM((hps, rows, dk), F32)],
        compiler_params=_cparams(3),
        name="retention",
    )(*args, *extra)


def _rg_gates(xc, wa_ref, wx_ref, ba, bx, lam):
    nb, bw, _ = wa_ref.shape
    ra, ia = [], []
    for blk in range(nb):
        xb = xc[:, blk * bw:(blk + 1) * bw].astype(BF16)
        ra.append(jnp.dot(xb, wa_ref[blk], preferred_element_type=F32))
        ia.append(jnp.dot(xb, wx_ref[blk], preferred_element_type=F32))
    r = jax.nn.sigmoid(jnp.concatenate(ra, axis=-1) + ba)
    i = jax.nn.sigmoid(jnp.concatenate(ia, axis=-1) + bx)
    log_a = -RG_C * r * jax.nn.softplus(-lam)
    a = jnp.exp(log_a)
    y = -jnp.tanh(log_a) * (a * a + 1.0)
    mult = y * lax.rsqrt(jnp.maximum(y, TINY))
    return a, mult, i


def _scan_rows(a, b, row_in_seg, seg_len):
    s = 1
    while s < seg_len:
        a_sh = pltpu.roll(a, s, axis=0)
        b_sh = pltpu.roll(b, s, axis=0)
        m = row_in_seg >= s
        b = jnp.where(m, a * b_sh + b, b)
        a = jnp.where(m, a * a_sh, a)
        s *= 2
    return a, b


def _rg_prompt_body(xr_ref, gr_ref, cw_ref, cb_ref, wa_ref, wx_ref, ba_ref, bx_ref, lam_ref,
                    o_ref, hl_ref, xt_ref, h_scr, x_scr, *, n_tiles):
    n = pl.program_id(1)

    @pl.when(n == 0)
    def _():
        h_scr[...] = jnp.zeros_like(h_scr)
        x_scr[...] = jnp.zeros_like(x_scr)

    xr = xr_ref[...]
    tc = xr.shape[0]
    row = lax.broadcasted_iota(jnp.int32, xr.shape, 0)
    row8 = lax.broadcasted_iota(jnp.int32, (SUBLANES, xr.shape[1]), 0)
    prev = x_scr[...]
    n_tap = cw_ref.shape[0]
    xc = None
    for j in range(n_tap):
        d = n_tap - 1 - j
        if d == 0:
            xd = xr
        else:
            rolled = pltpu.roll(xr, d, axis=0)
            head = jnp.where(row8 < d, pltpu.roll(prev, d, axis=0), rolled[:SUBLANES])
            xd = jnp.concatenate([head, rolled[SUBLANES:]], axis=0)
        term = xd * cw_ref[j:j + 1, :]
        xc = term + cb_ref[...] if xc is None else xc + term
    a, mult, gate_i = _rg_gates(xc, wa_ref, wx_ref, ba_ref[...], bx_ref[...], lam_ref[...])
    mult = jnp.where(row + n * tc == 0, 1.0, mult)
    bterm = xc * gate_i * mult
    a_grp, b_grp = _scan_rows(a, bterm, row % SUBLANES, SUBLANES)
    h_prev = h_scr[...]
    groups = []
    for gi in range(tc // SUBLANES):
        rows = slice(gi * SUBLANES, (gi + 1) * SUBLANES)
        h_g = a_grp[rows] * h_prev + b_grp[rows]
        groups.append(h_g)
        h_prev = h_g[SUBLANES - 1:SUBLANES]
    h = jnp.concatenate(groups, axis=0)
    o_ref[...] = (h * jax.nn.gelu(gr_ref[...])).astype(o_ref.dtype)
    h_scr[...] = h[tc - 1:tc, :]
    x_scr[...] = xr[tc - SUBLANES:, :]

    @pl.when(n == n_tiles - 1)
    def _():
        hl_ref[...] = h[tc - 1:tc, :]
        xt_ref[...] = xr[tc - SUBLANES:, :]


def _rg_prompt(p, cw, cb, wa, wx, ba, bx, lam, *, n_batch, seq, col0, d_rg, total_rows):
    tc = min(RG_TIME_TILE, seq)
    assert seq % tc == 0 and col0 % d_rg == 0 and (tc & (tc - 1)) == 0
    nt = seq // tc
    cb0 = col0 // d_rg
    vec = lambda a: a.reshape(1, d_rg)
    full = lambda a: pl.BlockSpec(a.shape, lambda b, n: (0,) * a.ndim)
    args = (cw, vec(cb), wa, wx, vec(ba), vec(bx), vec(lam))
    return pl.pallas_call(
        functools.partial(_rg_prompt_body, n_tiles=nt),
        grid=(n_batch, nt),
        in_specs=[pl.BlockSpec((tc, d_rg), lambda b, n: (b * nt + n, cb0)),
                  pl.BlockSpec((tc, d_rg), lambda b, n: (b * nt + n, cb0 + 1))] + [full(a) for a in args],
        out_specs=[pl.BlockSpec((tc, d_rg), lambda b, n: (b * nt + n, 0)),
                   pl.BlockSpec((None, 1, d_rg), lambda b, n: (b, 0, 0)),
                   pl.BlockSpec((None, SUBLANES, d_rg), lambda b, n: (b, 0, 0))],
        out_shape=[jax.ShapeDtypeStruct((total_rows, d_rg), BF16),
                   jax.ShapeDtypeStruct((n_batch, 1, d_rg), F32),
                   jax.ShapeDtypeStruct((n_batch, SUBLANES, d_rg), F32)],
        scratch_shapes=[pltpu.VMEM((1, d_rg), F32), pltpu.VMEM((SUBLANES, d_rg), F32)],
        compiler_params=_cparams(2),
        name="rglru_prompt",
    )(p, p, *args)


def _rg_sample_body(xr_ref, gr_ref, e_ref, h0_ref, cw_ref, cb_ref, wa_ref, wx_ref, ba_ref, bx_ref, lam_ref,
                    o_ref, h_ref, *, dec_seq, first_pos):
    xr = xr_ref[...]
    row = lax.broadcasted_iota(jnp.int32, xr.shape, 0)
    t = row % dec_seq
    n_tap = cw_ref.shape[0]
    xc = None
    for j in range(n_tap):
        d = n_tap - 1 - j
        xd = xr if d == 0 else jnp.where(t >= d, pltpu.roll(xr, d, axis=0), e_ref[d - 1])
        term = xd * cw_ref[j:j + 1, :]
        xc = term + cb_ref[...] if xc is None else xc + term
    a, mult, gate_i = _rg_gates(xc, wa_ref, wx_ref, ba_ref[...], bx_ref[...], lam_ref[...])
    if first_pos == 0:
        mult = jnp.where(t == 0, 1.0, mult)
    bterm = xc * gate_i * mult + jnp.where(t == 0, a * h0_ref[...], 0.0)
    _, h = _scan_rows(a, bterm, t, dec_seq)
    h_ref[...] = h
    o_ref[...] = (h * jax.nn.gelu(gr_ref[...])).astype(o_ref.dtype)


def _rg_sample(p, e_prev, h0_rows, cw, cb, wa, wx, ba, bx, lam, *, row0, m_s, dec_seq, col0, d_rg):
    tr = min(128, m_s)
    assert m_s % tr == 0 and row0 % tr == 0 and tr % dec_seq == 0 and (dec_seq & (dec_seq - 1)) == 0
    rb0 = row0 // tr
    cb0 = col0 // d_rg
    vec = lambda a: a.reshape(1, d_rg)
    full = lambda a: pl.BlockSpec(a.shape, lambda i: (0,) * a.ndim)
    args = (cw, vec(cb), wa, wx, vec(ba), vec(bx), vec(lam))
    n_prev = e_prev.shape[0]
    return pl.pallas_call(
        functools.partial(_rg_sample_body, dec_seq=dec_seq, first_pos=PAST_LEN),
        grid=(m_s // tr,),
        in_specs=[pl.BlockSpec((tr, d_rg), lambda i: (rb0 + i, cb0)),
                  pl.BlockSpec((tr, d_rg), lambda i: (rb0 + i, cb0 + 1)),
                  pl.BlockSpec((n_prev, tr, d_rg), lambda i: (0, i, 0)),
                  pl.BlockSpec((tr, d_rg), lambda i: (i, 0))] + [full(a) for a in args],
        out_specs=[pl.BlockSpec((tr, d_rg), lambda i: (i, 0)), pl.BlockSpec((tr, d_rg), lambda i: (i, 0))],
        out_shape=[jax.ShapeDtypeStruct((m_s, d_rg), BF16), jax.ShapeDtypeStruct((m_s, d_rg), F32)],
        compiler_params=_cparams(1),
        name="rglru_sample",
    )(p, p, e_prev, h0_rows, *args)


def _lower_bound(logit_rows, layer):
    m = logit_rows[0]
    for z in logit_rows[1:]:
        m = jnp.maximum(m, z)
    e = [jnp.exp(z - m) for z in logit_rows]
    tot = e[0]
    for x in e[1:]:
        tot = tot + x
    if layer == 0:
        return jnp.zeros_like(tot)
    num = e[1]
    for x in e[2:layer + 1]:
        num = num + x
    return num / tot


_GLA_VREGS = GLA_BLOCK // SUBLANES


def _gla_level_masks():
    r = np.arange(GLA_BLOCK)
    t = r // SUBLANES + _GLA_VREGS * (r % SUBLANES)
    tq, tk = t[:, None], t[None, :]
    n_levels = GLA_BLOCK.bit_length()
    masks = np.zeros((n_levels, GLA_BLOCK, GLA_BLOCK), np.float32)
    masks[0] = tq == tk
    for lvl in range(1, n_levels):
        gs = 1 << lvl
        masks[lvl] = (tq > tk) & (tq // gs == tk // gs) & (tq // (gs // 2) != tk // (gs // 2))
    return masks


def _gla_prompt_body(q_ref, f_ref, i_ref, g_ref, lbl_ref, ng_ref, msk_ref, o_ref, so_ref, st_scr, o_scr,
                     *, heads, layer, n_blocks):
    n = pl.program_id(2)

    @pl.when(n == 0)
    def _():
        st_scr[...] = jnp.zeros_like(st_scr)

    nv = _GLA_VREGS
    sub = lax.broadcasted_iota(jnp.int32, (SUBLANES, LANES), 0)
    zero = jnp.zeros((SUBLANES, LANES), F32)
    for hh in range(heads):
        cols = slice(hh * LANES, (hh + 1) * LANES)

        def load(ref):
            return jnp.concatenate([ref[hh, pl.ds(j, SUBLANES, stride=nv), :] for j in range(nv)], axis=0)

        def groups(x):
            return [x[SUBLANES * j:SUBLANES * (j + 1)] for j in range(nv)]

        lb = _lower_bound([lbl_ref[r:r + 1, cols] for r in range(lbl_ref.shape[0])], layer)
        q = _silu(load(q_ref))
        fg = lb + (1.0 - lb) * jax.nn.sigmoid(load(f_ref))
        kk = 1.0 - fg
        v = load(i_ref)
        vb = v.astype(BF16)
        lf = groups(jnp.log(fg))

        c = [lf[0]]
        for j in range(1, nv):
            c.append(c[-1] + lf[j])
        tot = c[nv - 1]
        x = tot
        s = 1
        while s < SUBLANES:
            x = x + jnp.where(sub >= s, pltpu.roll(x, s, axis=0), 0.0)
            s *= 2
        before = x - tot
        bj = [cj + before for cj in c]
        b = jnp.concatenate(bj, axis=0)
        b_last = bj[nv - 1][SUBLANES - 1:SUBLANES, :]

        qj = groups(q)
        kj = groups(kk)
        def keep(lvl, scores, acc):
            return jnp.where(msk_ref[lvl] > 0, scores, acc)

        a_mat = keep(0, _dot_nt(q, kk), jnp.zeros((GLA_BLOCK, GLA_BLOCK), F32))
        lvl = 1
        gs = 2
        while gs <= nv:
            hs = gs // 2
            qd, kd = [], []
            for j in range(nv):
                ref = (j // gs) * gs + hs - 1
                if j % gs >= hs:
                    qd.append(qj[j] * jnp.exp(bj[j] - bj[ref]))
                    kd.append(zero)
                else:
                    kd.append(kj[j] if j == ref else kj[j] * jnp.exp(bj[ref] - bj[j]))
                    qd.append(zero)
            a_mat = keep(lvl, _dot_nt(jnp.concatenate(qd, axis=0), jnp.concatenate(kd, axis=0)), a_mat)
            lvl += 1
            gs *= 2
        m = 2
        while m <= SUBLANES:
            src = sub - sub % m + (m // 2 - 1)
            ref = zero
            for s_src in range(m // 2 - 1, SUBLANES, m):
                row_b = jnp.broadcast_to(bj[nv - 1][s_src:s_src + 1, :], (SUBLANES, LANES))
                ref = jnp.where(src == s_src, row_b, ref)
            e = [jnp.exp(-jnp.abs(bj[j] - ref)) for j in range(nv)]
            qd = jnp.concatenate([qj[j] * e[j] for j in range(nv)], axis=0)
            kd = jnp.concatenate([kj[j] * e[j] for j in range(nv)], axis=0)
            a_mat = keep(lvl, _dot_nt(qd, kd), a_mat)
            lvl += 1
            m *= 2

        st = st_scr[hh]
        o = _dot(a_mat, vb) + _dot_nt(q * jnp.exp(b), st)
        st_new = st * jnp.exp(b_last) + _dot_tn(vb, kk * jnp.exp(b_last - b))
        st_scr[hh] = st_new

        o = o * lax.rsqrt(jnp.mean(o * o, axis=-1, keepdims=True) + NORM_EPS) * ng_ref[...]
        o = o * jax.nn.sigmoid(load(g_ref))
        for j in range(nv):
            o_scr[hh, pl.ds(j, SUBLANES, stride=nv), :] = o[SUBLANES * j:SUBLANES * (j + 1)]
        o_ref[:, cols] = o_scr[hh].astype(o_ref.dtype)

    @pl.when(n == n_blocks - 1)
    def _():
        for hh in range(heads):
            so_ref[hh] = st_scr[hh].T


def _gla_prompt(p3, lb_logits, norm_g, *, layer, n_batch, seq, n_heads, total_rows):
    blk = GLA_BLOCK
    hb = GLA_HEADS_PER_STEP
    assert seq % blk == 0 and n_heads % hb == 0
    nb = seq // blk
    hg = n_heads // hb
    w = hb * LANES
    masks = jnp.asarray(_gla_level_masks())

    def part_spec(part):
        return pl.BlockSpec((hb, blk, LANES), lambda b, h, n: (part * hg + h, b * nb + n, 0))

    return pl.pallas_call(
        functools.partial(_gla_prompt_body, heads=hb, layer=layer, n_blocks=nb),
        grid=(n_batch, hg, nb),
        in_specs=[part_spec(part) for part in range(4)] + [
                  pl.BlockSpec((lb_logits.shape[0], w), lambda b, h, n: (0, h)),
                  pl.BlockSpec((1, LANES), lambda b, h, n: (0, 0)),
                  pl.BlockSpec(masks.shape, lambda b, h, n: (0, 0, 0))],
        out_specs=[pl.BlockSpec((blk, w), lambda b, h, n: (b * nb + n, h)),
                   pl.BlockSpec((None, hb, LANES, LANES), lambda b, h, n: (b, h, 0, 0))],
        out_shape=[jax.ShapeDtypeStruct((total_rows, n_heads * LANES), BF16),
                   jax.ShapeDtypeStruct((n_batch, n_heads, LANES, LANES), F32)],
        scratch_shapes=[pltpu.VMEM((hb, LANES, LANES), F32), pltpu.VMEM((hb, blk, LANES), F32)],
        compiler_params=_cparams(3),
        name="hgrn_prompt",
    )(p3, p3, p3, p3, lb_logits, norm_g.reshape(1, LANES), masks)


def _gla_sample_body(q_ref, f_ref, i_ref, g_ref, lbl_ref, ng_ref, s0_ref, *rest, layer):
    o_ref, so_ref, qd_scr, kd_scr, o_scr = rest[-5:]
    n_g, n_t, n_h, _ = q_ref.shape
    lb = _lower_bound([lbl_ref[r] for r in range(lbl_ref.shape[0])], layer)
    q = _silu(q_ref[...])
    fg = lb + (1.0 - lb) * jax.nn.sigmoid(f_ref[...])
    kk = 1.0 - fg
    lf = jnp.log(fg)
    v = i_ref[...]
    bt = [lf[:, 0]]
    for t in range(1, n_t):
        bt.append(bt[-1] + lf[:, t])
    b_last = bt[n_t - 1]
    for t in range(n_t):
        acc = None
        for s in range(t + 1):
            w = q[:, t] * kk[:, s]
            if s < t:
                w = w * jnp.exp(bt[t] - bt[s])
            term = jnp.sum(w, axis=-1, keepdims=True) * v[:, s]
            acc = term if acc is None else acc + term
        o_scr[:, t] = acc
        qd_scr[:, t] = q[:, t] * jnp.exp(bt[t])
        kd_scr[:, t] = kk[:, t] * jnp.exp(b_last - bt[t])
    e_last = jnp.exp(b_last)
    for gi in range(n_g):
        e_cols = e_last[gi].T
        for h in range(n_h):
            s = s0_ref[gi, h]
            o_scr[gi, :, h, :] += _dot(qd_scr[gi, :, h, :], s)
            so_ref[gi, h] = s * e_cols[:, h:h + 1] + _dot_tn(kd_scr[gi, :, h, :], i_ref[gi, :, h, :])
    o = o_scr[...]
    o = o * lax.rsqrt(jnp.mean(o * o, axis=-1, keepdims=True) + NORM_EPS) * ng_ref[...]
    o_ref[...] = (o * jax.nn.sigmoid(g_ref[...])).astype(o_ref.dtype)


def _gla_sample(p4, lb_logits, norm_g, s0, *, layer, prev_state, seq0):
    _, td, h4, _ = p4.shape
    bd = s0.shape[1]
    nh = h4 // 4
    g = min(GLA_SAMPLE_GROUP, bd)
    assert bd % g == 0 and seq0 % g == 0
    blk = (g, td, nh, LANES)
    sblk = (g, nh, LANES, LANES)
    lb3 = lb_logits.reshape(lb_logits.shape[0], nh, LANES)
    part = lambda off: pl.BlockSpec(blk, lambda i: (seq0 // g + i, 0, off, 0))
    in_specs = [part(0), part(1), part(2), part(3),
                pl.BlockSpec(lb3.shape, lambda i: (0, 0, 0)),
                pl.BlockSpec((1, LANES), lambda i: (0, 0)),
                pl.BlockSpec((None,) + sblk, lambda i: (layer, i, 0, 0, 0))]
    args = [p4, p4, p4, p4, lb3, norm_g.reshape(1, LANES), s0]
    so_shape, so_spec, extra, extra_specs, aliases = _stacked_out(
        s0.shape[0], layer, (bd, nh, LANES, LANES), sblk, lambda i: (i, 0, 0, 0), prev_state, len(args))
    return pl.pallas_call(
        functools.partial(_gla_sample_body, layer=layer),
        grid=(bd // g,),
        in_specs=in_specs + extra_specs,
        out_specs=[pl.BlockSpec(blk, lambda i: (i, 0, 0, 0)), so_spec],
        out_shape=[jax.ShapeDtypeStruct((bd, td, nh, LANES), BF16), so_shape],
        input_output_aliases=aliases,
        scratch_shapes=[pltpu.VMEM(blk, F32), pltpu.VMEM(blk, F32), pltpu.VMEM(blk, F32)],
        compiler_params=_cparams(1),
        name="hgrn_sample",
    )(*args, *extra)


def _prev_rows(buf, dec_seq):
    bd, n_prev, d = buf.shape
    outs = []
    for dd in range(1, n_prev + 1):
        rows = [buf[:, n_prev - dd + t] if t < dd else jnp.zeros((bd, d), buf.dtype) for t in range(dec_seq)]
        outs.append(jnp.stack(rows, axis=1).reshape(bd * dec_seq, d))
    return outs


def _first_rows(vals, dec_seq):
    bd, d = vals.shape
    z = jnp.zeros((bd, dec_seq - 1, d), vals.dtype)
    return jnp.concatenate([vals[:, None, :], z], axis=1).reshape(bd * dec_seq, d)


def kernel(x_prompt, x_sample, state_ret, state_rglru_h, state_rglru_conv, state_hgrn, state_ffn_conv,
           ev_w_in, ev_w_out, ev_rg_conv_w, ev_rg_conv_b, ev_rg_wa, ev_rg_ba, ev_rg_wx, ev_rg_bx, ev_rg_lambda,
           od_w_in, od_w_out, od_norm_g, od_lb_logits, ln_g, ln_b, ffn_w_up, ffn_conv_w, ffn_conv_b, ffn_w_down):
    bp, tp, d_model = x_prompt.shape
    bd, td, _ = x_sample.shape
    depth = ln_g.shape[0]
    m_p, m_s = bp * tp, bd * td
    m = m_p + m_s
    alpha = (2.0 * depth) ** 0.25
    h_ret, dk_ret = state_ret.shape[2], state_ret.shape[3]
    d_ret = h_ret * dk_ret
    d_rg = state_rglru_h.shape[-1]
    h_hg = state_hgrn.shape[2]
    d_ff = ffn_conv_b.shape[-1]
    assert state_hgrn.shape[3] == LANES and state_hgrn.shape[4] == LANES and d_ret == d_rg
    assert ffn_conv_w.shape[1] == 3 and td >= 3

    x = (x_prompt.reshape(m_p, d_model), x_sample.reshape(m_s, d_model))
    xb = jnp.concatenate(x, axis=0).astype(BF16)
    pos_p = jnp.arange(tp, dtype=jnp.int32)
    pos_s = PAST_LEN + jnp.arange(td, dtype=jnp.int32)
    zero_ret = jnp.zeros((1, bp) + state_ret.shape[2:], F32)
    n_even = state_ret.shape[0]
    w_out_ev = ev_w_out.astype(BF16)
    w_out_od = od_w_out.astype(BF16)
    w_down = ffn_w_down.astype(BF16)

    ret_p = ret_s = hg_s = ff_s = None
    n_h_p, n_h_s, n_cv_p, n_cv_s, n_hg_p, n_ff_p = [], [], [], [], [], []
    tm_ff = min(ROW_TILE, m_s)
    tm_ln = min(LN_ROW_TILE, m_s)
    tm_mix = min(MIX_LN_ROW_TILE, m_s)
    tm_in = min(IN_PROJ_ROW_TILE, m_s)
    for l in range(depth):
        if l % 2 == 0:
            e = l // 2
            p = _matmul(xb, ev_w_in, e, tm_in, IN_PROJ_COLS)
            chunk = RET_CHUNK if tp % RET_CHUNK == 0 else tp
            o_ret, ret_p = _retention(p, zero_ret, pos_p, layer=e, n_layers=n_even, prev_state=ret_p, row0=0,
                                      n_batch=bp, n_seq=1, chunk=chunk, n_chunks=tp // chunk, n_heads=h_ret,
                                      dk=dk_ret, heads_per_step=h_ret)
            g_ret = min(RET_SAMPLE_GROUP, bd)
            o_ret_s, ret_s = _retention(p, state_ret, pos_s, layer=e, n_layers=n_even, prev_state=ret_s,
                                        row0=m_p, n_batch=bd // g_ret, n_seq=g_ret, chunk=td, n_chunks=1,
                                        n_heads=h_ret, dk=dk_ret, heads_per_step=1)
            wa = ev_rg_wa[e].astype(BF16)
            wx = ev_rg_wx[e].astype(BF16)
            rg_args = (ev_rg_conv_w[e], ev_rg_conv_b[e], wa, wx, ev_rg_ba[e], ev_rg_bx[e], ev_rg_lambda[e])
            o_rg, hl_p, xt_p = _rg_prompt(p, *rg_args, n_batch=bp, seq=tp, col0=4 * d_ret, d_rg=d_rg,
                                          total_rows=m_p)
            e_prev = jnp.stack(_prev_rows(state_rglru_conv[e], td))
            o_rg_s, h_s = _rg_sample(p, e_prev, _first_rows(state_rglru_h[e], td), *rg_args,
                                     row0=m_p, m_s=m_s, dec_seq=td, col0=4 * d_ret, d_rg=d_rg)
            n_conv = state_rglru_conv.shape[2]
            xr_s = p[m_p:, 4 * d_ret:4 * d_ret + d_rg].reshape(bd, td, d_rg)
            n_h_p.append(hl_p[:, 0])
            n_h_s.append(h_s.reshape(bd, td, d_rg)[:, td - 1])
            n_cv_p.append(xt_p[:, SUBLANES - n_conv:])
            n_cv_s.append(xr_s[:, td - n_conv:])
            x_new, xb = _proj_ln([(o_ret, o_ret_s), (o_rg, o_rg_s)], w_out_ev, e, x, ln_g[l, 0], ln_b[l, 0],
                                 alpha, tm_mix, m_p)
        else:
            o = l // 2
            p3 = _matmul(xb, od_w_in, o, tm_in, IN_PROJ_COLS, column_major=True)
            o_hg, g_p = _gla_prompt(p3, od_lb_logits, od_norm_g[o], layer=o, n_batch=bp, seq=tp,
                                    n_heads=h_hg, total_rows=m_p)
            p4 = jnp.swapaxes(p3[:, m_p:], 0, 1).reshape(bd, td, 4 * h_hg, LANES)
            o_hg_s, hg_s = _gla_sample(p4, od_lb_logits, od_norm_g[o], state_hgrn, layer=o, prev_state=hg_s,
                                       seq0=0)
            n_hg_p.append(g_p)
            x_new, xb = _proj_ln([(o_hg, o_hg_s.reshape(m_s, h_hg * LANES))], w_out_od, o, x, ln_g[l, 0],
                                 ln_b[l, 0], alpha, tm_mix, m_p)
        x = (x_new,)
        h, tails, ff_s = _ffn_up(xb, ffn_w_up, l, ffn_conv_w[l], ffn_conv_b[l], state_ffn_conv, ff_s,
                                 m_prompt=m_p, seq=tp, dec_seq=td, tm=tm_ff, tn=512)
        tiles_per_seq = tp // tm_ff
        tails = tails.reshape(m // tm_ff, SUBLANES, d_ff)[:bp * tiles_per_seq]
        tails = tails.reshape(bp, tiles_per_seq, SUBLANES, d_ff)
        n_ff_p.append(tails[:, tiles_per_seq - 1, SUBLANES - 2:])
        last = l == depth - 1
        x_new, xb = _proj_ln([(h,)], w_down, l, x, ln_g[l, 1], ln_b[l, 1], alpha, tm_ln, m_p, split_out=last)
        x = (x_new,)

    y_prompt = x_new.reshape(bp, tp, d_model)
    y_sample = xb.reshape(bd, td, d_model)
    return (y_prompt, y_sample, ret_p, ret_s, jnp.stack(n_h_p), jnp.stack(n_h_s),
            jnp.stack(n_cv_p), jnp.stack(n_cv_s), jnp.stack(n_hg_p), hg_s,
            jnp.stack(n_ff_p), ff_s)
```

```python
import functools

import numpy as np
import jax
import jax.numpy as jnp
from jax import lax
from jax.experimental import pallas as pl
from jax.experimental.pallas import tpu as pltpu

F32 = jnp.float32
BF16 = jnp.bfloat16

LN_EPS = 1e-5
NORM_EPS = 1e-6
ROPE_BASE = 10000.0
RG_C = 8.0
TINY = 1e-37
GELU_C0 = 0.7978845608028654
GELU_C1 = 0.044715
PAST_LEN = 16384
LANES = 128
SUBLANES = 8
MXU_COLS = 256
ROW_TILE = 512
FFN_ROW_SUB = 128
LN_ROW_TILE = 256
MIX_LN_ROW_TILE = 512
IN_PROJ_ROW_TILE = 512
IN_PROJ_COLS = 2048
RET_CHUNK = 256
RET_SAMPLE_GROUP = 8
RG_TIME_TILE = 256
GLA_BLOCK = 128
GLA_HEADS_PER_STEP = 16
GLA_SAMPLE_GROUP = 4
MIB = 1024 * 1024


def _cparams(n_axes, vmem_mib=48):
    return pltpu.CompilerParams(dimension_semantics=("arbitrary",) * n_axes,
                                vmem_limit_bytes=vmem_mib * MIB)


def _dot(a, b):
    return jnp.dot(a.astype(BF16), b.astype(BF16), preferred_element_type=F32)


def _dot_nt(a, b):
    return lax.dot_general(a.astype(BF16), b.astype(BF16), (((1,), (1,)), ((), ())),
                           preferred_element_type=F32)


def _dot_tn(a, b):
    return lax.dot_general(a.astype(BF16), b.astype(BF16), (((0,), (0,)), ((), ())),
                           preferred_element_type=F32)


def _silu(x):
    return x * jax.nn.sigmoid(x)


def _mm_body(x_ref, w_ref, o_ref, wb_scr):
    @pl.when(pl.program_id(1) == 0)
    def _():
        wb_scr[...] = w_ref[...].astype(BF16)

    res = jnp.dot(x_ref[...], wb_scr[...], preferred_element_type=F32)
    if len(o_ref.shape) == 2:
        o_ref[...] = res
    else:
        for c in range(o_ref.shape[0]):
            o_ref[c] = res[:, c * LANES:(c + 1) * LANES]


def _matmul(xb, w, layer, tm, tn, column_major=False):
    m, k = xb.shape
    n = w.shape[2]
    assert m % tm == 0 and n % tn == 0 and tn % LANES == 0
    if column_major:
        out_spec = pl.BlockSpec((tn // LANES, tm, LANES), lambda j, i: (j, i, 0))
        out_shape = jax.ShapeDtypeStruct((n // LANES, m, LANES), F32)
    else:
        out_spec = pl.BlockSpec((tm, tn), lambda j, i: (i, j))
        out_shape = jax.ShapeDtypeStruct((m, n), F32)
    return pl.pallas_call(
        _mm_body,
        grid=(n // tn, m // tm),
        in_specs=[pl.BlockSpec((tm, k), lambda j, i: (i, 0)),
                  pl.BlockSpec((None, k, tn), lambda j, i: (layer, 0, j))],
        out_specs=out_spec,
        out_shape=out_shape,
        scratch_shapes=[pltpu.VMEM((k, tn), BF16)],
        compiler_params=_cparams(2, vmem_mib=56),
        name="in_proj",
    )(xb, w)


def _proj_ln_body(*refs, part_arity, res_arity, alpha, n_prompt_tiles, split_out):
    i = pl.program_id(0)
    n_in = sum(part_arity)
    a_refs = refs[:n_in]
    w_refs = refs[n_in:n_in + len(part_arity)]
    x_refs = refs[n_in + len(part_arity):][:res_arity]
    g_ref, b_ref, o1_ref, o2_ref = refs[n_in + len(part_arity) + res_arity:]

    def rows_of(group):
        if len(group) == 1:
            return group[0][...]
        return jnp.where(i < n_prompt_tiles, group[0][...], group[1][...])

    acc = None
    pos = 0
    for arity, w_ref in zip(part_arity, w_refs):
        d = jnp.dot(rows_of(a_refs[pos:pos + arity]), w_ref[...], preferred_element_type=F32)
        pos += arity
        acc = d if acc is None else acc + d
    y = alpha * rows_of(x_refs) + acc
    mu = jnp.mean(y, axis=-1, keepdims=True)
    d = y - mu
    var = jnp.mean(d * d, axis=-1, keepdims=True)
    out = d * lax.rsqrt(var + LN_EPS) * g_ref[...] + b_ref[...]
    if split_out:
        @pl.when(i < n_prompt_tiles)
        def _():
            o1_ref[...] = out

        @pl.when(i >= n_prompt_tiles)
        def _():
            o2_ref[...] = out
    else:
        o1_ref[...] = out
        o2_ref[...] = out.astype(BF16)


def _proj_ln(parts, wb, layer, x, g, b, alpha, tm, m_prompt, split_out=False):
    m = sum(a.shape[0] for a in x)
    d = x[0].shape[1]
    assert m % tm == 0 and m_prompt % tm == 0
    npt = m_prompt // tm
    kp = parts[0][0].shape[1]
    assert all(a.shape[1] == kp for p in parts for a in p) and wb.shape[1] == kp * len(parts)
    prompt_rows = lambda i: (jnp.minimum(i, npt - 1), 0)
    sample_rows = lambda i: (jnp.maximum(i - npt, 0), 0)

    def row_specs(group, width):
        if len(group) == 1:
            return [pl.BlockSpec((tm, width), lambda i: (i, 0))]
        return [pl.BlockSpec((tm, width), prompt_rows), pl.BlockSpec((tm, width), sample_rows)]

    in_specs, args = [], []
    for p in parts:
        args += list(p)
        in_specs += row_specs(p, kp)
    for part in range(len(parts)):
        in_specs.append(pl.BlockSpec((None, kp, d), lambda i, part=part: (layer, part, 0),
                                     pipeline_mode=pl.Buffered(1)))
    in_specs += row_specs(x, d)
    in_specs += [pl.BlockSpec((1, d), lambda i: (0, 0)),
                 pl.BlockSpec((1, d), lambda i: (0, 0))]
    if split_out:
        out_specs = [pl.BlockSpec((tm, d), prompt_rows), pl.BlockSpec((tm, d), sample_rows)]
        out_shape = [jax.ShapeDtypeStruct((m_prompt, d), F32), jax.ShapeDtypeStruct((m - m_prompt, d), F32)]
    else:
        out_specs = [pl.BlockSpec((tm, d), lambda i: (i, 0)), pl.BlockSpec((tm, d), lambda i: (i, 0))]
        out_shape = [jax.ShapeDtypeStruct((m, d), F32), jax.ShapeDtypeStruct((m, d), BF16)]
    return pl.pallas_call(
        functools.partial(_proj_ln_body, part_arity=tuple(len(p) for p in parts), res_arity=len(x), alpha=alpha,
                          n_prompt_tiles=npt, split_out=split_out),
        grid=(m // tm,),
        in_specs=in_specs,
        out_specs=out_specs,
        out_shape=out_shape,
        compiler_params=_cparams(1, vmem_mib=56),
        name="proj_ln",
    )(*args, *([wb] * len(parts)), *x, g.reshape(1, d), b.reshape(1, d))


def _ffn_up_body(x_ref, wu_ref, wv_ref, cw_ref, cb_ref, buf_ref, *rest,
                 n_prompt_tiles, tiles_per_seq, dec_seq):
    h_ref, tail_ref, fs_ref, wub_scr, wvb_scr, carry_ref, e1_scr, e2_scr, us_scr, uh_scr = rest[-10:]
    i = pl.program_id(1)

    @pl.when(i == 0)
    def _():
        wub_scr[...] = wu_ref[...].astype(BF16)
        wvb_scr[...] = wv_ref[...].astype(BF16)

    tm = x_ref.shape[0]
    tn = h_ref.shape[1]
    cw = min(MXU_COLS, tn)
    rs = min(FFN_ROW_SUB, tm)
    lane_blocks = cw // LANES
    row = lax.broadcasted_iota(jnp.int32, (rs, cw), 0)

    def dots(c, r):
        cols = slice(c * cw, (c + 1) * cw)
        x = x_ref[r * rs:(r + 1) * rs, :]
        return (jnp.dot(x, wub_scr[:, cols], preferred_element_type=F32),
                jnp.dot(x, wvb_scr[:, cols], preferred_element_type=F32))

    def epilogue(c, r, u, u1, u2, v):
        cols = slice(c * cw, (c + 1) * cw)
        uc = u2 * cw_ref[0:1, cols] + cb_ref[:, cols]
        uc = uc + u1 * cw_ref[1:2, cols]
        uc = uc + u * cw_ref[2:3, cols]
        th = jnp.tanh(uc * (GELU_C0 + (GELU_C0 * GELU_C1) * (uc * uc)))
        h_ref[r * rs:(r + 1) * rs, cols] = ((uc + uc * th) * (0.5 * v)).astype(h_ref.dtype)

    @pl.when(i < n_prompt_tiles)
    def _():
        first = i % tiles_per_seq == 0
        n_r = tm // rs
        for c in range(tn // cw):
            cols = slice(c * cw, (c + 1) * cw)
            prev = jnp.where(first, 0.0, carry_ref[:, cols])
            for r in range(n_r):
                slot = (c * n_r + r) % 2
                u, v = dots(c, r)
                uh_scr[slot, 0:SUBLANES, :] = prev
                uh_scr[slot, SUBLANES:, :] = u
                u1 = uh_scr[slot, SUBLANES - 1:SUBLANES - 1 + rs, :]
                u2 = uh_scr[slot, SUBLANES - 2:SUBLANES - 2 + rs, :]
                epilogue(c, r, u, u1, u2, v)
                prev = u[rs - SUBLANES:]
            carry_ref[:, cols] = prev
            tail_ref[:, cols] = prev

    @pl.when(i >= n_prompt_tiles)
    def _():
        t = row % dec_seq
        n_seq = tm // dec_seq
        e1_scr[...] = jnp.zeros_like(e1_scr)
        e2_scr[...] = jnp.zeros_like(e2_scr)
        for lb in range(tn // LANES):
            lanes = slice(lb * LANES, (lb + 1) * LANES)
            e1_scr[lb, pl.ds(0, n_seq, stride=dec_seq), :] = buf_ref[:, 1, lanes]
            e2_scr[lb, pl.ds(0, n_seq, stride=dec_seq), :] = buf_ref[:, 0, lanes]
            e2_scr[lb, pl.ds(1, n_seq, stride=dec_seq), :] = buf_ref[:, 1, lanes]
        for c in range(tn // cw):
            for r in range(tm // rs):
                rows = slice(r * rs, (r + 1) * rs)
                lbs = range(c * lane_blocks, (c + 1) * lane_blocks)
                e1 = jnp.concatenate([e1_scr[lb, rows, :] for lb in lbs], axis=1)
                e2 = jnp.concatenate([e2_scr[lb, rows, :] for lb in lbs], axis=1)
                u, v = dots(c, r)
                epilogue(c, r, u, jnp.where(t >= 1, pltpu.roll(u, 1, axis=0), e1),
                         jnp.where(t >= 2, pltpu.roll(u, 2, axis=0), e2), v)
                for k, lb in enumerate(lbs):
                    us_scr[lb, rows, :] = u[:, k * LANES:(k + 1) * LANES]
        for lb in range(tn // LANES):
            lanes = slice(lb * LANES, (lb + 1) * LANES)
            fs_ref[:, 0, lanes] = us_scr[lb, pl.ds(dec_seq - 2, n_seq, stride=dec_seq), :]
            fs_ref[:, 1, lanes] = us_scr[lb, pl.ds(dec_seq - 1, n_seq, stride=dec_seq), :]
        tail_ref[...] = jnp.zeros_like(tail_ref)


def _ffn_up(xb, w_up, layer, cw, cb, conv_state, prev_state, *, m_prompt, seq, dec_seq, tm, tn):
    m, k = xb.shape
    d_ff = w_up.shape[2] // 2
    n_layers, bd = conv_state.shape[:2]
    m_s = m - m_prompt
    assert m_prompt % tm == 0 and m_s % tm == 0 and seq % tm == 0 and d_ff % tn == 0
    assert tm % dec_seq == 0 and dec_seq >= 2 and cw.shape[0] == 3 and conv_state.shape[2] == 2
    assert tn % min(MXU_COLS, tn) == 0 and tm % min(FFN_ROW_SUB, tm) == 0 and m_s == bd * dec_seq
    npt = m_prompt // tm
    nj = d_ff // tn
    n_seq = tm // dec_seq
    in_specs = [pl.BlockSpec((tm, k), lambda j, i: (i, 0)),
                pl.BlockSpec((None, k, tn), lambda j, i: (layer, 0, j)),
                pl.BlockSpec((None, k, tn), lambda j, i: (layer, 0, j + nj)),
                pl.BlockSpec((3, tn), lambda j, i: (0, j)),
                pl.BlockSpec((1, tn), lambda j, i: (0, j)),
                pl.BlockSpec((None, n_seq, 2, tn), lambda j, i: (layer, jnp.maximum(i - npt, 0), 0, j))]
    args = [xb, w_up, w_up, cw, cb.reshape(1, d_ff), conv_state]
    fs_shape, fs_spec, extra, extra_specs, aliases = _stacked_out(
        n_layers, layer, (bd, 2, d_ff), (n_seq, 2, tn), lambda j, i: (jnp.maximum(i - npt, 0), 0, j),
        prev_state, len(args), out_index=2)
    n_lb = tn // LANES
    return pl.pallas_call(
        functools.partial(_ffn_up_body, n_prompt_tiles=npt, tiles_per_seq=seq // tm, dec_seq=dec_seq),
        grid=(nj, m // tm),
        in_specs=in_specs + extra_specs,
        out_specs=[pl.BlockSpec((tm, tn), lambda j, i: (i, j)),
                   pl.BlockSpec((SUBLANES, tn), lambda j, i: (i, j)),
                   fs_spec],
        out_shape=[jax.ShapeDtypeStruct((m, d_ff), BF16),
                   jax.ShapeDtypeStruct((m // tm * SUBLANES, d_ff), F32),
                   fs_shape],
        input_output_aliases=aliases,
        scratch_shapes=[pltpu.VMEM((k, tn), BF16), pltpu.VMEM((k, tn), BF16), pltpu.VMEM((SUBLANES, tn), F32),
                        pltpu.VMEM((n_lb, tm, LANES), F32), pltpu.VMEM((n_lb, tm, LANES), F32),
                        pltpu.VMEM((n_lb, tm, LANES), F32),
                        pltpu.VMEM((2, min(FFN_ROW_SUB, tm) + SUBLANES, min(MXU_COLS, tn)), F32)],
        compiler_params=_cparams(2),
        name="ffn_up",
    )(*args, *extra)


def _ret_body(q_ref, k_ref, v_ref, g_ref, cos_ref, sin_ref, dm_ref, xi_ref, zeta_ref, cdec_ref, s0_ref,
              *rest, n_seq, chunk, n_chunks, scale):
    o_ref, so_ref, s_scr, q_scr, kz_scr, o_scr = rest[-6:]
    n = pl.program_id(2)

    @pl.when(n == 0)
    def _():
        s_scr[...] = s0_ref[...]

    n_heads = dm_ref.shape[0]
    dk = q_ref.shape[1] // n_heads
    half = dk // 2
    cos = cos_ref[...]
    sin = sin_ref[...]

    def rot(x):
        x1 = x[:, :half]
        x2 = x[:, half:]
        return jnp.concatenate([x1 * cos - x2 * sin, x2 * cos + x1 * sin], axis=-1)

    for hh in range(n_heads):
        cols = slice(hh * dk, (hh + 1) * dk)
        q = rot(q_ref[:, cols])
        k = rot(k_ref[:, cols]) * scale
        vb = v_ref[:, cols].astype(BF16)
        inner = _dot_nt(q, k) * dm_ref[hh]
        o_scr[hh] = _dot(inner, vb)
        q_scr[hh] = q
        kz_scr[hh] = k * zeta_ref[hh]
        cdec = cdec_ref[hh]
        for gi in range(n_seq):
            rows = pl.ds(gi * chunk, chunk)
            s = s_scr[gi, hh]
            o_scr[hh, rows, :] += _dot(q_scr[hh, rows, :], s) * xi_ref[hh, rows, :]
            s_scr[gi, hh] = s * cdec + _dot_tn(kz_scr[hh, rows, :], v_ref[rows, cols])
        o = o_scr[hh]
        gate = g_ref[:, cols]
        o = o * lax.rsqrt(jnp.mean(o * o, axis=-1, keepdims=True) + NORM_EPS) * _silu(gate)
        o_ref[:, cols] = o.astype(o_ref.dtype)

    @pl.when(n == n_chunks - 1)
    def _():
        so_ref[...] = s_scr[...]


def _ret_tables(n_heads, chunk, n_seq):
    lg = jnp.log(1.0 - 2.0 ** (-5.0 - jnp.arange(n_heads, dtype=F32)))
    idx = jnp.arange(chunk, dtype=F32)
    diff = idx[:, None] - idx[None, :]
    dmask = jnp.where(diff[None] >= 0, jnp.exp(jnp.maximum(diff, 0.0)[None] * lg[:, None, None]), 0.0)
    eye = jnp.eye(n_seq, dtype=F32)
    dm = jnp.einsum('ab,hts->hatbs', eye, dmask).reshape(n_heads, n_seq * chunk, n_seq * chunk)
    xi = jnp.tile(jnp.exp((idx[None, :] + 1.0) * lg[:, None]), (1, n_seq))[:, :, None]
    zeta = jnp.tile(jnp.exp((chunk - 1.0 - idx[None, :]) * lg[:, None]), (1, n_seq))[:, :, None]
    cdec = jnp.exp(chunk * lg)[:, None, None]
    return dm, xi, zeta, cdec


def _rope_tables(pos, half):
    inv = ROPE_BASE ** (-jnp.arange(half, dtype=F32) / half)
    ang = pos.astype(F32)[:, None] * inv[None, :]
    return jnp.cos(ang), jnp.sin(ang)


def _stacked_out(n_layers, layer, shape, block, index_map, prev, n_inputs, out_index=1):
    sds = jax.ShapeDtypeStruct((n_layers,) + tuple(shape), F32)
    spec = pl.BlockSpec((None,) + tuple(block), lambda *g: (layer,) + tuple(index_map(*g)))
    if prev is None:
        return sds, spec, [], [], {}
    return sds, spec, [prev], [pl.BlockSpec(memory_space=pl.ANY)], {n_inputs: out_index}


def _retention(p, s0, pos, *, layer, n_layers, prev_state, row0, n_batch, n_seq, chunk, n_chunks, n_heads, dk,
               heads_per_step):
    rows = n_seq * chunk
    assert row0 % rows == 0
    rb0 = row0 // rows
    dm, xi, zeta, cdec = _ret_tables(n_heads, chunk, n_seq)
    cos, sin = _rope_tables(pos, dk // 2)
    if n_seq > 1:
        assert n_chunks == 1
        cos = jnp.tile(cos, (n_seq, 1))
        sin = jnp.tile(sin, (n_seq, 1))
    hps = heads_per_step
    assert n_heads % hps == 0
    hg = n_heads // hps
    w = hps * dk

    def col(off):
        return lambda b, h, n: (rb0 + b * n_chunks + n, off * hg + h)

    tab = lambda b, h, n: (h, 0, 0)
    s0_layer = layer if s0.shape[0] > 1 else 0
    in_specs = [pl.BlockSpec((rows, w), col(0)), pl.BlockSpec((rows, w), col(1)),
                pl.BlockSpec((rows, w), col(2)), pl.BlockSpec((rows, w), col(3)),
                pl.BlockSpec((rows, dk // 2), lambda b, h, n: (n, 0)),
                pl.BlockSpec((rows, dk // 2), lambda b, h, n: (n, 0)),
                pl.BlockSpec((hps, rows, rows), tab),
                pl.BlockSpec((hps, rows, 1), tab),
                pl.BlockSpec((hps, rows, 1), tab),
                pl.BlockSpec((hps, 1, 1), tab),
                pl.BlockSpec((None, n_seq, hps, dk, dk), lambda b, h, n: (s0_layer, b, h, 0, 0))]
    args = [p, p, p, p, cos, sin, dm, xi, zeta, cdec, s0]
    so_shape, so_spec, extra, extra_specs, aliases = _stacked_out(
        n_layers, layer, (n_batch * n_seq, n_heads, dk, dk), (n_seq, hps, dk, dk),
        lambda b, h, n: (b, h, 0, 0), prev_state, len(args))
    total_rows = n_batch * n_chunks * rows
    return pl.pallas_call(
        functools.partial(_ret_body, n_seq=n_seq, chunk=chunk, n_chunks=n_chunks, scale=dk ** -0.5),
        grid=(n_batch, hg, n_chunks),
        in_specs=in_specs + extra_specs,
        out_specs=[pl.BlockSpec((rows, w), lambda b, h, n: (b * n_chunks + n, h)), so_spec],
        out_shape=[jax.ShapeDtypeStruct((total_rows, n_heads * dk), BF16), so_shape],
        input_output_aliases=aliases,
        scratch_shapes=[pltpu.VMEM((n_seq, hps, dk, dk), F32), pltpu.VMEM((hps, rows, dk), F32),
                        pltpu.VMEM((hps, rows, dk), F32), pltpu.VMEM((hps, rows, dk), F32)],
        compiler_params=_cparams(3),
        name="retention",
    )(*args, *extra)


def _rg_gates(xc, wa_ref, wx_ref, ba, bx, lam):
    nb, bw, _ = wa_ref.shape
    ra, ia = [], []
    for blk in range(nb):
        xb = xc[:, blk * bw:(blk + 1) * bw].astype(BF16)
        ra.append(jnp.dot(xb, wa_ref[blk], preferred_element_type=F32))
        ia.append(jnp.dot(xb, wx_ref[blk], preferred_element_type=F32))
    r = jax.nn.sigmoid(jnp.concatenate(ra, axis=-1) + ba)
    i = jax.nn.sigmoid(jnp.concatenate(ia, axis=-1) + bx)
    log_a = -RG_C * r * jax.nn.softplus(-lam)
    a = jnp.exp(log_a)
    y = -jnp.tanh(log_a) * (a * a + 1.0)
    mult = y * lax.rsqrt(jnp.maximum(y, TINY))
    return a, mult, i


def _scan_rows(a, b, row_in_seg, seg_len):
    s = 1
    while s < seg_len:
        a_sh = pltpu.roll(a, s, axis=0)
        b_sh = pltpu.roll(b, s, axis=0)
        m = row_in_seg >= s
        b = jnp.where(m, a * b_sh + b, b)
        a = jnp.where(m, a * a_sh, a)
        s *= 2
    return a, b


def _rg_prompt_body(xr_ref, gr_ref, cw_ref, cb_ref, wa_ref, wx_ref, ba_ref, bx_ref, lam_ref,
                    o_ref, hl_ref, xt_ref, h_scr, x_scr, *, n_tiles):
    n = pl.program_id(1)

    @pl.when(n == 0)
    def _():
        h_scr[...] = jnp.zeros_like(h_scr)
        x_scr[...] = jnp.zeros_like(x_scr)

    xr = xr_ref[...]
    tc = xr.shape[0]
    row = lax.broadcasted_iota(jnp.int32, xr.shape, 0)
    row8 = lax.broadcasted_iota(jnp.int32, (SUBLANES, xr.shape[1]), 0)
    prev = x_scr[...]
    n_tap = cw_ref.shape[0]
    xc = None
    for j in range(n_tap):
        d = n_tap - 1 - j
        if d == 0:
            xd = xr
        else:
            rolled = pltpu.roll(xr, d, axis=0)
            head = jnp.where(row8 < d, pltpu.roll(prev, d, axis=0), rolled[:SUBLANES])
            xd = jnp.concatenate([head, rolled[SUBLANES:]], axis=0)
        term = xd * cw_ref[j:j + 1, :]
        xc = term + cb_ref[...] if xc is None else xc + term
    a, mult, gate_i = _rg_gates(xc, wa_ref, wx_ref, ba_ref[...], bx_ref[...], lam_ref[...])
    mult = jnp.where(row + n * tc == 0, 1.0, mult)
    bterm = xc * gate_i * mult
    a_grp, b_grp = _scan_rows(a, bterm, row % SUBLANES, SUBLANES)
    h_prev = h_scr[...]
    groups = []
    for gi in range(tc // SUBLANES):
        rows = slice(gi * SUBLANES, (gi + 1) * SUBLANES)
        h_g = a_grp[rows] * h_prev + b_grp[rows]
        groups.append(h_g)
        h_prev = h_g[SUBLANES - 1:SUBLANES]
    h = jnp.concatenate(groups, axis=0)
    o_ref[...] = (h * jax.nn.gelu(gr_ref[...])).astype(o_ref.dtype)
    h_scr[...] = h[tc - 1:tc, :]
    x_scr[...] = xr[tc - SUBLANES:, :]

    @pl.when(n == n_tiles - 1)
    def _():
        hl_ref[...] = h[tc - 1:tc, :]
        xt_ref[...] = xr[tc - SUBLANES:, :]


def _rg_prompt(p, cw, cb, wa, wx, ba, bx, lam, *, n_batch, seq, col0, d_rg, total_rows):
    tc = min(RG_TIME_TILE, seq)
    assert seq % tc == 0 and col0 % d_rg == 0 and (tc & (tc - 1)) == 0
    nt = seq // tc
    cb0 = col0 // d_rg
    vec = lambda a: a.reshape(1, d_rg)
    full = lambda a: pl.BlockSpec(a.shape, lambda b, n: (0,) * a.ndim)
    args = (cw, vec(cb), wa, wx, vec(ba), vec(bx), vec(lam))
    return pl.pallas_call(
        functools.partial(_rg_prompt_body, n_tiles=nt),
        grid=(n_batch, nt),
        in_specs=[pl.BlockSpec((tc, d_rg), lambda b, n: (b * nt + n, cb0)),
                  pl.BlockSpec((tc, d_rg), lambda b, n: (b * nt + n, cb0 + 1))] + [full(a) for a in args],
        out_specs=[pl.BlockSpec((tc, d_rg), lambda b, n: (b * nt + n, 0)),
                   pl.BlockSpec((None, 1, d_rg), lambda b, n: (b, 0, 0)),
                   pl.BlockSpec((None, SUBLANES, d_rg), lambda b, n: (b, 0, 0))],
        out_shape=[jax.ShapeDtypeStruct((total_rows, d_rg), BF16),
                   jax.ShapeDtypeStruct((n_batch, 1, d_rg), F32),
                   jax.ShapeDtypeStruct((n_batch, SUBLANES, d_rg), F32)],
        scratch_shapes=[pltpu.VMEM((1, d_rg), F32), pltpu.VMEM((SUBLANES, d_rg), F32)],
        compiler_params=_cparams(2),
        name="rglru_prompt",
    )(p, p, *args)


def _rg_sample_body(xr_ref, gr_ref, e_ref, h0_ref, cw_ref, cb_ref, wa_ref, wx_ref, ba_ref, bx_ref, lam_ref,
                    o_ref, h_ref, *, dec_seq, first_pos):
    xr = xr_ref[...]
    row = lax.broadcasted_iota(jnp.int32, xr.shape, 0)
    t = row % dec_seq
    n_tap = cw_ref.shape[0]
    xc = None
    for j in range(n_tap):
        d = n_tap - 1 - j
        xd = xr if d == 0 else jnp.where(t >= d, pltpu.roll(xr, d, axis=0), e_ref[d - 1])
        term = xd * cw_ref[j:j + 1, :]
        xc = term + cb_ref[...] if xc is None else xc + term
    a, mult, gate_i = _rg_gates(xc, wa_ref, wx_ref, ba_ref[...], bx_ref[...], lam_ref[...])
    if first_pos == 0:
        mult = jnp.where(t == 0, 1.0, mult)
    bterm = xc * gate_i * mult + jnp.where(t == 0, a * h0_ref[...], 0.0)
    _, h = _scan_rows(a, bterm, t, dec_seq)
    h_ref[...] = h
    o_ref[...] = (h * jax.nn.gelu(gr_ref[...])).astype(o_ref.dtype)


def _rg_sample(p, e_prev, h0_rows, cw, cb, wa, wx, ba, bx, lam, *, row0, m_s, dec_seq, col0, d_rg):
    tr = min(128, m_s)
    assert m_s % tr == 0 and row0 % tr == 0 and tr % dec_seq == 0 and (dec_seq & (dec_seq - 1)) == 0
    rb0 = row0 // tr
    cb0 = col0 // d_rg
    vec = lambda a: a.reshape(1, d_rg)
    full = lambda a: pl.BlockSpec(a.shape, lambda i: (0,) * a.ndim)
    args = (cw, vec(cb), wa, wx, vec(ba), vec(bx), vec(lam))
    n_prev = e_prev.shape[0]
    return pl.pallas_call(
        functools.partial(_rg_sample_body, dec_seq=dec_seq, first_pos=PAST_LEN),
        grid=(m_s // tr,),
        in_specs=[pl.BlockSpec((tr, d_rg), lambda i: (rb0 + i, cb0)),
                  pl.BlockSpec((tr, d_rg), lambda i: (rb0 + i, cb0 + 1)),
                  pl.BlockSpec((n_prev, tr, d_rg), lambda i: (0, i, 0)),
                  pl.BlockSpec((tr, d_rg), lambda i: (i, 0))] + [full(a) for a in args],
        out_specs=[pl.BlockSpec((tr, d_rg), lambda i: (i, 0)), pl.BlockSpec((tr, d_rg), lambda i: (i, 0))],
        out_shape=[jax.ShapeDtypeStruct((m_s, d_rg), BF16), jax.ShapeDtypeStruct((m_s, d_rg), F32)],
        compiler_params=_cparams(1),
        name="rglru_sample",
    )(p, p, e_prev, h0_rows, *args)


def _lower_bound(logit_rows, layer):
    m = logit_rows[0]
    for z in logit_rows[1:]:
        m = jnp.maximum(m, z)
    e = [jnp.exp(z - m) for z in logit_rows]
    tot = e[0]
    for x in e[1:]:
        tot = tot + x
    if layer == 0:
        return jnp.zeros_like(tot)
    num = e[1]
    for x in e[2:layer + 1]:
        num = num + x
    return num / tot


_GLA_VREGS = GLA_BLOCK // SUBLANES


def _gla_level_masks():
    r = np.arange(GLA_BLOCK)
    t = r // SUBLANES + _GLA_VREGS * (r % SUBLANES)
    tq, tk = t[:, None], t[None, :]
    n_levels = GLA_BLOCK.bit_length()
    masks = np.zeros((n_levels, GLA_BLOCK, GLA_BLOCK), np.float32)
    masks[0] = tq == tk
    for lvl in range(1, n_levels):
        gs = 1 << lvl
        masks[lvl] = (tq > tk) & (tq // gs == tk // gs) & (tq // (gs // 2) != tk // (gs // 2))
    return masks


def _gla_prompt_body(q_ref, f_ref, i_ref, g_ref, lbl_ref, ng_ref, msk_ref, o_ref, so_ref, st_scr, o_scr,
                     *, heads, layer, n_blocks):
    n = pl.program_id(2)

    @pl.when(n == 0)
    def _():
        st_scr[...] = jnp.zeros_like(st_scr)

    nv = _GLA_VREGS
    sub = lax.broadcasted_iota(jnp.int32, (SUBLANES, LANES), 0)
    zero = jnp.zeros((SUBLANES, LANES), F32)
    for hh in range(heads):
        cols = slice(hh * LANES, (hh + 1) * LANES)

        def load(ref):
            return jnp.concatenate([ref[hh, pl.ds(j, SUBLANES, stride=nv), :] for j in range(nv)], axis=0)

        def groups(x):
            return [x[SUBLANES * j:SUBLANES * (j + 1)] for j in range(nv)]

        lb = _lower_bound([lbl_ref[r:r + 1, cols] for r in range(lbl_ref.shape[0])], layer)
        q = _silu(load(q_ref))
        fg = lb + (1.0 - lb) * jax.nn.sigmoid(load(f_ref))
        kk = 1.0 - fg
        v = load(i_ref)
        vb = v.astype(BF16)
        lf = groups(jnp.log(fg))

        c = [lf[0]]
        for j in range(1, nv):
            c.append(c[-1] + lf[j])
        tot = c[nv - 1]
        x = tot
        s = 1
        while s < SUBLANES:
            x = x + jnp.where(sub >= s, pltpu.roll(x, s, axis=0), 0.0)
            s *= 2
        before = x - tot
        bj = [cj + before for cj in c]
        b = jnp.concatenate(bj, axis=0)
        b_last = bj[nv - 1][SUBLANES - 1:SUBLANES, :]

        qj = groups(q)
        kj = groups(kk)
        def keep(lvl, scores, acc):
            return jnp.where(msk_ref[lvl] > 0, scores, acc)

        a_mat = keep(0, _dot_nt(q, kk), jnp.zeros((GLA_BLOCK, GLA_BLOCK), F32))
        lvl = 1
        gs = 2
        while gs <= nv:
            hs = gs // 2
            qd, kd = [], []
            for j in range(nv):
                ref = (j // gs) * gs + hs - 1
                if j % gs >= hs:
                    qd.append(qj[j] * jnp.exp(bj[j] - bj[ref]))
                    kd.append(zero)
                else:
                    kd.append(kj[j] if j == ref else kj[j] * jnp.exp(bj[ref] - bj[j]))
                    qd.append(zero)
            a_mat = keep(lvl, _dot_nt(jnp.concatenate(qd, axis=0), jnp.concatenate(kd, axis=0)), a_mat)
            lvl += 1
            gs *= 2
        m = 2
        while m <= SUBLANES:
            src = sub - sub % m + (m // 2 - 1)
            ref = zero
            for s_src in range(m // 2 - 1, SUBLANES, m):
                row_b = jnp.broadcast_to(bj[nv - 1][s_src:s_src + 1, :], (SUBLANES, LANES))
                ref = jnp.where(src == s_src, row_b, ref)
            e = [jnp.exp(-jnp.abs(bj[j] - ref)) for j in range(nv)]
            qd = jnp.concatenate([qj[j] * e[j] for j in range(nv)], axis=0)
            kd = jnp.concatenate([kj[j] * e[j] for j in range(nv)], axis=0)
            a_mat = keep(lvl, _dot_nt(qd, kd), a_mat)
            lvl += 1
            m *= 2

        st = st_scr[hh]
        o = _dot(a_mat, vb) + _dot_nt(q * jnp.exp(b), st)
        st_new = st * jnp.exp(b_last) + _dot_tn(vb, kk * jnp.exp(b_last - b))
        st_scr[hh] = st_new

        o = o * lax.rsqrt(jnp.mean(o * o, axis=-1, keepdims=True) + NORM_EPS) * ng_ref[...]
        o = o * jax.nn.sigmoid(load(g_ref))
        for j in range(nv):
            o_scr[hh, pl.ds(j, SUBLANES, stride=nv), :] = o[SUBLANES * j:SUBLANES * (j + 1)]
        o_ref[:, cols] = o_scr[hh].astype(o_ref.dtype)

    @pl.when(n == n_blocks - 1)
    def _():
        for hh in range(heads):
            so_ref[hh] = st_scr[hh].T


def _gla_prompt(p3, lb_logits, norm_g, *, layer, n_batch, seq, n_heads, total_rows):
    blk = GLA_BLOCK
    hb = GLA_HEADS_PER_STEP
    assert seq % blk == 0 and n_heads % hb == 0
    nb = seq // blk
    hg = n_heads // hb
    w = hb * LANES
    masks = jnp.asarray(_gla_level_masks())

    def part_spec(part):
        return pl.BlockSpec((hb, blk, LANES), lambda b, h, n: (part * hg + h, b * nb + n, 0))

    return pl.pallas_call(
        functools.partial(_gla_prompt_body, heads=hb, layer=layer, n_blocks=nb),
        grid=(n_batch, hg, nb),
        in_specs=[part_spec(part) for part in range(4)] + [
                  pl.BlockSpec((lb_logits.shape[0], w), lambda b, h, n: (0, h)),
                  pl.BlockSpec((1, LANES), lambda b, h, n: (0, 0)),
                  pl.BlockSpec(masks.shape, lambda b, h, n: (0, 0, 0))],
        out_specs=[pl.BlockSpec((blk, w), lambda b, h, n: (b * nb + n, h)),
                   pl.BlockSpec((None, hb, LANES, LANES), lambda b, h, n: (b, h, 0, 0))],
        out_shape=[jax.ShapeDtypeStruct((total_rows, n_heads * LANES), BF16),
                   jax.ShapeDtypeStruct((n_batch, n_heads, LANES, LANES), F32)],
        scratch_shapes=[pltpu.VMEM((hb, LANES, LANES), F32), pltpu.VMEM((hb, blk, LANES), F32)],
        compiler_params=_cparams(3),
        name="hgrn_prompt",
    )(p3, p3, p3, p3, lb_logits, norm_g.reshape(1, LANES), masks)


def _gla_sample_body(q_ref, f_ref, i_ref, g_ref, lbl_ref, ng_ref, s0_ref, *rest, layer):
    o_ref, so_ref, qd_scr, kd_scr, o_scr = rest[-5:]
    n_g, n_t, n_h, _ = q_ref.shape
    lb = _lower_bound([lbl_ref[r] for r in range(lbl_ref.shape[0])], layer)
    q = _silu(q_ref[...])
    fg = lb + (1.0 - lb) * jax.nn.sigmoid(f_ref[...])
    kk = 1.0 - fg
    lf = jnp.log(fg)
    v = i_ref[...]
    bt = [lf[:, 0]]
    for t in range(1, n_t):
        bt.append(bt[-1] + lf[:, t])
    b_last = bt[n_t - 1]
    for t in range(n_t):
        acc = None
        for s in range(t + 1):
            w = q[:, t] * kk[:, s]
            if s < t:
                w = w * jnp.exp(bt[t] - bt[s])
            term = jnp.sum(w, axis=-1, keepdims=True) * v[:, s]
            acc = term if acc is None else acc + term
        o_scr[:, t] = acc
        qd_scr[:, t] = q[:, t] * jnp.exp(bt[t])
        kd_scr[:, t] = kk[:, t] * jnp.exp(b_last - bt[t])
    e_last = jnp.exp(b_last)
    for gi in range(n_g):
        e_cols = e_last[gi].T
        for h in range(n_h):
            s = s0_ref[gi, h]
            o_scr[gi, :, h, :] += _dot(qd_scr[gi, :, h, :], s)
            so_ref[gi, h] = s * e_cols[:, h:h + 1] + _dot_tn(kd_scr[gi, :, h, :], i_ref[gi, :, h, :])
    o = o_scr[...]
    o = o * lax.rsqrt(jnp.mean(o * o, axis=-1, keepdims=True) + NORM_EPS) * ng_ref[...]
    o_ref[...] = (o * jax.nn.sigmoid(g_ref[...])).astype(o_ref.dtype)


def _gla_sample(p4, lb_logits, norm_g, s0, *, layer, prev_state, seq0):
    _, td, h4, _ = p4.shape
    bd = s0.shape[1]
    nh = h4 // 4
    g = min(GLA_SAMPLE_GROUP, bd)
    assert bd % g == 0 and seq0 % g == 0
    blk = (g, td, nh, LANES)
    sblk = (g, nh, LANES, LANES)
    lb3 = lb_logits.reshape(lb_logits.shape[0], nh, LANES)
    part = lambda off: pl.BlockSpec(blk, lambda i: (seq0 // g + i, 0, off, 0))
    in_specs = [part(0), part(1), part(2), part(3),
                pl.BlockSpec(lb3.shape, lambda i: (0, 0, 0)),
                pl.BlockSpec((1, LANES), lambda i: (0, 0)),
                pl.BlockSpec((None,) + sblk, lambda i: (layer, i, 0, 0, 0))]
    args = [p4, p4, p4, p4, lb3, norm_g.reshape(1, LANES), s0]
    so_shape, so_spec, extra, extra_specs, aliases = _stacked_out(
        s0.shape[0], layer, (bd, nh, LANES, LANES), sblk, lambda i: (i, 0, 0, 0), prev_state, len(args))
    return pl.pallas_call(
        functools.partial(_gla_sample_body, layer=layer),
        grid=(bd // g,),
        in_specs=in_specs + extra_specs,
        out_specs=[pl.BlockSpec(blk, lambda i: (i, 0, 0, 0)), so_spec],
        out_shape=[jax.ShapeDtypeStruct((bd, td, nh, LANES), BF16), so_shape],
        input_output_aliases=aliases,
        scratch_shapes=[pltpu.VMEM(blk, F32), pltpu.VMEM(blk, F32), pltpu.VMEM(blk, F32)],
        compiler_params=_cparams(1),
        name="hgrn_sample",
    )(*args, *extra)


def _prev_rows(buf, dec_seq):
    bd, n_prev, d = buf.shape
    outs = []
    for dd in range(1, n_prev + 1):
        rows = [buf[:, n_prev - dd + t] if t < dd else jnp.zeros((bd, d), buf.dtype) for t in range(dec_seq)]
        outs.append(jnp.stack(rows, axis=1).reshape(bd * dec_seq, d))
    return outs


def _first_rows(vals, dec_seq):
    bd, d = vals.shape
    z = jnp.zeros((bd, dec_seq - 1, d), vals.dtype)
    return jnp.concatenate([vals[:, None, :], z], axis=1).reshape(bd * dec_seq, d)


def kernel(x_prompt, x_sample, state_ret, state_rglru_h, state_rglru_conv, state_hgrn, state_ffn_conv,
           ev_w_in, ev_w_out, ev_rg_conv_w, ev_rg_conv_b, ev_rg_wa, ev_rg_ba, ev_rg_wx, ev_rg_bx, ev_rg_lambda,
           od_w_in, od_w_out, od_norm_g, od_lb_logits, ln_g, ln_b, ffn_w_up, ffn_conv_w, ffn_conv_b, ffn_w_down):
    bp, tp, d_model = x_prompt.shape
    bd, td, _ = x_sample.shape
    depth = ln_g.shape[0]
    m_p, m_s = bp * tp, bd * td
    m = m_p + m_s
    alpha = (2.0 * depth) ** 0.25
    h_ret, dk_ret = state_ret.shape[2], state_ret.shape[3]
    d_ret = h_ret * dk_ret
    d_rg = state_rglru_h.shape[-1]
    h_hg = state_hgrn.shape[2]
    d_ff = ffn_conv_b.shape[-1]
    assert state_hgrn.shape[3] == LANES and state_hgrn.shape[4] == LANES and d_ret == d_rg
    assert ffn_conv_w.shape[1] == 3 and td >= 3

    x = (x_prompt.reshape(m_p, d_model), x_sample.reshape(m_s, d_model))
    xb = jnp.concatenate(x, axis=0).astype(BF16)
    pos_p = jnp.arange(tp, dtype=jnp.int32)
    pos_s = PAST_LEN + jnp.arange(td, dtype=jnp.int32)
    zero_ret = jnp.zeros((1, bp) + state_ret.shape[2:], F32)
    n_even = state_ret.shape[0]
    w_out_ev = ev_w_out.astype(BF16)
    w_out_od = od_w_out.astype(BF16)
    w_down = ffn_w_down.astype(BF16)

    ret_p = ret_s = hg_s = ff_s = None
    n_h_p, n_h_s, n_cv_p, n_cv_s, n_hg_p, n_ff_p = [], [], [], [], [], []
    tm_ff = min(ROW_TILE, m_s)
    tm_ln = min(LN_ROW_TILE, m_s)
    tm_mix = min(MIX_LN_ROW_TILE, m_s)
    tm_in = min(IN_PROJ_ROW_TILE, m_s)
    for l in range(depth):
        if l % 2 == 0:
            e = l // 2
            p = _matmul(xb, ev_w_in, e, tm_in, IN_PROJ_COLS)
            chunk = RET_CHUNK if tp % RET_CHUNK == 0 else tp
            o_ret, ret_p = _retention(p, zero_ret, pos_p, layer=e, n_layers=n_even, prev_state=ret_p, row0=0,
                                      n_batch=bp, n_seq=1, chunk=chunk, n_chunks=tp // chunk, n_heads=h_ret,
                                      dk=dk_ret, heads_per_step=h_ret)
            g_ret = min(RET_SAMPLE_GROUP, bd)
            o_ret_s, ret_s = _retention(p, state_ret, pos_s, layer=e, n_layers=n_even, prev_state=ret_s,
                                        row0=m_p, n_batch=bd // g_ret, n_seq=g_ret, chunk=td, n_chunks=1,
                                        n_heads=h_ret, dk=dk_ret, heads_per_step=1)
            wa = ev_rg_wa[e].astype(BF16)
            wx = ev_rg_wx[e].astype(BF16)
            rg_args = (ev_rg_conv_w[e], ev_rg_conv_b[e], wa, wx, ev_rg_ba[e], ev_rg_bx[e], ev_rg_lambda[e])
            o_rg, hl_p, xt_p = _rg_prompt(p, *rg_args, n_batch=bp, seq=tp, col0=4 * d_ret, d_rg=d_rg,
                                          total_rows=m_p)
            e_prev = jnp.stack(_prev_rows(state_rglru_conv[e], td))
            o_rg_s, h_s = _rg_sample(p, e_prev, _first_rows(state_rglru_h[e], td), *rg_args,
                                     row0=m_p, m_s=m_s, dec_seq=td, col0=4 * d_ret, d_rg=d_rg)
            n_conv = state_rglru_conv.shape[2]
            xr_s = p[m_p:, 4 * d_ret:4 * d_ret + d_rg].reshape(bd, td, d_rg)
            n_h_p.append(hl_p[:, 0])
            n_h_s.append(h_s.reshape(bd, td, d_rg)[:, td - 1])
            n_cv_p.append(xt_p[:, SUBLANES - n_conv:])
            n_cv_s.append(xr_s[:, td - n_conv:])
            x_new, xb = _proj_ln([(o_ret, o_ret_s), (o_rg, o_rg_s)], w_out_ev, e, x, ln_g[l, 0], ln_b[l, 0],
                                 alpha, tm_mix, m_p)
        else:
            o = l // 2
            p3 = _matmul(xb, od_w_in, o, tm_in, IN_PROJ_COLS, column_major=True)
            o_hg, g_p = _gla_prompt(p3, od_lb_logits, od_norm_g[o], layer=o, n_batch=bp, seq=tp,
                                    n_heads=h_hg, total_rows=m_p)
            p4 = jnp.swapaxes(p3[:, m_p:], 0, 1).reshape(bd, td, 4 * h_hg, LANES)
            o_hg_s, hg_s = _gla_sample(p4, od_lb_logits, od_norm_g[o], state_hgrn, layer=o, prev_state=hg_s,
                                       seq0=0)
            n_hg_p.append(g_p)
            x_new, xb = _proj_ln([(o_hg, o_hg_s.reshape(m_s, h_hg * LANES))], w_out_od, o, x, ln_g[l, 0],
                                 ln_b[l, 0], alpha, tm_mix, m_p)
        x = (x_new,)
        h, tails, ff_s = _ffn_up(xb, ffn_w_up, l, ffn_conv_w[l], ffn_conv_b[l], state_ffn_conv, ff_s,
                                 m_prompt=m_p, seq=tp, dec_seq=td, tm=tm_ff, tn=512)
        tiles_per_seq = tp // tm_ff
        tails = tails.reshape(m // tm_ff, SUBLANES, d_ff)[:bp * tiles_per_seq]
        tails = tails.reshape(bp, tiles_per_seq, SUBLANES, d_ff)
        n_ff_p.append(tails[:, tiles_per_seq - 1, SUBLANES - 2:])
        last = l == depth - 1
        x_new, xb = _proj_ln([(h,)], w_down, l, x, ln_g[l, 1], ln_b[l, 1], alpha, tm_ln, m_p, split_out=last)
        x = (x_new,)

    y_prompt = x_new.reshape(bp, tp, d_model)
    y_sample = xb.reshape(bd, td, d_model)
    return (y_prompt, y_sample, ret_p, ret_s, jnp.stack(n_h_p), jnp.stack(n_h_s),
            jnp.stack(n_cv_p), jnp.stack(n_cv_s), jnp.stack(n_hg_p), hg_s,
            jnp.stack(n_ff_p), ff_s)
```

```python
import functools

import numpy as np
import jax
import jax.numpy as jnp
from jax import lax
from jax.experimental import pallas as pl
from jax.experimental.pallas import tpu as pltpu

F32 = jnp.float32
BF16 = jnp.bfloat16

LN_EPS = 1e-5
NORM_EPS = 1e-6
ROPE_BASE = 10000.0
RG_C = 8.0
TINY = 1e-37
GELU_C0 = 0.7978845608028654
GELU_C1 = 0.044715
PAST_LEN = 16384
LANES = 128
SUBLANES = 8
FFN_COL_SUB = 512
ROW_TILE = 512
FFN_ROW_SUB = 128
LN_ROW_TILE = 256
MIX_LN_ROW_TILE = 512
IN_PROJ_ROW_TILE = 512
IN_PROJ_COLS = 2048
RET_CHUNK = 256
RET_SAMPLE_GROUP = 8
RG_TIME_TILE = 512
GLA_BLOCK = 128
GLA_HEADS_PER_STEP = 16
GLA_SAMPLE_GROUP = 4
MIB = 1024 * 1024


def _cparams(n_axes, vmem_mib=48):
    return pltpu.CompilerParams(dimension_semantics=("arbitrary",) * n_axes,
                                vmem_limit_bytes=vmem_mib * MIB)


def _dot(a, b):
    return jnp.dot(a.astype(BF16), b.astype(BF16), preferred_element_type=F32)


def _dot_nt(a, b):
    return lax.dot_general(a.astype(BF16), b.astype(BF16), (((1,), (1,)), ((), ())),
                           preferred_element_type=F32)


def _dot_tn(a, b):
    return lax.dot_general(a.astype(BF16), b.astype(BF16), (((0,), (0,)), ((), ())),
                           preferred_element_type=F32)


def _silu(x):
    return x * jax.nn.sigmoid(x)


def _mm_body(x_ref, w_ref, o_ref, wb_scr):
    @pl.when(pl.program_id(1) == 0)
    def _():
        wb_scr[...] = w_ref[...].astype(BF16)

    res = jnp.dot(x_ref[...], wb_scr[...], preferred_element_type=F32)
    if len(o_ref.shape) == 2:
        o_ref[...] = res
    else:
        for c in range(o_ref.shape[0]):
            o_ref[c] = res[:, c * LANES:(c + 1) * LANES]


def _matmul(xb, w, layer, tm, tn, column_major=False):
    m, k = xb.shape
    n = w.shape[2]
    assert m % tm == 0 and n % tn == 0 and tn % LANES == 0
    if column_major:
        out_spec = pl.BlockSpec((tn // LANES, tm, LANES), lambda j, i: (j, i, 0))
        out_shape = jax.ShapeDtypeStruct((n // LANES, m, LANES), F32)
    else:
        out_spec = pl.BlockSpec((tm, tn), lambda j, i: (i, j))
        out_shape = jax.ShapeDtypeStruct((m, n), F32)
    return pl.pallas_call(
        _mm_body,
        grid=(n // tn, m // tm),
        in_specs=[pl.BlockSpec((tm, k), lambda j, i: (i, 0)),
                  pl.BlockSpec((None, k, tn), lambda j, i: (layer, 0, j))],
        out_specs=out_spec,
        out_shape=out_shape,
        scratch_shapes=[pltpu.VMEM((k, tn), BF16)],
        compiler_params=_cparams(2, vmem_mib=56),
        name="in_proj",
    )(xb, w)


def _proj_ln_body(*refs, part_arity, res_arity, alpha, n_prompt_tiles, split_out):
    i = pl.program_id(0)
    n_in = sum(part_arity)
    a_refs = refs[:n_in]
    w_refs = refs[n_in:n_in + len(part_arity)]
    x_refs = refs[n_in + len(part_arity):][:res_arity]
    g_ref, b_ref, o1_ref, o2_ref = refs[n_in + len(part_arity) + res_arity:]

    def rows_of(group):
        if len(group) == 1:
            return group[0][...]
        return jnp.where(i < n_prompt_tiles, group[0][...], group[1][...])

    acc = None
    pos = 0
    for arity, w_ref in zip(part_arity, w_refs):
        d = jnp.dot(rows_of(a_refs[pos:pos + arity]), w_ref[...], preferred_element_type=F32)
        pos += arity
        acc = d if acc is None else acc + d
    y = alpha * rows_of(x_refs) + acc
    mu = jnp.mean(y, axis=-1, keepdims=True)
    d = y - mu
    var = jnp.mean(d * d, axis=-1, keepdims=True)
    out = d * lax.rsqrt(var + LN_EPS) * g_ref[...] + b_ref[...]
    if split_out:
        @pl.when(i < n_prompt_tiles)
        def _():
            o1_ref[...] = out

        @pl.when(i >= n_prompt_tiles)
        def _():
            o2_ref[...] = out
    else:
        o1_ref[...] = out
        o2_ref[...] = out.astype(BF16)


def _proj_ln(parts, wb, layer, x, g, b, alpha, tm, m_prompt, split_out=False):
    m = sum(a.shape[0] for a in x)
    d = x[0].shape[1]
    assert m % tm == 0 and m_prompt % tm == 0
    npt = m_prompt // tm
    kp = parts[0][0].shape[1]
    assert all(a.shape[1] == kp for p in parts for a in p) and wb.shape[1] == kp * len(parts)
    prompt_rows = lambda i: (jnp.minimum(i, npt - 1), 0)
    sample_rows = lambda i: (jnp.maximum(i - npt, 0), 0)

    def row_specs(group, width):
        if len(group) == 1:
            return [pl.BlockSpec((tm, width), lambda i: (i, 0))]
        return [pl.BlockSpec((tm, width), prompt_rows), pl.BlockSpec((tm, width), sample_rows)]

    in_specs, args = [], []
    for p in parts:
        args += list(p)
        in_specs += row_specs(p, kp)
    for part in range(len(parts)):
        in_specs.append(pl.BlockSpec((None, kp, d), lambda i, part=part: (layer, part, 0),
                                     pipeline_mode=pl.Buffered(1)))
    in_specs += row_specs(x, d)
    in_specs += [pl.BlockSpec((1, d), lambda i: (0, 0)),
                 pl.BlockSpec((1, d), lambda i: (0, 0))]
    if split_out:
        out_specs = [pl.BlockSpec((tm, d), prompt_rows), pl.BlockSpec((tm, d), sample_rows)]
        out_shape = [jax.ShapeDtypeStruct((m_prompt, d), F32), jax.ShapeDtypeStruct((m - m_prompt, d), F32)]
    else:
        out_specs = [pl.BlockSpec((tm, d), lambda i: (i, 0)), pl.BlockSpec((tm, d), lambda i: (i, 0))]
        out_shape = [jax.ShapeDtypeStruct((m, d), F32), jax.ShapeDtypeStruct((m, d), BF16)]
    return pl.pallas_call(
        functools.partial(_proj_ln_body, part_arity=tuple(len(p) for p in parts), res_arity=len(x), alpha=alpha,
                          n_prompt_tiles=npt, split_out=split_out),
        grid=(m // tm,),
        in_specs=in_specs,
        out_specs=out_specs,
        out_shape=out_shape,
        compiler_params=_cparams(1, vmem_mib=56),
        name="proj_ln",
    )(*args, *([wb] * len(parts)), *x, g.reshape(1, d), b.reshape(1, d))


def _ffn_up_body(x_ref, wu_ref, wv_ref, cw_ref, cb_ref, buf_ref, *rest,
                 n_prompt_tiles, tiles_per_seq, dec_seq):
    h_ref, tail_ref, fs_ref, wub_scr, wvb_scr, carry_ref, e1_scr, e2_scr, us_scr, uh_scr = rest[-10:]
    i = pl.program_id(1)

    @pl.when(i == 0)
    def _():
        wub_scr[...] = wu_ref[...].astype(BF16)
        wvb_scr[...] = wv_ref[...].astype(BF16)

    tm = x_ref.shape[0]
    tn = h_ref.shape[1]
    cw = min(FFN_COL_SUB, tn)
    rs = min(FFN_ROW_SUB, tm)
    lane_blocks = cw // LANES
    row = lax.broadcasted_iota(jnp.int32, (rs, cw), 0)

    def dots(c, r):
        cols = slice(c * cw, (c + 1) * cw)
        x = x_ref[r * rs:(r + 1) * rs, :]
        return (jnp.dot(x, wub_scr[:, cols], preferred_element_type=F32),
                jnp.dot(x, wvb_scr[:, cols], preferred_element_type=F32))

    def epilogue(c, r, u, u1, u2, v):
        cols = slice(c * cw, (c + 1) * cw)
        uc = u2 * cw_ref[0:1, cols] + cb_ref[:, cols]
        uc = uc + u1 * cw_ref[1:2, cols]
        uc = uc + u * cw_ref[2:3, cols]
        th = jnp.tanh(uc * (GELU_C0 + (GELU_C0 * GELU_C1) * (uc * uc)))
        h_ref[r * rs:(r + 1) * rs, cols] = ((uc + uc * th) * (0.5 * v)).astype(h_ref.dtype)

    @pl.when(i < n_prompt_tiles)
    def _():
        first = i % tiles_per_seq == 0
        n_r = tm // rs
        for c in range(tn // cw):
            cols = slice(c * cw, (c + 1) * cw)
            prev = jnp.where(first, 0.0, carry_ref[:, cols])
            for r in range(n_r):
                slot = (c * n_r + r) % 2
                u, v = dots(c, r)
                uh_scr[slot, 0:SUBLANES, :] = prev
                uh_scr[slot, SUBLANES:, :] = u
                u1 = uh_scr[slot, SUBLANES - 1:SUBLANES - 1 + rs, :]
                u2 = uh_scr[slot, SUBLANES - 2:SUBLANES - 2 + rs, :]
                epilogue(c, r, u, u1, u2, v)
                prev = u[rs - SUBLANES:]
            carry_ref[:, cols] = prev
            tail_ref[:, cols] = prev

    @pl.when(i >= n_prompt_tiles)
    def _():
        t = row % dec_seq
        n_seq = tm // dec_seq
        e1_scr[...] = jnp.zeros_like(e1_scr)
        e2_scr[...] = jnp.zeros_like(e2_scr)
        for lb in range(tn // LANES):
            lanes = slice(lb * LANES, (lb + 1) * LANES)
            e1_scr[lb, pl.ds(0, n_seq, stride=dec_seq), :] = buf_ref[:, 1, lanes]
            e2_scr[lb, pl.ds(0, n_seq, stride=dec_seq), :] = buf_ref[:, 0, lanes]
            e2_scr[lb, pl.ds(1, n_seq, stride=dec_seq), :] = buf_ref[:, 1, lanes]
        for c in range(tn // cw):
            for r in range(tm // rs):
                rows = slice(r * rs, (r + 1) * rs)
                lbs = range(c * lane_blocks, (c + 1) * lane_blocks)
                e1 = jnp.concatenate([e1_scr[lb, rows, :] for lb in lbs], axis=1)
                e2 = jnp.concatenate([e2_scr[lb, rows, :] for lb in lbs], axis=1)
                u, v = dots(c, r)
                epilogue(c, r, u, jnp.where(t >= 1, pltpu.roll(u, 1, axis=0), e1),
                         jnp.where(t >= 2, pltpu.roll(u, 2, axis=0), e2), v)
                for k, lb in enumerate(lbs):
                    us_scr[lb, rows, :] = u[:, k * LANES:(k + 1) * LANES]
        for lb in range(tn // LANES):
            lanes = slice(lb * LANES, (lb + 1) * LANES)
            fs_ref[:, 0, lanes] = us_scr[lb, pl.ds(dec_seq - 2, n_seq, stride=dec_seq), :]
            fs_ref[:, 1, lanes] = us_scr[lb, pl.ds(dec_seq - 1, n_seq, stride=dec_seq), :]
        tail_ref[...] = jnp.zeros_like(tail_ref)


def _ffn_up(xb, w_up, layer, cw, cb, conv_state, prev_state, *, m_prompt, seq, dec_seq, tm, tn):
    m, k = xb.shape
    d_ff = w_up.shape[2] // 2
    n_layers, bd = conv_state.shape[:2]
    m_s = m - m_prompt
    assert m_prompt % tm == 0 and m_s % tm == 0 and seq % tm == 0 and d_ff % tn == 0
    assert tm % dec_seq == 0 and dec_seq >= 2 and cw.shape[0] == 3 and conv_state.shape[2] == 2
    assert tn % min(FFN_COL_SUB, tn) == 0 and tm % min(FFN_ROW_SUB, tm) == 0 and m_s == bd * dec_seq
    npt = m_prompt // tm
    nj = d_ff // tn
    n_seq = tm // dec_seq
    in_specs = [pl.BlockSpec((tm, k), lambda j, i: (i, 0)),
                pl.BlockSpec((None, k, tn), lambda j, i: (layer, 0, j)),
                pl.BlockSpec((None, k, tn), lambda j, i: (layer, 0, j + nj)),
                pl.BlockSpec((3, tn), lambda j, i: (0, j)),
                pl.BlockSpec((1, tn), lambda j, i: (0, j)),
                pl.BlockSpec((None, n_seq, 2, tn), lambda j, i: (layer, jnp.maximum(i - npt, 0), 0, j))]
    args = [xb, w_up, w_up, cw, cb.reshape(1, d_ff), conv_state]
    fs_shape, fs_spec, extra, extra_specs, aliases = _stacked_out(
        n_layers, layer, (bd, 2, d_ff), (n_seq, 2, tn), lambda j, i: (jnp.maximum(i - npt, 0), 0, j),
        prev_state, len(args), out_index=2)
    n_lb = tn // LANES
    return pl.pallas_call(
        functools.partial(_ffn_up_body, n_prompt_tiles=npt, tiles_per_seq=seq // tm, dec_seq=dec_seq),
        grid=(nj, m // tm),
        in_specs=in_specs + extra_specs,
        out_specs=[pl.BlockSpec((tm, tn), lambda j, i: (i, j)),
                   pl.BlockSpec((SUBLANES, tn), lambda j, i: (i, j)),
                   fs_spec],
        out_shape=[jax.ShapeDtypeStruct((m, d_ff), BF16),
                   jax.ShapeDtypeStruct((m // tm * SUBLANES, d_ff), F32),
                   fs_shape],
        input_output_aliases=aliases,
        scratch_shapes=[pltpu.VMEM((k, tn), BF16), pltpu.VMEM((k, tn), BF16), pltpu.VMEM((SUBLANES, tn), F32),
                        pltpu.VMEM((n_lb, tm, LANES), F32), pltpu.VMEM((n_lb, tm, LANES), F32),
                        pltpu.VMEM((n_lb, tm, LANES), F32),
                        pltpu.VMEM((2, min(FFN_ROW_SUB, tm) + SUBLANES, min(FFN_COL_SUB, tn)), F32)],
        compiler_params=_cparams(2),
        name="ffn_up",
    )(*args, *extra)


def _ret_body(q_ref, k_ref, v_ref, g_ref, cos_ref, sin_ref, dm_ref, xi_ref, zeta_ref, cdec_ref, s0_ref,
              *rest, n_seq, chunk, n_chunks, scale):
    o_ref, so_ref, s_scr, q_scr, kz_scr, o_scr = rest[-6:]
    n = pl.program_id(2)

    @pl.when(n == 0)
    def _():
        s_scr[...] = s0_ref[...]

    n_heads = dm_ref.shape[0]
    dk = q_ref.shape[1] // n_heads
    half = dk // 2
    cos = cos_ref[...]
    sin = sin_ref[...]

    def rot(x):
        x1 = x[:, :half]
        x2 = x[:, half:]
        return jnp.concatenate([x1 * cos - x2 * sin, x2 * cos + x1 * sin], axis=-1)

    for hh in range(n_heads):
        cols = slice(hh * dk, (hh + 1) * dk)
        q = rot(q_ref[:, cols])
        k = rot(k_ref[:, cols]) * scale
        vb = v_ref[:, cols].astype(BF16)
        inner = _dot_nt(q, k) * dm_ref[hh]
        o_scr[hh] = _dot(inner, vb)
        q_scr[hh] = q
        kz_scr[hh] = k * zeta_ref[hh]
        cdec = cdec_ref[hh]
        for gi in range(n_seq):
            rows = pl.ds(gi * chunk, chunk)
            s = s_scr[gi, hh]
            o_scr[hh, rows, :] += _dot(q_scr[hh, rows, :], s) * xi_ref[hh, rows, :]
            s_scr[gi, hh] = s * cdec + _dot_tn(kz_scr[hh, rows, :], v_ref[rows, cols])
        o = o_scr[hh]
        gate = g_ref[:, cols]
        o = o * lax.rsqrt(jnp.mean(o * o, axis=-1, keepdims=True) + NORM_EPS) * _silu(gate)
        o_ref[:, cols] = o.astype(o_ref.dtype)

    @pl.when(n == n_chunks - 1)
    def _():
        so_ref[...] = s_scr[...]


def _ret_tables(n_heads, chunk, n_seq):
    lg = jnp.log(1.0 - 2.0 ** (-5.0 - jnp.arange(n_heads, dtype=F32)))
    idx = jnp.arange(chunk, dtype=F32)
    diff = idx[:, None] - idx[None, :]
    dmask = jnp.where(diff[None] >= 0, jnp.exp(jnp.maximum(diff, 0.0)[None] * lg[:, None, None]), 0.0)
    eye = jnp.eye(n_seq, dtype=F32)
    dm = jnp.einsum('ab,hts->hatbs', eye, dmask).reshape(n_heads, n_seq * chunk, n_seq * chunk)
    xi = jnp.tile(jnp.exp((idx[None, :] + 1.0) * lg[:, None]), (1, n_seq))[:, :, None]
    zeta = jnp.tile(jnp.exp((chunk - 1.0 - idx[None, :]) * lg[:, None]), (1, n_seq))[:, :, None]
    cdec = jnp.exp(chunk * lg)[:, None, None]
    return dm, xi, zeta, cdec


def _rope_tables(pos, half):
    inv = ROPE_BASE ** (-jnp.arange(half, dtype=F32) / half)
    ang = pos.astype(F32)[:, None] * inv[None, :]
    return jnp.cos(ang), jnp.sin(ang)


def _stacked_out(n_layers, layer, shape, block, index_map, prev, n_inputs, out_index=1):
    sds = jax.ShapeDtypeStruct((n_layers,) + tuple(shape), F32)
    spec = pl.BlockSpec((None,) + tuple(block), lambda *g: (layer,) + tuple(index_map(*g)))
    if prev is None:
        return sds, spec, [], [], {}
    return sds, spec, [prev], [pl.BlockSpec(memory_space=pl.ANY)], {n_inputs: out_index}


def _retention(p, s0, pos, *, layer, n_layers, prev_state, row0, n_batch, n_seq, chunk, n_chunks, n_heads, dk,
               heads_per_step):
    rows = n_seq * chunk
    assert row0 % rows == 0
    rb0 = row0 // rows
    dm, xi, zeta, cdec = _ret_tables(n_heads, chunk, n_seq)
    cos, sin = _rope_tables(pos, dk // 2)
    if n_seq > 1:
        assert n_chunks == 1
        cos = jnp.tile(cos, (n_seq, 1))
        sin = jnp.tile(sin, (n_seq, 1))
    hps = heads_per_step
    assert n_heads % hps == 0
    hg = n_heads // hps
    w = hps * dk

    def col(off):
        return lambda b, h, n: (rb0 + b * n_chunks + n, off * hg + h)

    tab = lambda b, h, n: (h, 0, 0)
    s0_layer = layer if s0.shape[0] > 1 else 0
    in_specs = [pl.BlockSpec((rows, w), col(0)), pl.BlockSpec((rows, w), col(1)),
                pl.BlockSpec((rows, w), col(2)), pl.BlockSpec((rows, w), col(3)),
                pl.BlockSpec((rows, dk // 2), lambda b, h, n: (n, 0)),
                pl.BlockSpec((rows, dk // 2), lambda b, h, n: (n, 0)),
                pl.BlockSpec((hps, rows, rows), tab),
                pl.BlockSpec((hps, rows, 1), tab),
                pl.BlockSpec((hps, rows, 1), tab),
                pl.BlockSpec((hps, 1, 1), tab),
                pl.BlockSpec((None, n_seq, hps, dk, dk), lambda b, h, n: (s0_layer, b, h, 0, 0))]
    args = [p, p, p, p, cos, sin, dm, xi, zeta, cdec, s0]
    so_shape, so_spec, extra, extra_specs, aliases = _stacked_out(
        n_layers, layer, (n_batch * n_seq, n_heads, dk, dk), (n_seq, hps, dk, dk),
        lambda b, h, n: (b, h, 0, 0), prev_state, len(args))
    total_rows = n_batch * n_chunks * rows
    return pl.pallas_call(
        functools.partial(_ret_body, n_seq=n_seq, chunk=chunk, n_chunks=n_chunks, scale=dk ** -0.5),
        grid=(n_batch, hg, n_chunks),
        in_specs=in_specs + extra_specs,
        out_specs=[pl.BlockSpec((rows, w), lambda b, h, n: (b * n_chunks + n, h)), so_spec],
        out_shape=[jax.ShapeDtypeStruct((total_rows, n_heads * dk), BF16), so_shape],
        input_output_aliases=aliases,
        scratch_shapes=[pltpu.VMEM((n_seq, hps, dk, dk), F32), pltpu.VMEM((hps, rows, dk), F32),
                        pltpu.VMEM((hps, rows, dk), F32), pltpu.VMEM((hps, rows, dk), F32)],
        compiler_params=_cparams(3),
        name="retention",
    )(*args, *extra)


def _rg_gates(xc, wa_ref, wx_ref, ba, bx, lam):
    nb, bw, _ = wa_ref.shape
    ra, ia = [], []
    for blk in range(nb):
        xb = xc[:, blk * bw:(blk + 1) * bw].astype(BF16)
        ra.append(jnp.dot(xb, wa_ref[blk], preferred_element_type=F32))
        ia.append(jnp.dot(xb, wx_ref[blk], preferred_element_type=F32))
    r = jax.nn.sigmoid(jnp.concatenate(ra, axis=-1) + ba)
    i = jax.nn.sigmoid(jnp.concatenate(ia, axis=-1) + bx)
    log_a = -RG_C * r * jax.nn.softplus(-lam)
    a = jnp.exp(log_a)
    y = -jnp.tanh(log_a) * (a * a + 1.0)
    mult = y * lax.rsqrt(jnp.maximum(y, TINY))
    return a, mult, i


def _scan_rows(a, b, row_in_seg, seg_len):
    s = 1
    while s < seg_len:
        a_sh = pltpu.roll(a, s, axis=0)
        b_sh = pltpu.roll(b, s, axis=0)
        m = row_in_seg >= s
        b = jnp.where(m, a * b_sh + b, b)
        a = jnp.where(m, a * a_sh, a)
        s *= 2
    return a, b


def _rg_prompt_body(xr_ref, gr_ref, cw_ref, cb_ref, wa_ref, wx_ref, ba_ref, bx_ref, lam_ref,
                    o_ref, hl_ref, xt_ref, h_scr, x_scr, *, n_tiles):
    n = pl.program_id(1)

    @pl.when(n == 0)
    def _():
        h_scr[...] = jnp.zeros_like(h_scr)
        x_scr[...] = jnp.zeros_like(x_scr)

    xr = xr_ref[...]
    tc = xr.shape[0]
    row = lax.broadcasted_iota(jnp.int32, xr.shape, 0)
    row8 = lax.broadcasted_iota(jnp.int32, (SUBLANES, xr.shape[1]), 0)
    prev = x_scr[...]
    n_tap = cw_ref.shape[0]
    xc = None
    for j in range(n_tap):
        d = n_tap - 1 - j
        if d == 0:
            xd = xr
        else:
            rolled = pltpu.roll(xr, d, axis=0)
            head = jnp.where(row8 < d, pltpu.roll(prev, d, axis=0), rolled[:SUBLANES])
            xd = jnp.concatenate([head, rolled[SUBLANES:]], axis=0)
        term = xd * cw_ref[j:j + 1, :]
        xc = term + cb_ref[...] if xc is None else xc + term
    a, mult, gate_i = _rg_gates(xc, wa_ref, wx_ref, ba_ref[...], bx_ref[...], lam_ref[...])
    mult = jnp.where(row + n * tc == 0, 1.0, mult)
    bterm = xc * gate_i * mult
    a_grp, b_grp = _scan_rows(a, bterm, row % SUBLANES, SUBLANES)
    h_prev = h_scr[...]
    groups = []
    for gi in range(tc // SUBLANES):
        rows = slice(gi * SUBLANES, (gi + 1) * SUBLANES)
        h_g = a_grp[rows] * h_prev + b_grp[rows]
        groups.append(h_g)
        h_prev = h_g[SUBLANES - 1:SUBLANES]
    h = jnp.concatenate(groups, axis=0)
    o_ref[...] = (h * jax.nn.gelu(gr_ref[...])).astype(o_ref.dtype)
    h_scr[...] = h[tc - 1:tc, :]
    x_scr[...] = xr[tc - SUBLANES:, :]

    @pl.when(n == n_tiles - 1)
    def _():
        hl_ref[...] = h[tc - 1:tc, :]
        xt_ref[...] = xr[tc - SUBLANES:, :]


def _rg_prompt(p, cw, cb, wa, wx, ba, bx, lam, *, n_batch, seq, col0, d_rg, total_rows):
    tc = min(RG_TIME_TILE, seq)
    assert seq % tc == 0 and col0 % d_rg == 0 and (tc & (tc - 1)) == 0
    nt = seq // tc
    cb0 = col0 // d_rg
    vec = lambda a: a.reshape(1, d_rg)
    full = lambda a: pl.BlockSpec(a.shape, lambda b, n: (0,) * a.ndim)
    args = (cw, vec(cb), wa, wx, vec(ba), vec(bx), vec(lam))
    return pl.pallas_call(
        functools.partial(_rg_prompt_body, n_tiles=nt),
        grid=(n_batch, nt),
        in_specs=[pl.BlockSpec((tc, d_rg), lambda b, n: (b * nt + n, cb0)),
                  pl.BlockSpec((tc, d_rg), lambda b, n: (b * nt + n, cb0 + 1))] + [full(a) for a in args],
        out_specs=[pl.BlockSpec((tc, d_rg), lambda b, n: (b * nt + n, 0)),
                   pl.BlockSpec((None, 1, d_rg), lambda b, n: (b, 0, 0)),
                   pl.BlockSpec((None, SUBLANES, d_rg), lambda b, n: (b, 0, 0))],
        out_shape=[jax.ShapeDtypeStruct((total_rows, d_rg), BF16),
                   jax.ShapeDtypeStruct((n_batch, 1, d_rg), F32),
                   jax.ShapeDtypeStruct((n_batch, SUBLANES, d_rg), F32)],
        scratch_shapes=[pltpu.VMEM((1, d_rg), F32), pltpu.VMEM((SUBLANES, d_rg), F32)],
        compiler_params=_cparams(2),
        name="rglru_prompt",
    )(p, p, *args)


def _rg_sample_body(xr_ref, gr_ref, e_ref, h0_ref, cw_ref, cb_ref, wa_ref, wx_ref, ba_ref, bx_ref, lam_ref,
                    o_ref, h_ref, *, dec_seq, first_pos):
    xr = xr_ref[...]
    row = lax.broadcasted_iota(jnp.int32, xr.shape, 0)
    t = row % dec_seq
    n_tap = cw_ref.shape[0]
    xc = None
    for j in range(n_tap):
        d = n_tap - 1 - j
        xd = xr if d == 0 else jnp.where(t >= d, pltpu.roll(xr, d, axis=0), e_ref[d - 1])
        term = xd * cw_ref[j:j + 1, :]
        xc = term + cb_ref[...] if xc is None else xc + term
    a, mult, gate_i = _rg_gates(xc, wa_ref, wx_ref, ba_ref[...], bx_ref[...], lam_ref[...])
    if first_pos == 0:
        mult = jnp.where(t == 0, 1.0, mult)
    bterm = xc * gate_i * mult + jnp.where(t == 0, a * h0_ref[...], 0.0)
    _, h = _scan_rows(a, bterm, t, dec_seq)
    h_ref[...] = h
    o_ref[...] = (h * jax.nn.gelu(gr_ref[...])).astype(o_ref.dtype)


def _rg_sample(p, e_prev, h0_rows, cw, cb, wa, wx, ba, bx, lam, *, row0, m_s, dec_seq, col0, d_rg):
    tr = min(128, m_s)
    assert m_s % tr == 0 and row0 % tr == 0 and tr % dec_seq == 0 and (dec_seq & (dec_seq - 1)) == 0
    rb0 = row0 // tr
    cb0 = col0 // d_rg
    vec = lambda a: a.reshape(1, d_rg)
    full = lambda a: pl.BlockSpec(a.shape, lambda i: (0,) * a.ndim)
    args = (cw, vec(cb), wa, wx, vec(ba), vec(bx), vec(lam))
    n_prev = e_prev.shape[0]
    return pl.pallas_call(
        functools.partial(_rg_sample_body, dec_seq=dec_seq, first_pos=PAST_LEN),
        grid=(m_s // tr,),
        in_specs=[pl.BlockSpec((tr, d_rg), lambda i: (rb0 + i, cb0)),
                  pl.BlockSpec((tr, d_rg), lambda i: (rb0 + i, cb0 + 1)),
                  pl.BlockSpec((n_prev, tr, d_rg), lambda i: (0, i, 0)),
                  pl.BlockSpec((tr, d_rg), lambda i: (i, 0))] + [full(a) for a in args],
        out_specs=[pl.BlockSpec((tr, d_rg), lambda i: (i, 0)), pl.BlockSpec((tr, d_rg), lambda i: (i, 0))],
        out_shape=[jax.ShapeDtypeStruct((m_s, d_rg), BF16), jax.ShapeDtypeStruct((m_s, d_rg), F32)],
        compiler_params=_cparams(1),
        name="rglru_sample",
    )(p, p, e_prev, h0_rows, *args)


def _lower_bound(logit_rows, layer):
    m = logit_rows[0]
    for z in logit_rows[1:]:
        m = jnp.maximum(m, z)
    e = [jnp.exp(z - m) for z in logit_rows]
    tot = e[0]
    for x in e[1:]:
        tot = tot + x
    if layer == 0:
        return jnp.zeros_like(tot)
    num = e[1]
    for x in e[2:layer + 1]:
        num = num + x
    return num / tot


_GLA_VREGS = GLA_BLOCK // SUBLANES


def _gla_level_masks():
    r = np.arange(GLA_BLOCK)
    t = r // SUBLANES + _GLA_VREGS * (r % SUBLANES)
    tq, tk = t[:, None], t[None, :]
    n_levels = GLA_BLOCK.bit_length()
    masks = np.zeros((n_levels, GLA_BLOCK, GLA_BLOCK), np.float32)
    masks[0] = tq == tk
    for lvl in range(1, n_levels):
        gs = 1 << lvl
        masks[lvl] = (tq > tk) & (tq // gs == tk // gs) & (tq // (gs // 2) != tk // (gs // 2))
    return masks


def _gla_prompt_body(q_ref, f_ref, i_ref, g_ref, lbl_ref, ng_ref, msk_ref, o_ref, so_ref, st_scr, o_scr,
                     *, heads, layer, n_blocks):
    n = pl.program_id(2)

    @pl.when(n == 0)
    def _():
        st_scr[...] = jnp.zeros_like(st_scr)

    nv = _GLA_VREGS
    sub = lax.broadcasted_iota(jnp.int32, (SUBLANES, LANES), 0)
    zero = jnp.zeros((SUBLANES, LANES), F32)
    for hh in range(heads):
        cols = slice(hh * LANES, (hh + 1) * LANES)

        def load(ref):
            return jnp.concatenate([ref[hh, pl.ds(j, SUBLANES, stride=nv), :] for j in range(nv)], axis=0)

        def groups(x):
            return [x[SUBLANES * j:SUBLANES * (j + 1)] for j in range(nv)]

        lb = _lower_bound([lbl_ref[r:r + 1, cols] for r in range(lbl_ref.shape[0])], layer)
        q = _silu(load(q_ref))
        fg = lb + (1.0 - lb) * jax.nn.sigmoid(load(f_ref))
        kk = 1.0 - fg
        v = load(i_ref)
        vb = v.astype(BF16)
        lf = groups(jnp.log(fg))

        c = [lf[0]]
        for j in range(1, nv):
            c.append(c[-1] + lf[j])
        tot = c[nv - 1]
        x = tot
        s = 1
        while s < SUBLANES:
            x = x + jnp.where(sub >= s, pltpu.roll(x, s, axis=0), 0.0)
            s *= 2
        before = x - tot
        bj = [cj + before for cj in c]
        b = jnp.concatenate(bj, axis=0)
        b_last = bj[nv - 1][SUBLANES - 1:SUBLANES, :]

        qj = groups(q)
        kj = groups(kk)
        def keep(lvl, scores, acc):
            return jnp.where(msk_ref[lvl] > 0, scores, acc)

        a_mat = keep(0, _dot_nt(q, kk), jnp.zeros((GLA_BLOCK, GLA_BLOCK), F32))
        lvl = 1
        gs = 2
        while gs <= nv:
            hs = gs // 2
            qd, kd = [], []
            for j in range(nv):
                ref = (j // gs) * gs + hs - 1
                if j % gs >= hs:
                    qd.append(qj[j] * jnp.exp(bj[j] - bj[ref]))
                    kd.append(zero)
                else:
                    kd.append(kj[j] if j == ref else kj[j] * jnp.exp(bj[ref] - bj[j]))
                    qd.append(zero)
            a_mat = keep(lvl, _dot_nt(jnp.concatenate(qd, axis=0), jnp.concatenate(kd, axis=0)), a_mat)
            lvl += 1
            gs *= 2
        m = 2
        while m <= SUBLANES:
            src = sub - sub % m + (m // 2 - 1)
            ref = zero
            for s_src in range(m // 2 - 1, SUBLANES, m):
                row_b = jnp.broadcast_to(bj[nv - 1][s_src:s_src + 1, :], (SUBLANES, LANES))
                ref = jnp.where(src == s_src, row_b, ref)
            e = [jnp.exp(-jnp.abs(bj[j] - ref)) for j in range(nv)]
            qd = jnp.concatenate([qj[j] * e[j] for j in range(nv)], axis=0)
            kd = jnp.concatenate([kj[j] * e[j] for j in range(nv)], axis=0)
            a_mat = keep(lvl, _dot_nt(qd, kd), a_mat)
            lvl += 1
            m *= 2

        st = st_scr[hh]
        o = _dot(a_mat, vb) + _dot_nt(q * jnp.exp(b), st)
        st_new = st * jnp.exp(b_last) + _dot_tn(vb, kk * jnp.exp(b_last - b))
        st_scr[hh] = st_new

        o = o * lax.rsqrt(jnp.mean(o * o, axis=-1, keepdims=True) + NORM_EPS) * ng_ref[...]
        o = o * jax.nn.sigmoid(load(g_ref))
        for j in range(nv):
            o_scr[hh, pl.ds(j, SUBLANES, stride=nv), :] = o[SUBLANES * j:SUBLANES * (j + 1)]
        o_ref[:, cols] = o_scr[hh].astype(o_ref.dtype)

    @pl.when(n == n_blocks - 1)
    def _():
        for hh in range(heads):
            so_ref[hh] = st_scr[hh].T


def _gla_prompt(p3, lb_logits, norm_g, *, layer, n_batch, seq, n_heads, total_rows):
    blk = GLA_BLOCK
    hb = GLA_HEADS_PER_STEP
    assert seq % blk == 0 and n_heads % hb == 0
    nb = seq // blk
    hg = n_heads // hb
    w = hb * LANES
    masks = jnp.asarray(_gla_level_masks())

    def part_spec(part):
        return pl.BlockSpec((hb, blk, LANES), lambda b, h, n: (part * hg + h, b * nb + n, 0))

    return pl.pallas_call(
        functools.partial(_gla_prompt_body, heads=hb, layer=layer, n_blocks=nb),
        grid=(n_batch, hg, nb),
        in_specs=[part_spec(part) for part in range(4)] + [
                  pl.BlockSpec((lb_logits.shape[0], w), lambda b, h, n: (0, h)),
                  pl.BlockSpec((1, LANES), lambda b, h, n: (0, 0)),
                  pl.BlockSpec(masks.shape, lambda b, h, n: (0, 0, 0))],
        out_specs=[pl.BlockSpec((blk, w), lambda b, h, n: (b * nb + n, h)),
                   pl.BlockSpec((None, hb, LANES, LANES), lambda b, h, n: (b, h, 0, 0))],
        out_shape=[jax.ShapeDtypeStruct((total_rows, n_heads * LANES), BF16),
                   jax.ShapeDtypeStruct((n_batch, n_heads, LANES, LANES), F32)],
        scratch_shapes=[pltpu.VMEM((hb, LANES, LANES), F32), pltpu.VMEM((hb, blk, LANES), F32)],
        compiler_params=_cparams(3),
        name="hgrn_prompt",
    )(p3, p3, p3, p3, lb_logits, norm_g.reshape(1, LANES), masks)


def _gla_sample_body(q_ref, f_ref, i_ref, g_ref, lbl_ref, ng_ref, s0_ref, *rest, layer):
    o_ref, so_ref, qd_scr, kd_scr, o_scr = rest[-5:]
    n_g, n_t, n_h, _ = q_ref.shape
    lb = _lower_bound([lbl_ref[r] for r in range(lbl_ref.shape[0])], layer)
    q = _silu(q_ref[...])
    fg = lb + (1.0 - lb) * jax.nn.sigmoid(f_ref[...])
    kk = 1.0 - fg
    lf = jnp.log(fg)
    v = i_ref[...]
    bt = [lf[:, 0]]
    for t in range(1, n_t):
        bt.append(bt[-1] + lf[:, t])
    b_last = bt[n_t - 1]
    for t in range(n_t):
        acc = None
        for s in range(t + 1):
            w = q[:, t] * kk[:, s]
            if s < t:
                w = w * jnp.exp(bt[t] - bt[s])
            term = jnp.sum(w, axis=-1, keepdims=True) * v[:, s]
            acc = term if acc is None else acc + term
        o_scr[:, t] = acc
        qd_scr[:, t] = q[:, t] * jnp.exp(bt[t])
        kd_scr[:, t] = kk[:, t] * jnp.exp(b_last - bt[t])
    e_last = jnp.exp(b_last)
    for gi in range(n_g):
        e_cols = e_last[gi].T
        for h in range(n_h):
            s = s0_ref[gi, h]
            o_scr[gi, :, h, :] += _dot(qd_scr[gi, :, h, :], s)
            so_ref[gi, h] = s * e_cols[:, h:h + 1] + _dot_tn(kd_scr[gi, :, h, :], i_ref[gi, :, h, :])
    o = o_scr[...]
    o = o * lax.rsqrt(jnp.mean(o * o, axis=-1, keepdims=True) + NORM_EPS) * ng_ref[...]
    o_ref[...] = (o * jax.nn.sigmoid(g_ref[...])).astype(o_ref.dtype)


def _gla_sample(p4, lb_logits, norm_g, s0, *, layer, prev_state, seq0):
    _, td, h4, _ = p4.shape
    bd = s0.shape[1]
    nh = h4 // 4
    g = min(GLA_SAMPLE_GROUP, bd)
    assert bd % g == 0 and seq0 % g == 0
    blk = (g, td, nh, LANES)
    sblk = (g, nh, LANES, LANES)
    lb3 = lb_logits.reshape(lb_logits.shape[0], nh, LANES)
    part = lambda off: pl.BlockSpec(blk, lambda i: (seq0 // g + i, 0, off, 0))
    in_specs = [part(0), part(1), part(2), part(3),
                pl.BlockSpec(lb3.shape, lambda i: (0, 0, 0)),
                pl.BlockSpec((1, LANES), lambda i: (0, 0)),
                pl.BlockSpec((None,) + sblk, lambda i: (layer, i, 0, 0, 0))]
    args = [p4, p4, p4, p4, lb3, norm_g.reshape(1, LANES), s0]
    so_shape, so_spec, extra, extra_specs, aliases = _stacked_out(
        s0.shape[0], layer, (bd, nh, LANES, LANES), sblk, lambda i: (i, 0, 0, 0), prev_state, len(args))
    return pl.pallas_call(
        functools.partial(_gla_sample_body, layer=layer),
        grid=(bd // g,),
        in_specs=in_specs + extra_specs,
        out_specs=[pl.BlockSpec(blk, lambda i: (i, 0, 0, 0)), so_spec],
        out_shape=[jax.ShapeDtypeStruct((bd, td, nh, LANES), BF16), so_shape],
        input_output_aliases=aliases,
        scratch_shapes=[pltpu.VMEM(blk, F32), pltpu.VMEM(blk, F32), pltpu.VMEM(blk, F32)],
        compiler_params=_cparams(1),
        name="hgrn_sample",
    )(*args, *extra)


def _prev_rows(buf, dec_seq):
    bd, n_prev, d = buf.shape
    outs = []
    for dd in range(1, n_prev + 1):
        rows = [buf[:, n_prev - dd + t] if t < dd else jnp.zeros((bd, d), buf.dtype) for t in range(dec_seq)]
        outs.append(jnp.stack(rows, axis=1).reshape(bd * dec_seq, d))
    return outs


def _first_rows(vals, dec_seq):
    bd, d = vals.shape
    z = jnp.zeros((bd, dec_seq - 1, d), vals.dtype)
    return jnp.concatenate([vals[:, None, :], z], axis=1).reshape(bd * dec_seq, d)


def kernel(x_prompt, x_sample, state_ret, state_rglru_h, state_rglru_conv, state_hgrn, state_ffn_conv,
           ev_w_in, ev_w_out, ev_rg_conv_w, ev_rg_conv_b, ev_rg_wa, ev_rg_ba, ev_rg_wx, ev_rg_bx, ev_rg_lambda,
           od_w_in, od_w_out, od_norm_g, od_lb_logits, ln_g, ln_b, ffn_w_up, ffn_conv_w, ffn_conv_b, ffn_w_down):
    bp, tp, d_model = x_prompt.shape
    bd, td, _ = x_sample.shape
    depth = ln_g.shape[0]
    m_p, m_s = bp * tp, bd * td
    m = m_p + m_s
    alpha = (2.0 * depth) ** 0.25
    h_ret, dk_ret = state_ret.shape[2], state_ret.shape[3]
    d_ret = h_ret * dk_ret
    d_rg = state_rglru_h.shape[-1]
    h_hg = state_hgrn.shape[2]
    d_ff = ffn_conv_b.shape[-1]
    assert state_hgrn.shape[3] == LANES and state_hgrn.shape[4] == LANES and d_ret == d_rg
    assert ffn_conv_w.shape[1] == 3 and td >= 3

    x = (x_prompt.reshape(m_p, d_model), x_sample.reshape(m_s, d_model))
    xb = jnp.concatenate(x, axis=0).astype(BF16)
    pos_p = jnp.arange(tp, dtype=jnp.int32)
    pos_s = PAST_LEN + jnp.arange(td, dtype=jnp.int32)
    zero_ret = jnp.zeros((1, bp) + state_ret.shape[2:], F32)
    n_even = state_ret.shape[0]
    w_out_ev = ev_w_out.astype(BF16)
    w_out_od = od_w_out.astype(BF16)
    w_down = ffn_w_down.astype(BF16)

    ret_p = ret_s = hg_s = ff_s = None
    n_h_p, n_h_s, n_cv_p, n_cv_s, n_hg_p, n_ff_p = [], [], [], [], [], []
    tm_ff = min(ROW_TILE, m_s)
    tm_ln = min(LN_ROW_TILE, m_s)
    tm_mix = min(MIX_LN_ROW_TILE, m_s)
    tm_in = min(IN_PROJ_ROW_TILE, m_s)
    for l in range(depth):
        if l % 2 == 0:
            e = l // 2
            p = _matmul(xb, ev_w_in, e, tm_in, IN_PROJ_COLS)
            chunk = RET_CHUNK if tp % RET_CHUNK == 0 else tp
            o_ret, ret_p = _retention(p, zero_ret, pos_p, layer=e, n_layers=n_even, prev_state=ret_p, row0=0,
                                      n_batch=bp, n_seq=1, chunk=chunk, n_chunks=tp // chunk, n_heads=h_ret,
                                      dk=dk_ret, heads_per_step=h_ret)
            g_ret = min(RET_SAMPLE_GROUP, bd)
            o_ret_s, ret_s = _retention(p, state_ret, pos_s, layer=e, n_layers=n_even, prev_state=ret_s,
                                        row0=m_p, n_batch=bd // g_ret, n_seq=g_ret, chunk=td, n_chunks=1,
                                        n_heads=h_ret, dk=dk_ret, heads_per_step=1)
            wa = ev_rg_wa[e].astype(BF16)
            wx = ev_rg_wx[e].astype(BF16)
            rg_args = (ev_rg_conv_w[e], ev_rg_conv_b[e], wa, wx, ev_rg_ba[e], ev_rg_bx[e], ev_rg_lambda[e])
            o_rg, hl_p, xt_p = _rg_prompt(p, *rg_args, n_batch=bp, seq=tp, col0=4 * d_ret, d_rg=d_rg,
                                          total_rows=m_p)
            e_prev = jnp.stack(_prev_rows(state_rglru_conv[e], td))
            o_rg_s, h_s = _rg_sample(p, e_prev, _first_rows(state_rglru_h[e], td), *rg_args,
                                     row0=m_p, m_s=m_s, dec_seq=td, col0=4 * d_ret, d_rg=d_rg)
            n_conv = state_rglru_conv.shape[2]
            xr_s = p[m_p:, 4 * d_ret:4 * d_ret + d_rg].reshape(bd, td, d_rg)
            n_h_p.append(hl_p[:, 0])
            n_h_s.append(h_s.reshape(bd, td, d_rg)[:, td - 1])
            n_cv_p.append(xt_p[:, SUBLANES - n_conv:])
            n_cv_s.append(xr_s[:, td - n_conv:])
            x_new, xb = _proj_ln([(o_ret, o_ret_s), (o_rg, o_rg_s)], w_out_ev, e, x, ln_g[l, 0], ln_b[l, 0],
                                 alpha, tm_mix, m_p)
        else:
            o = l // 2
            p3 = _matmul(xb, od_w_in, o, tm_in, IN_PROJ_COLS, column_major=True)
            o_hg, g_p = _gla_prompt(p3, od_lb_logits, od_norm_g[o], layer=o, n_batch=bp, seq=tp,
                                    n_heads=h_hg, total_rows=m_p)
            p4 = jnp.swapaxes(p3[:, m_p:], 0, 1).reshape(bd, td, 4 * h_hg, LANES)
            o_hg_s, hg_s = _gla_sample(p4, od_lb_logits, od_norm_g[o], state_hgrn, layer=o, prev_state=hg_s,
                                       seq0=0)
            n_hg_p.append(g_p)
            x_new, xb = _proj_ln([(o_hg, o_hg_s.reshape(m_s, h_hg * LANES))], w_out_od, o, x, ln_g[l, 0],
                                 ln_b[l, 0], alpha, tm_mix, m_p)
        x = (x_new,)
        h, tails, ff_s = _ffn_up(xb, ffn_w_up, l, ffn_conv_w[l], ffn_conv_b[l], state_ffn_conv, ff_s,
                                 m_prompt=m_p, seq=tp, dec_seq=td, tm=tm_ff, tn=512)
        tiles_per_seq = tp // tm_ff
        tails = tails.reshape(m // tm_ff, SUBLANES, d_ff)[:bp * tiles_per_seq]
        tails = tails.reshape(bp, tiles_per_seq, SUBLANES, d_ff)
        n_ff_p.append(tails[:, tiles_per_seq - 1, SUBLANES - 2:])
        last = l == depth - 1
        x_new, xb = _proj_ln([(h,)], w_down, l, x, ln_g[l, 1], ln_b[l, 1], alpha, tm_ln, m_p, split_out=last)
        x = (x_new,)

    y_prompt = x_new.reshape(bp, tp, d_model)
    y_sample = xb.reshape(bd, td, d_model)
    return (y_prompt, y_sample, ret_p, ret_s, jnp.stack(n_h_p), jnp.stack(n_h_s),
            jnp.stack(n_cv_p), jnp.stack(n_cv_s), jnp.stack(n_hg_p), hg_s,
            jnp.stack(n_ff_p), ff_s)
```
